```python
import jax
import jax.numpy as jnp
from jax import lax
import numpy as np

D_MODEL = 2048
BATCH = 4
SEQ = 2048
DEPTH = 2
DEC_BATCH = 128
DEC_SEQ = 8
PAST_LEN = 2048
PAGE_SIZE = 128

N_BRANCH = 4
MIX_W = D_MODEL // N_BRANCH
HD = 128
H_A = MIX_W // HD
H_B = MIX_W // HD
H_C = 4
GLA_DK = MIX_W // (2 * H_C)
GLA_DV = MIX_W // H_C
GLA_RANK = 16
GLA_TAU = 16.0
H_D = MIX_W // HD
CMP_BLOCK = 32
SEL_BLOCK = 64
N_SEL = 16
WINDOW = 512
Q_BLOCK = 128
CHUNK = 64
EPS = 1e-6
F32 = jnp.float32

PROJ_LAYOUT = (
    ('a_q', MIX_W), ('a_k', MIX_W), ('a_v', MIX_W), ('a_f', H_A), ('a_z', MIX_W),
    ('b_q', MIX_W), ('b_k', MIX_W), ('b_v', MIX_W), ('b_i', H_B), ('b_f', H_B), ('b_o', MIX_W), ('b_z', MIX_W),
    ('c_q', H_C * GLA_DK), ('c_k', H_C * GLA_DK), ('c_v', MIX_W), ('c_a', GLA_RANK), ('c_z', MIX_W),
    ('d_q', MIX_W), ('d_kv', 6 * HD), ('d_g', 3 * H_D), ('d_z', MIX_W),
    ('gate', N_BRANCH * D_MODEL),
)
N_IN = sum(w for _, w in PROJ_LAYOUT)

kernel_name = 'fox_mlstm_gla_nsa_parallel_hybrid_step'


def _rms(x, g):
    xf = x.astype(F32)
    y = xf * lax.rsqrt(jnp.mean(xf * xf, axis=-1, keepdims=True) + EPS)
    return y.astype(x.dtype) * g


def _masked_softmax(s, mask):
    s = jnp.where(mask, s.astype(F32), -jnp.inf)
    m = jnp.max(s, axis=-1, keepdims=True)
    m = jnp.where(jnp.isfinite(m), m, 0.0)
    e = jnp.exp(s - m)
    return e / jnp.maximum(jnp.sum(e, axis=-1, keepdims=True), 1e-30)


def _split_cols(h):
    parts = {}
    off = 0
    for name, width in PROJ_LAYOUT:
        parts[name] = h[..., off:off + width]
        off += width
    return parts


def _to_blocks(a, nb, blk):
    return jnp.moveaxis(a.reshape((a.shape[0], nb, blk) + a.shape[2:]), 1, 0)


def _from_blocks(a):
    a = jnp.moveaxis(a, 0, 1)
    return a.reshape((a.shape[0], a.shape[1] * a.shape[2]) + a.shape[3:])


def _map_query_blocks(body, T, *arrays):
    blk = min(Q_BLOCK, T)
    nb = T // blk
    xs = tuple(_to_blocks(a, nb, blk) for a in arrays)
    starts = jnp.arange(nb, dtype=jnp.int32) * blk
    out = lax.map(lambda args: body(args[0], *args[1:]), (starts,) + xs)
    return _from_blocks(out)


def _chunks(a, nc, L):
    a = a.reshape((a.shape[0], nc, L) + a.shape[2:])
    return jnp.moveaxis(jnp.moveaxis(a, 1, 0), 3, 2)


def _unchunk(a):
    nc, B, H, L, D = a.shape
    return a.transpose(1, 0, 3, 2, 4).reshape(B, nc * L, H, D)


def _fox(c, lp, st, q0):
    B, T = c['a_q'].shape[:2]
    q = _rms(c['a_q'].reshape(B, T, H_A, HD), lp['fox_qg'])
    k = _rms(c['a_k'].reshape(B, T, H_A, HD), lp['fox_kg'])
    v = c['a_v'].reshape(B, T, H_A, HD)
    lf = jax.nn.log_sigmoid((c['a_f'] + lp['fox_bf']).astype(F32))
    k_all = jnp.concatenate([st['fox_k'], k], axis=1)
    v_all = jnp.concatenate([st['fox_v'], v], axis=1)
    F_all = jnp.cumsum(jnp.concatenate([st['fox_lf'].astype(F32), lf], axis=1), axis=1)
    Tk = k_all.shape[1]
    kpos = jnp.arange(Tk)
    Fk = F_all.transpose(0, 2, 1)[:, :, None, :]

    def body(start, qb, fqb):
        blk = qb.shape[1]
        qpos = q0 + start + jnp.arange(blk)
        s = jnp.einsum('bqhd,bkhd->bhqk', qb, k_all).astype(F32) * (HD ** -0.5)
        s = s + fqb.transpose(0, 2, 1)[..., None] - Fk
        p = _masked_softmax(s, kpos[None, :] <= qpos[:, None])
        return jnp.einsum('bhqk,bkhd->bqhd', p.astype(v_all.dtype), v_all)

    o = _map_query_blocks(body, T, q, F_all[:, q0:])
    return o.reshape(B, T, MIX_W), jnp.stack([k, v], axis=2), lf


def _mlstm(c, lp, st):
    B, T = c['b_q'].shape[:2]
    q = c['b_q'].reshape(B, T, H_B, HD).astype(F32)
    k = c['b_k'].reshape(B, T, H_B, HD).astype(F32) * (HD ** -0.5)
    v = c['b_v'].reshape(B, T, H_B, HD).astype(F32)
    ig = (c['b_i'] + lp['mlstm_bi']).astype(F32)
    lf = jax.nn.log_sigmoid((c['b_f'] + lp['mlstm_bf']).astype(F32))
    L = min(CHUNK, T)
    nc = T // L
    causal = jnp.tril(jnp.ones((L, L), dtype=bool))

    def step(carry, xs):
        C, n, m = carry
        qc, kc, vc, ic, fc = xs
        b = jnp.cumsum(fc, axis=-1)
        dmat = jnp.where(causal, b[..., :, None] - b[..., None, :] + ic[..., None, :], -jnp.inf)
        inter = b + m[..., None]
        m_t = jnp.maximum(inter, jnp.max(dmat, axis=-1))
        w_inter = jnp.exp(inter - m_t)
        qk = jnp.einsum('bhtd,bhsd->bhts', qc, kc) * jnp.exp(dmat - m_t[..., None])
        num = w_inter[..., None] * jnp.einsum('bhtd,bhde->bhte', qc, C) + jnp.einsum('bhts,bhse->bhte', qk, vc)
        den = w_inter * jnp.einsum('bhtd,bhd->bht', qc, n) + jnp.sum(qk, axis=-1)
        h = num / jnp.maximum(jnp.abs(den), jnp.exp(-m_t))[..., None]
        g_end = b[..., -1:] - b + ic
        m_new = jnp.maximum(b[..., -1] + m, jnp.max(g_end, axis=-1))
        a_prev = jnp.exp(b[..., -1] + m - m_new)
        w_s = jnp.exp(g_end - m_new[..., None])
        C_new = a_prev[..., None, None] * C + jnp.einsum('bhs,bhsd,bhse->bhde', w_s, kc, vc)
        n_new = a_prev[..., None] * n + jnp.einsum('bhs,bhsd->bhd', w_s, kc)
        return (C_new, n_new, m_new), h

    carry0 = (st['C'].astype(F32), st['n'].astype(F32), st['m'].astype(F32))
    xs = tuple(_chunks(a, nc, L) for a in (q, k, v, ig, lf))
    (C, n, m), h = lax.scan(step, carry0, xs)
    h = _rms(_unchunk(h).astype(c['b_q'].dtype), lp['mlstm_hg']).reshape(B, T, MIX_W)
    return h * jax.nn.sigmoid(c['b_o']), (C, n, m)


def _gla(c, lp, st):
    B, T = c['c_q'].shape[:2]
    q = c['c_q'].reshape(B, T, H_C, GLA_DK).astype(F32) * (GLA_DK ** -0.5)
    k = c['c_k'].reshape(B, T, H_C, GLA_DK).astype(F32)
    v = c['c_v'].reshape(B, T, H_C, GLA_DV).astype(F32)
    la = jax.nn.log_sigmoid((c['c_a'] @ lp['gla_w2'] + lp['gla_b2']).astype(F32)).reshape(B, T, H_C, GLA_DK) / GLA_TAU
    L = min(CHUNK, T)
    nc = T // L
    causal = jnp.tril(jnp.ones((L, L), dtype=bool))[:, :, None]

    def step(S, xs):
        qc, kc, vc, ac = xs
        b = jnp.cumsum(ac, axis=2)
        inter = jnp.einsum('bhtk,bhkv->bhtv', qc * jnp.exp(b), S)
        diff = jnp.where(causal, b[:, :, :, None, :] - b[:, :, None, :, :], -jnp.inf)
        att = jnp.sum(qc[:, :, :, None, :] * kc[:, :, None, :, :] * jnp.exp(diff), axis=-1)
        o = inter + jnp.einsum('bhts,bhsv->bhtv', att, vc)
        b_end = b[:, :, -1:, :]
        S_new = jnp.exp(b_end[:, :, 0, :])[..., None] * S + jnp.einsum('bhsk,bhsv->bhkv', kc * jnp.exp(b_end - b), vc)
        return S_new, o

    S, o = lax.scan(step, st['S'].astype(F32), tuple(_chunks(a, nc, L) for a in (q, k, v, la)))
    o = _rms(_unchunk(o).astype(c['c_q'].dtype), lp['gla_hg']).reshape(B, T, MIX_W)
    return o, S


def _nsa(c, lp, st, q0):
    B, T = c['d_q'].shape[:2]
    q = _rms(c['d_q'].reshape(B, T, H_D, HD), lp['nsa_qg'])
    kv = c['d_kv'].reshape(B, T, 6, HD)
    kc_new, vc_new = kv[:, :, 0], kv[:, :, 1]
    ks_new, vs_new = _rms(kv[:, :, 2], lp['nsa_ksg']), kv[:, :, 3]
    kw_new, vw_new = _rms(kv[:, :, 4], lp['nsa_kwg']), kv[:, :, 5]
    g = jax.nn.sigmoid(c['d_g'].reshape(B, T, H_D, 3) + lp['nsa_bg'])
    kc_all = jnp.concatenate([st['nsa_kc'], kc_new], axis=1)
    vc_all = jnp.concatenate([st['nsa_vc'], vc_new], axis=1)
    ks_all = jnp.concatenate([st['nsa_ks'], ks_new], axis=1)
    vs_all = jnp.concatenate([st['nsa_vs'], vs_new], axis=1)
    Tk = kc_all.shape[1]
    NC = Tk // CMP_BLOCK
    kcmp = _rms(jnp.einsum('bcld,l->bcd', kc_all[:, :NC * CMP_BLOCK].reshape(B, NC, CMP_BLOCK, HD), lp['nsa_wk']), lp['nsa_kcg'])
    vcmp = jnp.einsum('bcld,l->bcd', vc_all[:, :NC * CMP_BLOCK].reshape(B, NC, CMP_BLOCK, HD), lp['nsa_wv'])
    cend = jnp.arange(NC) * CMP_BLOCK + (CMP_BLOCK - 1)
    NS = -(-Tk // SEL_BLOCK)
    padn = NS * SEL_BLOCK - Tk
    ks_blk = jnp.pad(ks_all, ((0, 0), (0, padn), (0, 0))).reshape(B, NS, SEL_BLOCK, HD)
    vs_blk = jnp.pad(vs_all, ((0, 0), (0, padn), (0, 0))).reshape(B, NS, SEL_BLOCK, HD)
    n_sel = min(N_SEL, NS)
    R = SEL_BLOCK // CMP_BLOCK
    jb = jnp.arange(NS)
    bidx = jnp.arange(B)[:, None, None]
    W_past = st['win_k'].shape[1]
    zpad = jnp.zeros((B, WINDOW, HD), kw_new.dtype)
    kw_pad = jnp.concatenate([zpad, st['win_k'], kw_new], axis=1)
    vw_pad = jnp.concatenate([zpad, st['win_v'], vw_new], axis=1)
    scale = HD ** -0.5

    def body(start, qb, gb):
        blk = qb.shape[1]
        qpos = q0 + start + jnp.arange(blk)
        sc = jnp.einsum('bqhd,bcd->bhqc', qb, kcmp).astype(F32) * scale
        pc = _masked_softmax(sc, cend[None, :] <= qpos[:, None])
        o_cmp = jnp.einsum('bhqc,bcd->bqhd', pc.astype(vcmp.dtype), vcmp)
        imp = jnp.pad(jnp.sum(pc, axis=1), ((0, 0), (0, 0), (0, NS * R - NC))).reshape(B, blk, NS, R).sum(-1)
        cur = qpos // SEL_BLOCK
        valid = jb[None, :] <= cur[:, None]
        forced = (jb[None, :] == 0) | (valid & (jb[None, :] >= cur[:, None] - 1))
        imp = jnp.where(forced, 1e4, jnp.where(valid, imp, -1e4))
        _, idx = lax.top_k(imp, n_sel)
        ksel = ks_blk[bidx, idx].reshape(B, blk, n_sel * SEL_BLOCK, HD)
        vsel = vs_blk[bidx, idx].reshape(B, blk, n_sel * SEL_BLOCK, HD)
        spos = (idx[..., None] * SEL_BLOCK + jnp.arange(SEL_BLOCK)).reshape(B, blk, n_sel * SEL_BLOCK)
        ss = jnp.einsum('bqhd,bqkd->bhqk', qb, ksel).astype(F32) * scale
        ps = _masked_softmax(ss, (spos <= qpos[None, :, None])[:, None])
        o_sel = jnp.einsum('bhqk,bqkd->bqhd', ps.astype(vsel.dtype), vsel)
        idx0 = start + W_past
        kw = lax.dynamic_slice_in_dim(kw_pad, idx0, WINDOW + blk, axis=1)
        vw = lax.dynamic_slice_in_dim(vw_pad, idx0, WINDOW + blk, axis=1)
        wpos = q0 + start - WINDOW + jnp.arange(WINDOW + blk)
        sw = jnp.einsum('bqhd,bkd->bhqk', qb, kw).astype(F32) * scale
        wmask = (wpos[None, :] <= qpos[:, None]) & (wpos[None, :] > qpos[:, None] - WINDOW) & (wpos[None, :] >= 0)
        pw = _masked_softmax(sw, wmask)
        o_win = jnp.einsum('bhqk,bkd->bqhd', pw.astype(vw.dtype), vw)
        return gb[..., 0:1] * o_cmp + gb[..., 1:2] * o_sel + gb[..., 2:3] * o_win

    o = _map_query_blocks(body, T, q, g)
    Wn = min(WINDOW, W_past + T)
    new_rows = jnp.stack([kc_new, vc_new, ks_new, vs_new], axis=2)
    new_win = jnp.stack([kw_pad[:, -Wn:], vw_pad[:, -Wn:]], axis=2)
    return o.reshape(B, T, MIX_W), new_rows, new_win


def _layer(x, lp, st):
    B, T, _ = x.shape
    q0 = st['fox_k'].shape[1]
    xn = _rms(x, lp['norm_g'])
    c = _split_cols(xn @ lp['w_in'])
    o_a, fox_kv, fox_lf = _fox(c, lp, st, q0)
    o_b, (C, n, m) = _mlstm(c, lp, st)
    o_c, S = _gla(c, lp, st)
    o_d, nsa_kv, nsa_win = _nsa(c, lp, st, q0)
    br = jnp.stack([o_a * jax.nn.silu(c['a_z']), o_b * jax.nn.silu(c['b_z']),
                    o_c * jax.nn.silu(c['c_z']), o_d * jax.nn.silu(c['d_z'])], axis=2)
    proj = jnp.einsum('btgm,gmd->btgd', br.astype(x.dtype), lp['w_branch'])
    gates = jax.nn.sigmoid(c['gate'].reshape(B, T, N_BRANCH, D_MODEL))
    y = x + jnp.sum(gates * proj, axis=2) @ lp['w_out']
    return y, (fox_kv, fox_lf, nsa_kv, nsa_win, C, n, m, S)


def _stack_layers(outs, i):
    return jnp.stack([o[i] for o in outs], axis=0)


def setup_inputs(seed: int = 0) -> dict:
    key = jax.random.key(seed)
    ks = jax.random.split(key, 32)
    n_pages = PAST_LEN // PAGE_SIZE
    n_pool = (DEC_BATCH * n_pages * 5) // 4
    w_buf = min(WINDOW, PAST_LEN)

    def nrm(i, shape, scale=1.0):
        return scale * jax.random.normal(ks[i], shape, F32)

    def gain(i, shape):
        return 1.0 + 0.01 * jax.random.normal(ks[i], shape, F32)

    page_table = jax.random.permutation(ks[2], n_pool)[:DEC_BATCH * n_pages].reshape(DEC_BATCH, n_pages).astype(jnp.int32)
    return {
        'x_prompt': nrm(0, (BATCH, SEQ, D_MODEL)),
        'x_sample': nrm(1, (DEC_BATCH, DEC_SEQ, D_MODEL)),
        'cache_fox_kv': nrm(3, (DEPTH, n_pool, PAGE_SIZE, 2, H_A, HD)),
        'cache_fox_logf': jax.nn.log_sigmoid(2.0 + nrm(4, (DEPTH, n_pool, PAGE_SIZE, H_A), 0.5)),
        'cache_nsa_kv': nrm(5, (DEPTH, n_pool, PAGE_SIZE, 4, HD)),
        'page_table': page_table,
        'state_nsa_win': nrm(6, (DEPTH, DEC_BATCH, w_buf, 2, HD)),
        'state_mlstm_C': nrm(7, (DEPTH, DEC_BATCH, H_B, HD, HD), 0.05),
        'state_mlstm_n': nrm(8, (DEPTH, DEC_BATCH, H_B, HD), 0.5),
        'state_mlstm_m': nrm(9, (DEPTH, DEC_BATCH, H_B)),
        'state_gla_S': nrm(10, (DEPTH, DEC_BATCH, H_C, GLA_DK, GLA_DV), 0.1),
        'norm_g': gain(11, (DEPTH, D_MODEL)),
        'w_in': nrm(12, (DEPTH, D_MODEL, N_IN), D_MODEL ** -0.5),
        'fox_qg': gain(13, (DEPTH, HD)),
        'fox_kg': gain(14, (DEPTH, HD)),
        'fox_bf': 1.0 + nrm(15, (DEPTH, H_A), 0.1),
        'mlstm_bi': nrm(16, (DEPTH, H_B), 0.1),
        'mlstm_bf': 3.0 + nrm(17, (DEPTH, H_B), 0.1),
        'mlstm_hg': gain(18, (DEPTH, HD)),
        'gla_w2': nrm(19, (DEPTH, GLA_RANK, H_C * GLA_DK), GLA_RANK ** -0.5),
        'gla_b2': nrm(20, (DEPTH, H_C * GLA_DK), 0.1),
        'gla_hg': gain(21, (DEPTH, GLA_DV)),
        'nsa_qg': gain(22, (DEPTH, HD)),
        'nsa_kcg': gain(23, (DEPTH, HD)),
        'nsa_ksg': gain(24, (DEPTH, HD)),
        'nsa_kwg': gain(25, (DEPTH, HD)),
        'nsa_wk': (1.0 + nrm(26, (DEPTH, CMP_BLOCK), 0.1)) / CMP_BLOCK,
        'nsa_wv': (1.0 + nrm(27, (DEPTH, CMP_BLOCK), 0.1)) / CMP_BLOCK,
        'nsa_bg': nrm(28, (DEPTH, H_D, 3), 0.1),
        'w_branch': nrm(29, (DEPTH, N_BRANCH, MIX_W, D_MODEL), MIX_W ** -0.5),
        'w_out': nrm(30, (DEPTH, D_MODEL, D_MODEL), D_MODEL ** -0.5),
    }


def reference(x_prompt, x_sample, cache_fox_kv, cache_fox_logf, cache_nsa_kv, page_table,
              state_nsa_win, state_mlstm_C, state_mlstm_n, state_mlstm_m, state_gla_S,
              norm_g, w_in, fox_qg, fox_kg, fox_bf, mlstm_bi, mlstm_bf, mlstm_hg,
              gla_w2, gla_b2, gla_hg, nsa_qg, nsa_kcg, nsa_ksg, nsa_kwg, nsa_wk, nsa_wv, nsa_bg,
              w_branch, w_out):
    DB, NP = page_table.shape
    P = NP * PAGE_SIZE
    Bp = x_prompt.shape[0]
    dt = x_prompt.dtype
    y_p, y_s = x_prompt, x_sample
    outs_p, outs_s = [], []
    for l in range(DEPTH):
        lp = {'norm_g': norm_g[l], 'w_in': w_in[l], 'fox_qg': fox_qg[l], 'fox_kg': fox_kg[l], 'fox_bf': fox_bf[l],
              'mlstm_bi': mlstm_bi[l], 'mlstm_bf': mlstm_bf[l], 'mlstm_hg': mlstm_hg[l],
              'gla_w2': gla_w2[l], 'gla_b2': gla_b2[l], 'gla_hg': gla_hg[l],
              'nsa_qg': nsa_qg[l], 'nsa_kcg': nsa_kcg[l], 'nsa_ksg': nsa_ksg[l], 'nsa_kwg': nsa_kwg[l],
              'nsa_wk': nsa_wk[l], 'nsa_wv': nsa_wv[l], 'nsa_bg': nsa_bg[l],
              'w_branch': w_branch[l], 'w_out': w_out[l]}
        e_row = jnp.zeros((Bp, 0, HD), dt)
        st_p = {'fox_k': jnp.zeros((Bp, 0, H_A, HD), dt), 'fox_v': jnp.zeros((Bp, 0, H_A, HD), dt),
                'fox_lf': jnp.zeros((Bp, 0, H_A), F32),
                'nsa_kc': e_row, 'nsa_vc': e_row, 'nsa_ks': e_row, 'nsa_vs': e_row,
                'win_k': e_row, 'win_v': e_row,
                'C': jnp.zeros((Bp, H_B, HD, HD), F32), 'n': jnp.zeros((Bp, H_B, HD), F32),
                'm': jnp.zeros((Bp, H_B), F32), 'S': jnp.zeros((Bp, H_C, GLA_DK, GLA_DV), F32)}
        fkv = cache_fox_kv[l][page_table].reshape(DB, P, 2, H_A, HD)
        nkv = cache_nsa_kv[l][page_table].reshape(DB, P, 4, HD)
        st_s = {'fox_k': fkv[:, :, 0], 'fox_v': fkv[:, :, 1],
                'fox_lf': cache_fox_logf[l][page_table].reshape(DB, P, H_A),
                'nsa_kc': nkv[:, :, 0], 'nsa_vc': nkv[:, :, 1], 'nsa_ks': nkv[:, :, 2], 'nsa_vs': nkv[:, :, 3],
                'win_k': state_nsa_win[l][:, :, 0], 'win_v': state_nsa_win[l][:, :, 1],
                'C': state_mlstm_C[l], 'n': state_mlstm_n[l], 'm': state_mlstm_m[l], 'S': state_gla_S[l]}
        y_p, o_p = _layer(y_p, lp, st_p)
        y_s, o_s = _layer(y_s, lp, st_s)
        outs_p.append(o_p)
        outs_s.append(o_s)
    return (y_p, y_s,
            _stack_layers(outs_p, 0), _stack_layers(outs_p, 1), _stack_layers(outs_p, 2), _stack_layers(outs_p, 3),
            _stack_layers(outs_p, 4), _stack_layers(outs_p, 5), _stack_layers(outs_p, 6), _stack_layers(outs_p, 7),
            _stack_layers(outs_s, 0), _stack_layers(outs_s, 1), _stack_layers(outs_s, 2), _stack_layers(outs_s, 3),
            _stack_layers(outs_s, 4), _stack_layers(outs_s, 5), _stack_layers(outs_s, 6), _stack_layers(outs_s, 7))
```

```python
import functools

import jax
import jax.numpy as jnp
from jax import lax
from jax.experimental import pallas as pl
from jax.experimental.pallas import tpu as pltpu

F32 = jnp.float32
BF16 = jnp.bfloat16
HIGHEST = lax.Precision.HIGHEST

D_MODEL = 2048
MIX_W = 512
HD = 128
N_HEADS = 4
GLA_DK = 64
GLA_DV = 128
GLA_RANK = 16
GLA_TAU = 16.0
CMP_BLOCK = 32
SEL_BLOCK = 64
SEL_SHIFT = 6
N_SEL = 16
WINDOW = 512
PAGE_SIZE = 128
EPS = 1e-6
ATT_SCALE = HD ** -0.5
NEG_INF = float("-inf")

_REF_LAYOUT = (
    ("a_q", 512), ("a_k", 512), ("a_v", 512), ("a_f", 4), ("a_z", 512),
    ("b_q", 512), ("b_k", 512), ("b_v", 512), ("b_i", 4), ("b_f", 4), ("b_o", 512), ("b_z", 512),
    ("c_q", 256), ("c_k", 256), ("c_v", 512), ("c_a", 16), ("c_z", 512),
    ("d_q", 512), ("d_kv", 768), ("d_g", 12), ("d_z", 512),
    ("gate", 8192),
)
_MY_LAYOUT = (
    "a_z", "b_z", "c_z", "d_z", "gate",
    "a_q", "a_k", "a_v", "d_q", "b_q", "b_k", "b_v", "b_o", "c_v", "c_q", "c_k", "d_kv",
    "a_f", "b_i", "b_f", "c_a", "d_g",
)
N_PROJ = 16384
COL_Z = 0
COL_GATE = 2048
COL_AQ, COL_AK, COL_AV, COL_DQ = 10240, 10752, 11264, 11776
COL_BQ, COL_BK, COL_BV, COL_BO = 12288, 12800, 13312, 13824
COL_CV, COL_CQ, COL_CK, COL_DKV, COL_SMALL = 14336, 14848, 15104, 15360, 16128
SM_AF, SM_BI, SM_BF, SM_CA, SM_DG = 0, 4, 8, 12, 28


def _cparams(sem, vmem_mb):
    return pltpu.CompilerParams(dimension_semantics=sem, vmem_limit_bytes=vmem_mb * 1024 * 1024)


def _log_sigmoid(x):
    return jnp.minimum(x, 0.0) - jnp.log(1.0 + jnp.exp(-jnp.abs(x)))


def _sigmoid(x):
    return 1.0 / (1.0 + jnp.exp(-x))


def _rms_lanes(x, g):
    return (x * lax.rsqrt(jnp.mean(x * x, axis=-1, keepdims=True) + EPS)) * g


def _dot(a, b):
    return jnp.dot(a, b, preferred_element_type=F32)


def _dot_nt(a, b):
    return lax.dot_general(a, b, (((1,), (1,)), ((), ())), preferred_element_type=F32)


def _dot_tn(a, b):
    return lax.dot_general(a, b, (((0,), (0,)), ((), ())), preferred_element_type=F32)


def _dot_exact(a, b):
    return jnp.dot(a, b, precision=HIGHEST, preferred_element_type=F32)


def _mxu_cast(rows):
    if rows >= 16:
        return lambda x: x.astype(BF16)
    return lambda x: x


def _eye_mask(n):
    return lax.broadcasted_iota(jnp.int32, (n, n), 0) == lax.broadcasted_iota(jnp.int32, (n, n), 1)


def _col_to_row(c):
    n = c.shape[0]
    return jnp.sum(jnp.where(_eye_mask(n), c, 0.0), axis=0, keepdims=True)


def _row_to_col(r):
    n = r.shape[1]
    return jnp.sum(jnp.where(_eye_mask(n), r, 0.0), axis=1, keepdims=True)


def _tril_ones(n):
    r = lax.broadcasted_iota(jnp.int32, (n, n), 0)
    c = lax.broadcasted_iota(jnp.int32, (n, n), 1)
    return jnp.where(c <= r, 1.0, 0.0).astype(F32)


def _cumsum_lanes(x):
    n = x.shape[-1]
    lane = lax.broadcasted_iota(jnp.int32, x.shape, x.ndim - 1)
    s = 1
    while s < n:
        x = x + jnp.where(lane >= s, pltpu.roll(x, s, axis=x.ndim - 1), 0.0)
        s *= 2
    return x


def _softmax_parts(s_list, mask_list):
    masked = [jnp.where(mk, s, NEG_INF) for s, mk in zip(s_list, mask_list)]
    m = masked[0].max(axis=-1, keepdims=True)
    for s in masked[1:]:
        m = jnp.maximum(m, s.max(axis=-1, keepdims=True))
    m = jnp.where(m > NEG_INF, m, 0.0)
    es = [jnp.exp(s - m) for s in masked]
    tot = es[0].sum(axis=-1, keepdims=True)
    for e in es[1:]:
        tot = tot + e.sum(axis=-1, keepdims=True)
    return es, 1.0 / jnp.maximum(tot, 1e-30)


def _norm_kernel(x_ref, g_ref, o_ref):
    o_ref[...] = _rms_lanes(x_ref[...], g_ref[...]).astype(o_ref.dtype)


def _rmsnorm(x2d, g):
    m, d = x2d.shape
    tm = min(512, m)
    return pl.pallas_call(
        _norm_kernel,
        grid=(m // tm,),
        in_specs=[pl.BlockSpec((tm, d), lambda i: (i, 0)), pl.BlockSpec((1, d), lambda i: (0, 0))],
        out_specs=pl.BlockSpec((tm, d), lambda i: (i, 0)),
        out_shape=jax.ShapeDtypeStruct((m, d), BF16),
        compiler_params=_cparams(("parallel",), 32),
        name="rmsnorm",
    )(x2d, g.reshape(1, d))


def _mm_kernel(a_ref, b_ref, o_ref):
    o_ref[...] = _dot(a_ref[...], b_ref[...])


def _matmul(a, b):
    m, k = a.shape
    _, n = b.shape
    tm, tn = min(1024, m), 1024
    return pl.pallas_call(
        _mm_kernel,
        grid=(m // tm, n // tn),
        in_specs=[pl.BlockSpec((tm, k), lambda i, j: (i, 0)), pl.BlockSpec((k, tn), lambda i, j: (0, j))],
        out_specs=pl.BlockSpec((tm, tn), lambda i, j: (i, j)),
        out_shape=jax.ShapeDtypeStruct((m, n), F32),
        compiler_params=_cparams(("parallel", "arbitrary"), 48),
        name="in_proj",
    )(a, b)


PREP_TM = 256


def _prep_kernel(aq_ref, ak_ref, av_ref, dq_ref, dkv_ref, sm_ref, gains_ref, bias_ref, wk_ref, wv_ref,
                 foxkv_ref, nsakv_ref, win_ref, so_ref, fq_ref, fk_ref, fv_ref, nq_ref, nkv_ref,
                 kcmp_ref, vcmp_ref):
    g_fq, g_fk, g_nq = gains_ref[0:1, :], gains_ref[1:2, :], gains_ref[2:3, :]
    g_ks, g_kw, g_kc = gains_ref[3:4, :], gains_ref[4:5, :], gains_ref[5:6, :]
    for h in range(N_HEADS):
        sl = slice(h * HD, (h + 1) * HD)
        fq_ref[:, sl] = _rms_lanes(aq_ref[:, sl], g_fq).astype(BF16)
        kn = _rms_lanes(ak_ref[:, sl], g_fk)
        foxkv_ref[:, sl] = kn
        fk_ref[:, sl] = kn.astype(BF16)
        v = av_ref[:, sl]
        foxkv_ref[:, MIX_W + h * HD:MIX_W + (h + 1) * HD] = v
        fv_ref[:, sl] = v.astype(BF16)
        nq_ref[:, sl] = _rms_lanes(dq_ref[:, sl], g_nq).astype(BF16)
    kc = dkv_ref[:, 0:128]
    vc = dkv_ref[:, 128:256]
    ks = _rms_lanes(dkv_ref[:, 256:384], g_ks)
    vs = dkv_ref[:, 384:512]
    kw = _rms_lanes(dkv_ref[:, 512:640], g_kw)
    vw = dkv_ref[:, 640:768]
    for j, a in enumerate((kc, vc, ks, vs)):
        nsakv_ref[:, j * HD:(j + 1) * HD] = a
    win_ref[:, 0:HD] = kw
    win_ref[:, HD:2 * HD] = vw
    for j, a in enumerate((kc, vc, ks, vs, kw, vw)):
        nkv_ref[:, j * HD:(j + 1) * HD] = a.astype(BF16)
    nb = PREP_TM // CMP_BLOCK
    kcs = jnp.sum(kc.reshape(nb, CMP_BLOCK, HD) * wk_ref[...][None], axis=1)
    kcmp_ref[...] = _rms_lanes(kcs, g_kc)
    vcmp_ref[...] = jnp.sum(vc.reshape(nb, CMP_BLOCK, HD) * wv_ref[...][None], axis=1)
    y = sm_ref[...] + bias_ref[...]
    lane = lax.broadcasted_iota(jnp.int32, y.shape, 1)
    is_ls = (lane < SM_BI) | ((lane >= SM_BF) & (lane < SM_CA))
    is_sg = (lane >= SM_DG) & (lane < SM_DG + 12)
    so_ref[...] = jnp.where(is_ls, _log_sigmoid(y), jnp.where(is_sg, _sigmoid(y), y))


def _prep(h2d, gains, bias, wk_b, wv_b):
    m = h2d.shape[0]
    tm = PREP_TM

    def col(width, off):
        blk = off // width
        return pl.BlockSpec((tm, width), lambda i, blk=blk: (i, blk))

    def full(shape):
        return pl.BlockSpec(shape, lambda i: (0, 0))

    def rows(width):
        return pl.BlockSpec((tm, width), lambda i: (i, 0))

    out_shapes = (
        jax.ShapeDtypeStruct((m, 1024), F32),
        jax.ShapeDtypeStruct((m, 512), F32),
        jax.ShapeDtypeStruct((m, 256), F32),
        jax.ShapeDtypeStruct((m, 128), F32),
        jax.ShapeDtypeStruct((m, 512), BF16),
        jax.ShapeDtypeStruct((m, 512), BF16),
        jax.ShapeDtypeStruct((m, 512), BF16),
        jax.ShapeDtypeStruct((m, 512), BF16),
        jax.ShapeDtypeStruct((m, 768), BF16),
        jax.ShapeDtypeStruct((m // CMP_BLOCK, 128), F32),
        jax.ShapeDtypeStruct((m // CMP_BLOCK, 128), F32),
    )
    out_specs = (rows(1024), rows(512), rows(256), rows(128), rows(512), rows(512), rows(512), rows(512),
                 rows(768),
                 pl.BlockSpec((tm // CMP_BLOCK, 128), lambda i: (i, 0)),
                 pl.BlockSpec((tm // CMP_BLOCK, 128), lambda i: (i, 0)))
    return pl.pallas_call(
        _prep_kernel,
        grid=(m // tm,),
        in_specs=[col(512, COL_AQ), col(512, COL_AK), col(512, COL_AV), col(512, COL_DQ),
                  col(768, COL_DKV), col(128, COL_SMALL),
                  full((8, 128)), full((1, 128)), full((CMP_BLOCK, 128)), full((CMP_BLOCK, 128))],
        out_specs=out_specs,
        out_shape=out_shapes,
        compiler_params=_cparams(("parallel",), 32),
        name="prep",
    )(h2d, h2d, h2d, h2d, h2d, h2d, gains, bias, wk_b, wv_b)


def _cumsum_rows_kernel(x_ref, o_ref):
    o_ref[...] = _cumsum_lanes(x_ref[...])


def _cumsum_rows(x):
    r, n = x.shape
    return pl.pallas_call(
        _cumsum_rows_kernel,
        grid=(1,),
        in_specs=[pl.BlockSpec((r, n), lambda i: (0, 0))],
        out_specs=pl.BlockSpec((r, n), lambda i: (0, 0)),
        out_shape=jax.ShapeDtypeStruct((r, n), F32),
        name="fox_cumsum",
    )(x)


FOX_TQ = 256


def _fox_prompt_kernel(q_ref, k_ref, v_ref, fc_ref, fr_ref, o_ref, *, t_len):
    tq = FOX_TQ
    i = pl.program_id(1)
    qpos = i * tq + lax.broadcasted_iota(jnp.int32, (tq, t_len), 0)
    kpos = lax.broadcasted_iota(jnp.int32, (tq, t_len), 1)
    mask = kpos <= qpos
    for h in range(N_HEADS):
        sl = slice(h * HD, (h + 1) * HD)
        s = _dot_nt(q_ref[:, sl], k_ref[:, sl]) * ATT_SCALE
        s = s + fc_ref[:, h:h + 1] - fr_ref[h:h + 1, :]
        (e,), inv = _softmax_parts([s], [mask])
        o_ref[:, sl] = _dot(e.astype(BF16), v_ref[:, sl]) * inv


def _fox_prompt(fq, fk, fv, f_col, f_row):
    b, t, _ = fq.shape
    tq = FOX_TQ
    return pl.pallas_call(
        functools.partial(_fox_prompt_kernel, t_len=t),
        grid=(b, t // tq),
        in_specs=[pl.BlockSpec((None, tq, MIX_W), lambda bi, i: (bi, i, 0)),
                  pl.BlockSpec((None, t, MIX_W), lambda bi, i: (bi, 0, 0)),
                  pl.BlockSpec((None, t, MIX_W), lambda bi, i: (bi, 0, 0)),
                  pl.BlockSpec((None, tq, N_HEADS), lambda bi, i: (bi, i, 0)),
                  pl.BlockSpec((None, N_HEADS, t), lambda bi, i: (bi, 0, 0))],
        out_specs=pl.BlockSpec((None, tq, MIX_W), lambda bi, i: (bi, i, 0)),
        out_shape=jax.ShapeDtypeStruct((b, t, MIX_W), F32),
        compiler_params=_cparams(("parallel", "arbitrary"), 48),
        name="fox_prompt",
    )(fq, fk, fv, f_col, f_row)


def _fox_sample_kernel(pt_ref, q_ref, kvn_ref, so_ref, *rest, n_pages, t_new):
    del pt_ref
    kv_pages = rest[:n_pages]
    lf_pages = rest[n_pages:2 * n_pages]
    o_ref = rest[2 * n_pages]
    past = n_pages * PAGE_SIZE
    carry = jnp.zeros((N_HEADS, 1), F32)
    f_parts = []
    for p in range(n_pages):
        c = _cumsum_lanes(lf_pages[p][...]) + carry
        f_parts.append(c)
        carry = c[:, PAGE_SIZE - 1:PAGE_SIZE]
    f_new_all = _dot_exact(_tril_ones(t_new), so_ref[...])
    tpos = lax.broadcasted_iota(jnp.int32, (t_new, PAGE_SIZE), 0) + past
    kpos0 = lax.broadcasted_iota(jnp.int32, (t_new, PAGE_SIZE), 1)
    rr = lax.broadcasted_iota(jnp.int32, (t_new, t_new), 0)
    cc = lax.broadcasted_iota(jnp.int32, (t_new, t_new), 1)
    for h in range(N_HEADS):
        sl = slice(h * HD, (h + 1) * HD)
        q = q_ref[:, sl]
        fq = f_new_all[:, SM_AF + h:SM_AF + h + 1] + carry[h:h + 1, :]
        fk_new = _col_to_row(fq)
        s_list, m_list, v_list = [], [], []
        for p in range(n_pages):
            kp = kv_pages[p][pl.ds(h, PAGE_SIZE, stride=2 * N_HEADS), :].astype(BF16)
            s = _dot_nt(q, kp) * ATT_SCALE + fq - f_parts[p][h:h + 1, :]
            s_list.append(s)
            m_list.append(kpos0 + p * PAGE_SIZE <= tpos)
        k_new = kvn_ref[:, sl]
        s_new = _dot_nt(q.astype(F32), k_new) * ATT_SCALE + fq - fk_new
        s_list.append(s_new)
        m_list.append(cc <= rr)
        es, inv = _softmax_parts(s_list, m_list)
        acc = _dot(es[n_pages], kvn_ref[:, MIX_W + h * HD:MIX_W + (h + 1) * HD])
        for p in range(n_pages):
            vp = kv_pages[p][pl.ds(N_HEADS + h, PAGE_SIZE, stride=2 * N_HEADS), :].astype(BF16)
            acc = acc + _dot(es[p].astype(BF16), vp)
        o_ref[:, sl] = acc * inv


def _fox_sample(layer, page_table, fq, foxkv_new, so, cache_kv4, cache_lf4):
    b, t, _ = fq.shape
    n_pages = page_table.shape[1]

    def page_spec(rows, p):
        return pl.BlockSpec((None, None, rows, 128), lambda bi, pt, p=p: (layer, pt[bi, p], 0, 0))

    in_specs = [pl.BlockSpec((None, t, MIX_W), lambda bi, pt: (bi, 0, 0)),
                pl.BlockSpec((None, t, 2 * MIX_W), lambda bi, pt: (bi, 0, 0)),
                pl.BlockSpec((None, t, 128), lambda bi, pt: (bi, 0, 0))]
    in_specs += [page_spec(PAGE_SIZE * 2 * N_HEADS, p) for p in range(n_pages)]
    in_specs += [page_spec(N_HEADS, p) for p in range(n_pages)]
    grid_spec = pltpu.PrefetchScalarGridSpec(
        num_scalar_prefetch=1, grid=(b,), in_specs=in_specs,
        out_specs=pl.BlockSpec((None, t, MIX_W), lambda bi, pt: (bi, 0, 0)))
    return pl.pallas_call(
        functools.partial(_fox_sample_kernel, n_pages=n_pages, t_new=t),
        grid_spec=grid_spec,
        out_shape=jax.ShapeDtypeStruct((b, t, MIX_W), F32),
        compiler_params=_cparams(("arbitrary",), 48),
        name="fox_sample",
    )(page_table, fq, foxkv_new, so, *([cache_kv4] * n_pages), *([cache_lf4] * n_pages))


def _mlstm_kernel(q_ref, k_ref, v_ref, og_ref, so_ref, hg_ref, c0_ref, n0_ref, m0_ref,
                  o_ref, c_ref, n_ref, m_ref, *, bb, chunk):
    L = chunk

    @pl.when(pl.program_id(1) == 0)
    def _():
        c_ref[...] = c0_ref[...]
        n_ref[...] = n0_ref[...]
        m_ref[...] = m0_ref[...]

    tril = _tril_ones(L)
    causal = tril > 0.5
    hg = hg_ref[...]
    cast = _mxu_cast(L)

    def body(bi, carry):
        so = so_ref[bi]
        csum = _dot_exact(tril, so)
        for h in range(N_HEADS):
            sl = slice(h * HD, (h + 1) * HD)
            q = q_ref[bi, :, sl]
            k = k_ref[bi, :, sl] * ATT_SCALE
            v = v_ref[bi, :, sl]
            ig = so[:, SM_BI + h:SM_BI + h + 1]
            bcol = csum[:, SM_BF + h:SM_BF + h + 1]
            brow = _col_to_row(bcol)
            irow = _col_to_row(ig)
            c_st = c_ref[bi, h]
            n_st = n_ref[bi, h:h + 1, :]
            m_st = m_ref[bi, h:h + 1, 0:1]
            dmat = jnp.where(causal, bcol - brow + irow, NEG_INF)
            inter = bcol + m_st
            m_t = jnp.maximum(inter, dmat.max(axis=-1, keepdims=True))
            w_inter = jnp.exp(inter - m_t)
            qb = cast(q)
            vb = cast(v)
            qk = _dot_nt(qb, cast(k)) * jnp.exp(dmat - m_t)
            num = w_inter * _dot(qb, cast(c_st)) + _dot(cast(qk), vb)
            den = w_inter * jnp.sum(q * n_st, axis=-1, keepdims=True) + jnp.sum(qk, axis=-1, keepdims=True)
            hout = num / jnp.maximum(jnp.abs(den), jnp.exp(-m_t))
            b_last = bcol[L - 1:L, :]
            g_end = b_last - bcol + ig
            m_new = jnp.maximum(b_last + m_st, g_end.max(axis=0, keepdims=True))
            a_prev = jnp.exp(b_last + m_st - m_new)
            kw = k * jnp.exp(g_end - m_new)
            c_ref[bi, h] = a_prev * c_st + _dot_tn(cast(kw), vb)
            n_ref[bi, h:h + 1, :] = a_prev * n_st + jnp.sum(kw, axis=0, keepdims=True)
            m_ref[bi, h:h + 1, :] = jnp.broadcast_to(m_new, (1, HD))
            hn = _rms_lanes(hout, hg)
            o_ref[bi, :, sl] = hn * _sigmoid(og_ref[bi, :, sl])
        return carry

    lax.fori_loop(0, bb, body, 0)


def _mlstm(h3d, so3d, hg, c0, n0, m0, bb, chunk):
    b, t, _ = h3d.shape
    nc = t // chunk

    def col(off):
        blk = off // MIX_W
        return pl.BlockSpec((bb, chunk, MIX_W), lambda bi, c, blk=blk: (bi, c, blk))

    st_c = pl.BlockSpec((bb, N_HEADS, HD, HD), lambda bi, c: (bi, 0, 0, 0))
    st_n = pl.BlockSpec((bb, N_HEADS, HD), lambda bi, c: (bi, 0, 0))
    return pl.pallas_call(
        functools.partial(_mlstm_kernel, bb=bb, chunk=chunk),
        grid=(b // bb, nc),
        in_specs=[col(COL_BQ), col(COL_BK), col(COL_BV), col(COL_BO),
                  pl.BlockSpec((bb, chunk, 128), lambda bi, c: (bi, c, 0)),
                  pl.BlockSpec((1, HD), lambda bi, c: (0, 0)),
                  st_c, st_n, st_n],
        out_specs=(pl.BlockSpec((bb, chunk, MIX_W), lambda bi, c: (bi, c, 0)), st_c, st_n, st_n),
        out_shape=(jax.ShapeDtypeStruct((b, t, MIX_W), F32),
                   jax.ShapeDtypeStruct((b, N_HEADS, HD, HD), F32),
                   jax.ShapeDtypeStruct((b, N_HEADS, HD), F32),
                   jax.ShapeDtypeStruct((b, N_HEADS, HD), F32)),
        compiler_params=_cparams(("parallel", "arbitrary"), 32),
        name="mlstm",
    )(h3d, h3d, h3d, h3d, so3d, hg.reshape(1, HD), c0, n0, m0)


def _gla_kernel(q_ref, k_ref, v_ref, so_ref, w2_ref, b2_ref, hg_ref, s0_ref, o_ref, s_ref, *, bb, chunk, sub):
    L = chunk

    @pl.when(pl.program_id(1) == 0)
    def _():
        s_ref[...] = s0_ref[...]

    tril = _tril_ones(L)
    hg = hg_ref[...]
    cast = _mxu_cast(L)
    w2 = cast(w2_ref[...])
    b2 = b2_ref[...]
    q_scale = GLA_DK ** -0.5

    def body(bi, carry):
        pre = _dot(cast(so_ref[bi]), w2) + b2
        la = _log_sigmoid(pre) / GLA_TAU
        csum = _dot_exact(tril, la)
        for h in range(N_HEADS):
            ksl = slice(h * GLA_DK, (h + 1) * GLA_DK)
            q = q_ref[bi, :, ksl] * q_scale
            k = k_ref[bi, :, ksl]
            v = v_ref[bi, :, h * GLA_DV:(h + 1) * GLA_DV]
            bh = csum[:, ksl]
            s_st = s_ref[bi, h]
            vb = cast(v)
            o = _dot(cast(q * jnp.exp(bh)), cast(s_st))
            for i in range(L // sub):
                r0 = i * sub
                hi = r0 + sub
                base = bh[r0 - 1:r0, :] if i > 0 else jnp.zeros((1, GLA_DK), F32)
                qi = q[r0:hi, :] * jnp.exp(bh[r0:hi, :] - base)
                ke = k[0:hi, :] * jnp.exp(base - bh[0:hi, :])
                a = _dot_nt(cast(qi), cast(ke))
                rr = lax.broadcasted_iota(jnp.int32, (sub, hi), 0) + r0
                cc = lax.broadcasted_iota(jnp.int32, (sub, hi), 1)
                a = jnp.where(cc <= rr, a, 0.0)
                oi = o[r0:hi, :] + _dot(cast(a), vb[0:hi, :])
                o_ref[bi, r0:hi, h * GLA_DV:(h + 1) * GLA_DV] = _rms_lanes(oi, hg)
            b_end = bh[L - 1:L, :]
            kd = k * jnp.exp(b_end - bh)
            s_ref[bi, h] = _row_to_col(jnp.exp(b_end)) * s_st + _dot_tn(cast(kd), vb)
        return carry

    lax.fori_loop(0, bb, body, 0)


def _gla(h3d, so3d, w2pad, b2, hg, s0, bb, chunk):
    b, t, _ = h3d.shape
    nc = t // chunk
    sub = min(16, chunk)
    st = pl.BlockSpec((bb, N_HEADS, GLA_DK, GLA_DV), lambda bi, c: (bi, 0, 0, 0))
    return pl.pallas_call(
        functools.partial(_gla_kernel, bb=bb, chunk=chunk, sub=sub),
        grid=(b // bb, nc),
        in_specs=[pl.BlockSpec((bb, chunk, 256), lambda bi, c: (bi, c, COL_CQ // 256)),
                  pl.BlockSpec((bb, chunk, 256), lambda bi, c: (bi, c, COL_CK // 256)),
                  pl.BlockSpec((bb, chunk, MIX_W), lambda bi, c: (bi, c, COL_CV // MIX_W)),
                  pl.BlockSpec((bb, chunk, 128), lambda bi, c: (bi, c, 0)),
                  pl.BlockSpec((128, 256), lambda bi, c: (0, 0)),
                  pl.BlockSpec((1, 256), lambda bi, c: (0, 0)),
                  pl.BlockSpec((1, GLA_DV), lambda bi, c: (0, 0)),
                  st],
        out_specs=(pl.BlockSpec((bb, chunk, MIX_W), lambda bi, c: (bi, c, 0)), st),
        out_shape=(jax.ShapeDtypeStruct((b, t, MIX_W), F32),
                   jax.ShapeDtypeStruct((b, N_HEADS, GLA_DK, GLA_DV), F32)),
        compiler_params=_cparams(("parallel", "arbitrary"), 32),
        name="gla",
    )(h3d, h3d, h3d, so3d, w2pad, b2.reshape(1, 256), hg.reshape(1, GLA_DV), s0)


def _select_blocks(imp, qpos_col, ns):
    r = imp.shape[0]
    j = lax.broadcasted_iota(jnp.int32, (r, 128), 1)
    cur = qpos_col >> SEL_SHIFT
    valid = j <= cur
    forced = (j == 0) | (valid & (j >= cur - 1))
    val = jnp.where(forced, 1e4, jnp.where(valid, imp, -1e4))
    val = jnp.where(j < ns, val, -3e38)
    rank = jnp.zeros((r, 128), F32)
    for i in range(ns):
        ci = val[:, i:i + 1]
        beats = (ci > val) | ((ci == val) & (j > i))
        rank = rank + jnp.where(beats, 1.0, 0.0)
    return (rank < float(min(N_SEL, ns))) & (j < ns)


def _pair_matrix(nc):
    c = jnp.arange(nc)[:, None]
    j = jnp.arange(128)[None, :]
    return (c // (SEL_BLOCK // CMP_BLOCK) == j).astype(F32)


def _expand_matrix(tk):
    j = jnp.arange(128)[:, None]
    s = jnp.arange(tk)[None, :]
    return (s // SEL_BLOCK == j).astype(BF16)


NSA_TQ = 128


def _nsa_prompt_kernel(q_ref, kcmp_ref, vcmp_ref, kv_ref, win_ref, so_ref, e_ref, pair_ref, o_ref, *, t_len):
    tq = NSA_TQ
    nc = t_len // CMP_BLOCK
    ns = -(-t_len // SEL_BLOCK)
    i = pl.program_id(1)
    start = pl.multiple_of(i * tq, tq)
    qpos_col = start + lax.broadcasted_iota(jnp.int32, (tq, 1), 0)
    so = so_ref[...]
    q_all = jnp.concatenate([q_ref[:, h * HD:(h + 1) * HD] for h in range(N_HEADS)], axis=0)
    sc = _dot_nt(q_all, kcmp_ref[...].astype(BF16)) * ATT_SCALE
    qpos4 = start + (lax.broadcasted_iota(jnp.int32, (N_HEADS * tq, 1), 0) & (tq - 1))
    cend = lax.broadcasted_iota(jnp.int32, (N_HEADS * tq, nc), 1) * CMP_BLOCK + (CMP_BLOCK - 1)
    (ec,), invc = _softmax_parts([sc], [cend <= qpos4])
    pc = ec * invc
    o_cmp = _dot(pc.astype(BF16), vcmp_ref[...].astype(BF16))
    imp_c = pc[0:tq]
    for h in range(1, N_HEADS):
        imp_c = imp_c + pc[h * tq:(h + 1) * tq]
    imp = _dot_exact(imp_c, pair_ref[...])
    sel = _select_blocks(imp, qpos_col, ns)
    sel_keys = _dot(jnp.where(sel, 1.0, 0.0).astype(BF16), e_ref[...]) > 0.5
    kpos = lax.broadcasted_iota(jnp.int32, (tq, t_len), 1)
    smask = sel_keys & (kpos <= qpos_col)
    band = WINDOW + tq
    kwin = win_ref[pl.ds(start, band), 0:HD]
    vwin = win_ref[pl.ds(start, band), HD:2 * HD]
    wpos = start - WINDOW + lax.broadcasted_iota(jnp.int32, (tq, band), 1)
    wmask = (wpos <= qpos_col) & (wpos > qpos_col - WINDOW) & (wpos >= 0)
    ks = kv_ref[:, 2 * HD:3 * HD]
    vs = kv_ref[:, 3 * HD:4 * HD]
    for h in range(N_HEADS):
        sl = slice(h * HD, (h + 1) * HD)
        q = q_ref[:, sl]
        (es,), invs = _softmax_parts([_dot_nt(q, ks) * ATT_SCALE], [smask])
        o_sel = _dot(es.astype(BF16), vs) * invs
        (ew,), invw = _softmax_parts([_dot_nt(q, kwin) * ATT_SCALE], [wmask])
        o_win = _dot(ew.astype(BF16), vwin) * invw
        g0 = so[:, SM_DG + 3 * h:SM_DG + 3 * h + 1]
        g1 = so[:, SM_DG + 3 * h + 1:SM_DG + 3 * h + 2]
        g2 = so[:, SM_DG + 3 * h + 2:SM_DG + 3 * h + 3]
        o_ref[:, sl] = g0 * o_cmp[h * tq:(h + 1) * tq] + g1 * o_sel + g2 * o_win


def _nsa_prompt(nq, kcmp, vcmp, nkv, win_pad, so):
    b, t, _ = nq.shape
    tq = NSA_TQ
    nc = t // CMP_BLOCK
    e_mat = _expand_matrix(t)
    pair = _pair_matrix(nc)
    return pl.pallas_call(
        functools.partial(_nsa_prompt_kernel, t_len=t),
        grid=(b, t // tq),
        in_specs=[pl.BlockSpec((None, tq, MIX_W), lambda bi, i: (bi, i, 0)),
                  pl.BlockSpec((None, nc, HD), lambda bi, i: (bi, 0, 0)),
                  pl.BlockSpec((None, nc, HD), lambda bi, i: (bi, 0, 0)),
                  pl.BlockSpec((None, t, 768), lambda bi, i: (bi, 0, 0)),
                  pl.BlockSpec((None, t + WINDOW, 256), lambda bi, i: (bi, 0, 0)),
                  pl.BlockSpec((None, tq, 128), lambda bi, i: (bi, i, 0)),
                  pl.BlockSpec((128, t), lambda bi, i: (0, 0)),
                  pl.BlockSpec((nc, 128), lambda bi, i: (0, 0))],
        out_specs=pl.BlockSpec((None, tq, MIX_W), lambda bi, i: (bi, i, 0)),
        out_shape=jax.ShapeDtypeStruct((b, t, MIX_W), F32),
        compiler_params=_cparams(("parallel", "arbitrary"), 48),
        name="nsa_prompt",
    )(nq, kcmp, vcmp, nkv, win_pad, so, e_mat, pair)


def _nsa_sample_kernel(pt_ref, q_ref, kvn_ref, winn_ref, so_ref, win_ref, wk_ref, wv_ref, kcg_ref,
                       e_ref, pair_ref, *rest, n_pages, t_new):
    del pt_ref
    pages = rest[:n_pages]
    o_ref = rest[n_pages]
    cmp_k, cmp_v = rest[n_pages + 1], rest[n_pages + 2]
    past = n_pages * PAGE_SIZE
    nc = past // CMP_BLOCK
    ns = -(-(past + t_new) // SEL_BLOCK)
    per_page = PAGE_SIZE // CMP_BLOCK
    rows = N_HEADS * t_new
    so = so_ref[...]

    def page_rows(p, j):
        return pages[p][pl.ds(j, PAGE_SIZE, stride=4), :]

    for p in range(n_pages):
        kc = page_rows(p, 0).reshape(per_page, CMP_BLOCK, HD)
        vc = page_rows(p, 1).reshape(per_page, CMP_BLOCK, HD)
        cmp_k[p * per_page:(p + 1) * per_page, :] = jnp.sum(kc * wk_ref[...][None], axis=1)
        cmp_v[p * per_page:(p + 1) * per_page, :] = jnp.sum(vc * wv_ref[...][None], axis=1)
    kcmp = _rms_lanes(cmp_k[...], kcg_ref[...]).astype(BF16)
    vcmp = cmp_v[...].astype(BF16)

    q_all = jnp.concatenate([q_ref[:, h * HD:(h + 1) * HD].astype(F32) for h in range(N_HEADS)], axis=0)
    q_b = q_all.astype(BF16)
    trow = lax.broadcasted_iota(jnp.int32, (rows, 1), 0) & (t_new - 1)
    qpos4 = past + trow
    qpos_col = past + lax.broadcasted_iota(jnp.int32, (t_new, 1), 0)
    sc = _dot_nt(q_b, kcmp) * ATT_SCALE
    cend = lax.broadcasted_iota(jnp.int32, (rows, nc), 1) * CMP_BLOCK + (CMP_BLOCK - 1)
    (ec,), invc = _softmax_parts([sc], [cend <= qpos4])
    pc = ec * invc
    o_cmp = _dot(pc.astype(BF16), vcmp)
    imp_c = pc[0:t_new]
    for h in range(1, N_HEADS):
        imp_c = imp_c + pc[h * t_new:(h + 1) * t_new]
    imp = _dot_exact(imp_c, pair_ref[...])
    sel = _select_blocks(imp, qpos_col, ns)
    self32 = jnp.where(sel, 1.0, 0.0)
    sel_past = _dot(self32.astype(BF16), e_ref[...])
    sel_past4 = jnp.concatenate([sel_past] * N_HEADS, axis=0) > 0.5
    new_blk = past // SEL_BLOCK
    sel_new = jnp.concatenate([self32[:, new_blk:new_blk + 1]] * N_HEADS, axis=0) > 0.5
    kvn = kvn_ref[...]
    s_list, m_list = [], []
    for p in range(n_pages):
        ksp = page_rows(p, 2).astype(BF16)
        s_list.append(_dot_nt(q_b, ksp) * ATT_SCALE)
        kpos = p * PAGE_SIZE + lax.broadcasted_iota(jnp.int32, (rows, PAGE_SIZE), 1)
        m_list.append(sel_past4[:, p * PAGE_SIZE:(p + 1) * PAGE_SIZE] & (kpos <= qpos4))
    npos = past + lax.broadcasted_iota(jnp.int32, (rows, t_new), 1)
    causal_new = npos <= qpos4
    s_list.append(_dot_nt(q_all, kvn[:, 2 * HD:3 * HD]) * ATT_SCALE)
    m_list.append(sel_new & causal_new)
    es, invs = _softmax_parts(s_list, m_list)
    o_sel = _dot(es[n_pages], kvn[:, 3 * HD:4 * HD])
    for p in range(n_pages):
        o_sel = o_sel + _dot(es[p].astype(BF16), page_rows(p, 3).astype(BF16))
    o_sel = o_sel * invs
    w_buf = win_ref.shape[0] // 2
    kwb = win_ref[pl.ds(0, w_buf, stride=2), :].astype(BF16)
    vwb = win_ref[pl.ds(1, w_buf, stride=2), :].astype(BF16)
    wpos = past - w_buf + lax.broadcasted_iota(jnp.int32, (rows, w_buf), 1)
    wmask = (wpos <= qpos4) & (wpos > qpos4 - WINDOW) & (wpos >= 0)
    wnew = winn_ref[...]
    sw_list = [_dot_nt(q_b, kwb) * ATT_SCALE, _dot_nt(q_all, wnew[:, 0:HD]) * ATT_SCALE]
    ew, invw = _softmax_parts(sw_list, [wmask, causal_new & (npos > qpos4 - WINDOW)])
    o_win = (_dot(ew[0].astype(BF16), vwb) + _dot(ew[1], wnew[:, HD:2 * HD])) * invw
    for h in range(N_HEADS):
        r = slice(h * t_new, (h + 1) * t_new)
        g0 = so[:, SM_DG + 3 * h:SM_DG + 3 * h + 1]
        g1 = so[:, SM_DG + 3 * h + 1:SM_DG + 3 * h + 2]
        g2 = so[:, SM_DG + 3 * h + 2:SM_DG + 3 * h + 3]
        o_ref[:, h * HD:(h + 1) * HD] = g0 * o_cmp[r] + g1 * o_sel[r] + g2 * o_win[r]


def _nsa_sample(layer, page_table, nq, nsakv_new, win_new, so, cache4, win_state4, wk_b, wv_b, kcg):
    b, t, _ = nq.shape
    n_pages = page_table.shape[1]
    past = n_pages * PAGE_SIZE
    nc = past // CMP_BLOCK
    w2 = win_state4.shape[2]
    e_mat = _expand_matrix(past)
    pair = _pair_matrix(nc)

    def tok(width):
        return pl.BlockSpec((None, t, width), lambda bi, pt: (bi, 0, 0))

    def const(shape):
        return pl.BlockSpec(shape, lambda bi, pt: (0, 0))

    in_specs = [tok(MIX_W), tok(MIX_W), tok(256), tok(128),
                pl.BlockSpec((None, None, w2, 128), lambda bi, pt: (layer, bi, 0, 0)),
                const((CMP_BLOCK, 128)), const((CMP_BLOCK, 128)), const((1, 128)),
                const((128, past)), const((nc, 128))]
    in_specs += [pl.BlockSpec((None, None, PAGE_SIZE * 4, 128), lambda bi, pt, p=p: (layer, pt[bi, p], 0, 0))
                 for p in range(n_pages)]
    grid_spec = pltpu.PrefetchScalarGridSpec(
        num_scalar_prefetch=1, grid=(b,), in_specs=in_specs,
        out_specs=pl.BlockSpec((None, t, MIX_W), lambda bi, pt: (bi, 0, 0)),
        scratch_shapes=[pltpu.VMEM((nc, HD), F32), pltpu.VMEM((nc, HD), F32)])
    return pl.pallas_call(
        functools.partial(_nsa_sample_kernel, n_pages=n_pages, t_new=t),
        grid_spec=grid_spec,
        out_shape=jax.ShapeDtypeStruct((b, t, MIX_W), F32),
        compiler_params=_cparams(("arbitrary",), 48),
        name="nsa_sample",
    )(page_table, nq, nsakv_new, win_new, so, win_state4, wk_b, wv_b, kcg, e_mat, pair,
      *([cache4] * n_pages))


OUT_TM = 256


def _out_kernel(x_ref, oa_ref, ob_ref, oc_ref, od_ref, z_ref, g0_ref, g1_ref, g2_ref, g3_ref,
                wb_ref, wo_ref, y_ref):
    acc = None
    for g, (o_r, g_r) in enumerate(((oa_ref, g0_ref), (ob_ref, g1_ref), (oc_ref, g2_ref), (od_ref, g3_ref))):
        z = z_ref[:, g * MIX_W:(g + 1) * MIX_W]
        br = (o_r[...] * (z * _sigmoid(z))).astype(BF16)
        term = _sigmoid(g_r[...]) * _dot(br, wb_ref[g])
        acc = term if acc is None else acc + term
    y_ref[...] = x_ref[...] + _dot(acc.astype(BF16), wo_ref[...])


def _out_proj(x2d, oa, ob, oc, od, h2d, wb, wo):
    m = x2d.shape[0]
    tm = OUT_TM

    def rows(width):
        return pl.BlockSpec((tm, width), lambda i: (i, 0))

    def gate(g):
        return pl.BlockSpec((tm, D_MODEL), lambda i, g=g: (i, COL_GATE // D_MODEL + g))

    single = pl.Buffered(1)
    return pl.pallas_call(
        _out_kernel,
        grid=(m // tm,),
        in_specs=[rows(D_MODEL), rows(MIX_W), rows(MIX_W), rows(MIX_W), rows(MIX_W),
                  rows(D_MODEL), gate(0), gate(1), gate(2), gate(3),
                  pl.BlockSpec((N_HEADS, MIX_W, D_MODEL), lambda i: (0, 0, 0), pipeline_mode=single),
                  pl.BlockSpec((D_MODEL, D_MODEL), lambda i: (0, 0), pipeline_mode=single)],
        out_specs=rows(D_MODEL),
        out_shape=jax.ShapeDtypeStruct((m, D_MODEL), F32),
        compiler_params=_cparams(("parallel",), 56),
        name="out_proj",
    )(x2d, oa, ob, oc, od, h2d, h2d, h2d, h2d, h2d, wb, wo)


def _permute_w_in(w):
    offs = {}
    off = 0
    for name, width in _REF_LAYOUT:
        offs[name] = (off, width)
        off += width
    parts = [w[:, offs[n][0]:offs[n][0] + offs[n][1]] for n in _MY_LAYOUT]
    used = sum(offs[n][1] for n in _MY_LAYOUT)
    parts.append(jnp.zeros((w.shape[0], N_PROJ - used), w.dtype))
    return jnp.concatenate(parts, axis=1).astype(BF16)


def _layer_params(l, p):
    zeros = jnp.zeros((2, HD), F32)
    gains = jnp.concatenate([p["fox_qg"][l][None], p["fox_kg"][l][None], p["nsa_qg"][l][None],
                             p["nsa_ksg"][l][None], p["nsa_kwg"][l][None], p["nsa_kcg"][l][None], zeros], axis=0)
    bias = jnp.concatenate([p["fox_bf"][l], p["mlstm_bi"][l], p["mlstm_bf"][l], jnp.zeros((GLA_RANK,), F32),
                            p["nsa_bg"][l].reshape(-1), jnp.zeros((128 - SM_DG - 12,), F32)]).reshape(1, 128)
    w2pad = jnp.zeros((128, 256), F32).at[SM_CA:SM_CA + GLA_RANK].set(p["gla_w2"][l])
    return dict(
        norm_g=p["norm_g"][l], w_in=_permute_w_in(p["w_in"][l]), gains=gains, bias=bias,
        wk_b=jnp.broadcast_to(p["nsa_wk"][l][:, None], (CMP_BLOCK, 128)),
        wv_b=jnp.broadcast_to(p["nsa_wv"][l][:, None], (CMP_BLOCK, 128)),
        kcg=p["nsa_kcg"][l].reshape(1, HD),
        mlstm_hg=p["mlstm_hg"][l], w2pad=w2pad, gla_b2=p["gla_b2"][l], gla_hg=p["gla_hg"][l],
        w_branch=p["w_branch"][l].astype(BF16), w_out=p["w_out"][l].astype(BF16))


def _layer(x3d, lp, st, *, is_prompt, layer, page_table=None, caches=None):
    b, t, _ = x3d.shape
    m = b * t
    x2d = x3d.reshape(m, D_MODEL)
    xn = _rmsnorm(x2d, lp["norm_g"])
    h2d = _matmul(xn, lp["w_in"])
    h3d = h2d.reshape(b, t, N_PROJ)
    (foxkv, nsakv, win, so, fq, fk, fv, nq, nkv, kcmp, vcmp) = _prep(
        h2d, lp["gains"], lp["bias"], lp["wk_b"], lp["wv_b"])
    so3 = so.reshape(b, t, 128)
    fq3 = fq.reshape(b, t, MIX_W)
    nq3 = nq.reshape(b, t, MIX_W)
    chunk = min(64, t)
    if is_prompt:
        lf_t = so3[:, :, SM_AF:SM_AF + N_HEADS].transpose(0, 2, 1).reshape(b * N_HEADS, t)
        f_row = _cumsum_rows(lf_t).reshape(b, N_HEADS, t)
        f_col = f_row.transpose(0, 2, 1)
        o_a = _fox_prompt(fq3, fk.reshape(b, t, MIX_W), fv.reshape(b, t, MIX_W), f_col, f_row)
        win_pad = jnp.pad(nkv.reshape(b, t, 768)[:, :, 4 * HD:], ((0, 0), (WINDOW, 0), (0, 0)))
        o_d = _nsa_prompt(nq3, kcmp.reshape(b, t // CMP_BLOCK, HD), vcmp.reshape(b, t // CMP_BLOCK, HD),
                          nkv.reshape(b, t, 768), win_pad, so3)
        bb = b
    else:
        o_a = _fox_sample(layer, page_table, fq3, foxkv.reshape(b, t, 1024), so3,
                          caches["fox_kv4"], caches["fox_lf4"])
        o_d = _nsa_sample(layer, page_table, nq3, nsakv.reshape(b, t, MIX_W), win.reshape(b, t, 256), so3,
                          caches["nsa_kv4"], caches["nsa_win4"], lp["wk_b"], lp["wv_b"], lp["kcg"])
        bb = 8
    o_b, c_new, n_new, m_new = _mlstm(h3d, so3, lp["mlstm_hg"], st["C"], st["n"], st["m"], bb, chunk)
    o_c, s_new = _gla(h3d, so3, lp["w2pad"], lp["gla_b2"], lp["gla_hg"], st["S"], bb, chunk)
    y2d = _out_proj(x2d, o_a.reshape(m, MIX_W), o_b.reshape(m, MIX_W), o_c.reshape(m, MIX_W),
                    o_d.reshape(m, MIX_W), h2d, lp["w_branch"], lp["w_out"])
    outs = dict(
        fox_kv=foxkv.reshape(b, t, 2, N_HEADS, HD),
        fox_lf=so3[:, :, SM_AF:SM_AF + N_HEADS],
        nsa_kv=nsakv.reshape(b, t, 4, HD),
        win=win.reshape(b, t, 2, HD),
        C=c_new, n=n_new, m=m_new[:, :, 0], S=s_new)
    return y2d.reshape(b, t, D_MODEL), outs


def kernel(x_prompt, x_sample, cache_fox_kv, cache_fox_logf, cache_nsa_kv, page_table, state_nsa_win, state_mlstm_C, state_mlstm_n, state_mlstm_m, state_gla_S, norm_g, w_in, fox_qg, fox_kg, fox_bf, mlstm_bi, mlstm_bf, mlstm_hg, gla_w2, gla_b2, gla_hg, nsa_qg, nsa_kcg, nsa_ksg, nsa_kwg, nsa_wk, nsa_wv, nsa_bg, w_branch, w_out):
    params = dict(norm_g=norm_g, w_in=w_in, fox_qg=fox_qg, fox_kg=fox_kg, fox_bf=fox_bf, mlstm_bi=mlstm_bi,
                  mlstm_bf=mlstm_bf, mlstm_hg=mlstm_hg, gla_w2=gla_w2, gla_b2=gla_b2, gla_hg=gla_hg,
                  nsa_qg=nsa_qg, nsa_kcg=nsa_kcg, nsa_ksg=nsa_ksg, nsa_kwg=nsa_kwg, nsa_wk=nsa_wk,
                  nsa_wv=nsa_wv, nsa_bg=nsa_bg, w_branch=w_branch, w_out=w_out)
    depth = w_in.shape[0]
    bp = x_prompt.shape[0]
    db = x_sample.shape[0]
    n_pool = cache_fox_kv.shape[1]
    w_buf = state_nsa_win.shape[2]
    caches = dict(
        fox_kv4=cache_fox_kv.reshape(depth, n_pool, PAGE_SIZE * 2 * N_HEADS, HD),
        fox_lf4=cache_fox_logf.transpose(0, 1, 3, 2),
        nsa_kv4=cache_nsa_kv.reshape(depth, n_pool, PAGE_SIZE * 4, HD),
        nsa_win4=state_nsa_win.reshape(depth, db, w_buf * 2, HD))
    y_p, y_s = x_prompt, x_sample
    outs_p, outs_s = [], []
    for l in range(depth):
        lp = _layer_params(l, params)
        st_p = dict(C=jnp.zeros((bp, N_HEADS, HD, HD), F32), n=jnp.zeros((bp, N_HEADS, HD), F32),
                    m=jnp.zeros((bp, N_HEADS, HD), F32), S=jnp.zeros((bp, N_HEADS, GLA_DK, GLA_DV), F32))
        st_s = dict(C=state_mlstm_C[l], n=state_mlstm_n[l],
                    m=jnp.broadcast_to(state_mlstm_m[l][:, :, None], (db, N_HEADS, HD)), S=state_gla_S[l])
        y_p, o_p = _layer(y_p, lp, st_p, is_prompt=True, layer=l)
        y_s, o_s = _layer(y_s, lp, st_s, is_prompt=False, layer=l, page_table=page_table, caches=caches)
        o_p["win"] = o_p["win"][:, -min(WINDOW, o_p["win"].shape[1]):]
        o_s["win"] = jnp.concatenate([state_nsa_win[l], o_s["win"]], axis=1)[:, -WINDOW:]
        outs_p.append(o_p)
        outs_s.append(o_s)
    names = ("fox_kv", "fox_lf", "nsa_kv", "win", "C", "n", "m", "S")

    def stack(outs, name):
        return jnp.stack([o[name] for o in outs], axis=0)

    return (y_p, y_s) + tuple(stack(outs_p, n) for n in names) + tuple(stack(outs_s, n) for n in names)
```

```python
import functools

import jax
import jax.numpy as jnp
from jax import lax
from jax.experimental import pallas as pl
from jax.experimental.pallas import tpu as pltpu

F32 = jnp.float32
BF16 = jnp.bfloat16
HIGHEST = lax.Precision.HIGHEST

D_MODEL = 2048
MIX_W = 512
HD = 128
N_HEADS = 4
GLA_DK = 64
GLA_DK_SHIFT = 6
GLA_DV = 128
GLA_RANK = 16
GLA_TAU = 16.0
CMP_BLOCK = 32
SEL_BLOCK = 64
SEL_SHIFT = 6
N_SEL = 16
WINDOW = 512
PAGE_SIZE = 128
EPS = 1e-6
ATT_SCALE = HD ** -0.5
NEG_INF = float("-inf")
KEY_CLASS = 512
BF16_ROWS = 16

_REF_LAYOUT = (
    ("a_q", 512), ("a_k", 512), ("a_v", 512), ("a_f", 4), ("a_z", 512),
    ("b_q", 512), ("b_k", 512), ("b_v", 512), ("b_i", 4), ("b_f", 4), ("b_o", 512), ("b_z", 512),
    ("c_q", 256), ("c_k", 256), ("c_v", 512), ("c_a", 16), ("c_z", 512),
    ("d_q", 512), ("d_kv", 768), ("d_g", 12), ("d_z", 512),
    ("gate", 8192),
)
N_REF = sum(w for _, w in _REF_LAYOUT)
_MY_LAYOUT = (
    "a_z", "b_z", "c_z", "d_z", "gate",
    "a_q", "a_k", "a_v", "d_q", "b_q", "b_k", "b_v", "b_o", "c_v", "c_q", "c_k", "d_kv",
    "a_f", "b_i", "b_f", "c_a", "d_g",
)
N_PROJ = 16384
COL_Z = 0
COL_GATE = 2048
COL_AQ, COL_AK, COL_AV, COL_DQ = 10240, 10752, 11264, 11776
COL_BQ, COL_BK, COL_BV, COL_BO = 12288, 12800, 13312, 13824
COL_CV, COL_CQ, COL_CK, COL_DKV, COL_SMALL = 14336, 14848, 15104, 15360, 16128
SM_AF, SM_BI, SM_BF, SM_CA, SM_DG = 0, 4, 8, 12, 28


def _cparams(sem, vmem_mb):
    return pltpu.CompilerParams(dimension_semantics=sem, vmem_limit_bytes=vmem_mb * 1024 * 1024)


def _log_sigmoid(x):
    return jnp.minimum(x, 0.0) - jnp.log(1.0 + jnp.exp(-jnp.abs(x)))


def _sigmoid(x):
    return 1.0 / (1.0 + jnp.exp(-x))


def _rms_lanes(x, g):
    return (x * lax.rsqrt(jnp.mean(x * x, axis=-1, keepdims=True) + EPS)) * g


def _dot(a, b):
    return jnp.dot(a, b, preferred_element_type=F32)


def _dot_nt(a, b):
    return lax.dot_general(a, b, (((1,), (1,)), ((), ())), preferred_element_type=F32)


def _dot_tn(a, b):
    return lax.dot_general(a, b, (((0,), (0,)), ((), ())), preferred_element_type=F32)


def _dot_exact(a, b):
    return jnp.dot(a, b, precision=HIGHEST, preferred_element_type=F32)


def _eye_mask(n):
    return lax.broadcasted_iota(jnp.int32, (n, n), 0) == lax.broadcasted_iota(jnp.int32, (n, n), 1)


def _col_to_row(c):
    n = c.shape[0]
    return jnp.sum(jnp.where(_eye_mask(n), c, 0.0), axis=0, keepdims=True)


def _row_to_col(r):
    n = r.shape[1]
    return jnp.sum(jnp.where(_eye_mask(n), r, 0.0), axis=1, keepdims=True)


def _tril_ones(n):
    r = lax.broadcasted_iota(jnp.int32, (n, n), 0)
    c = lax.broadcasted_iota(jnp.int32, (n, n), 1)
    return jnp.where(c <= r, 1.0, 0.0).astype(F32)


def _cumsum_lanes(x):
    n = x.shape[-1]
    lane = lax.broadcasted_iota(jnp.int32, x.shape, x.ndim - 1)
    s = 1
    while s < n:
        x = x + jnp.where(lane >= s, pltpu.roll(x, s, axis=x.ndim - 1), 0.0)
        s *= 2
    return x


def _pad_rows(x, rows):
    if x.shape[0] == rows:
        return x
    return jnp.concatenate([x, jnp.zeros((rows - x.shape[0], x.shape[1]), x.dtype)], axis=0)


def _softmax_parts(s_list, mask_list):
    masked = [jnp.where(mk, s, NEG_INF) for s, mk in zip(s_list, mask_list)]
    m = masked[0].max(axis=-1, keepdims=True)
    for s in masked[1:]:
        m = jnp.maximum(m, s.max(axis=-1, keepdims=True))
    m = jnp.where(m > NEG_INF, m, 0.0)
    es = [jnp.exp(s - m) for s in masked]
    tot = es[0].sum(axis=-1, keepdims=True)
    for e in es[1:]:
        tot = tot + e.sum(axis=-1, keepdims=True)
    return es, 1.0 / jnp.maximum(tot, 1e-30)


def _with_carried_outputs(kernel_fn, n_in, n_carried):
    if not n_carried:
        return kernel_fn

    def wrapped(*refs):
        return kernel_fn(*refs[:n_in], *refs[n_in + n_carried:])

    return wrapped


def _carry_args(prev, first_in_index, out_indices):
    if prev is None:
        return [], [], {}
    specs = [pl.BlockSpec(memory_space=pl.ANY) for _ in prev]
    aliases = {first_in_index + k: oi for k, oi in enumerate(out_indices)}
    return specs, list(prev), aliases


def _segments():
    offs = {}
    off = 0
    for name, width in _REF_LAYOUT:
        offs[name] = (off, width)
        off += width
    segs = []
    dst = 0
    for name in _MY_LAYOUT:
        src, width = offs[name]
        segs.append((src, dst, width))
        dst += width
    return segs, dst


def _permute_kernel(w_ref, o_ref):
    segs, used = _segments()
    for src, dst, width in segs:
        o_ref[:, dst:dst + width] = w_ref[:, src:src + width].astype(BF16)
    o_ref[:, used:N_PROJ] = jnp.zeros((o_ref.shape[0], N_PROJ - used), BF16)


def _permute_w_in(w_in):
    depth, d, n = w_in.shape
    tr = 128
    return pl.pallas_call(
        _permute_kernel,
        grid=(depth, d // tr),
        in_specs=[pl.BlockSpec((None, tr, n), lambda l, i: (l, i, 0))],
        out_specs=pl.BlockSpec((None, tr, N_PROJ), lambda l, i: (l, i, 0)),
        out_shape=jax.ShapeDtypeStruct((depth, d, N_PROJ), BF16),
        compiler_params=_cparams(("parallel", "parallel"), 40),
        name="permute_w_in",
    )(w_in)


def _norm_kernel(x_ref, g_ref, o_ref):
    o_ref[...] = _rms_lanes(x_ref[...], g_ref[...]).astype(o_ref.dtype)


def _rmsnorm(x2d, g):
    m, d = x2d.shape
    tm = min(512, m)
    return pl.pallas_call(
        _norm_kernel,
        grid=(m // tm,),
        in_specs=[pl.BlockSpec((tm, d), lambda i: (i, 0)), pl.BlockSpec((1, d), lambda i: (0, 0))],
        out_specs=pl.BlockSpec((tm, d), lambda i: (i, 0)),
        out_shape=jax.ShapeDtypeStruct((m, d), BF16),
        compiler_params=_cparams(("parallel",), 32),
        name="rmsnorm",
    )(x2d, g.reshape(1, d))


def _mm_kernel(a_ref, b_ref, o_ref):
    o_ref[...] = _dot(a_ref[...], b_ref[...])


def _in_proj(a, w_all, layer):
    m, k = a.shape
    n = w_all.shape[2]
    tm, tn = min(1024, m), 1024
    return pl.pallas_call(
        _mm_kernel,
        grid=(m // tm, n // tn),
        in_specs=[pl.BlockSpec((tm, k), lambda i, j: (i, 0)),
                  pl.BlockSpec((None, k, tn), lambda i, j: (layer, 0, j))],
        out_specs=pl.BlockSpec((tm, tn), lambda i, j: (i, j)),
        out_shape=jax.ShapeDtypeStruct((m, n), F32),
        compiler_params=_cparams(("parallel", "arbitrary"), 48),
        name="in_proj",
    )(a, w_all)


PREP_TM = 256
N_PREP_IN = 10


def _prep_kernel(aq_ref, ak_ref, av_ref, dq_ref, dkv_ref, sm_ref, gains_ref, bias_ref, wk_ref, wv_ref,
                 foxkv_ref, nsakv_ref, win_ref, so_ref, fq_ref, fk_ref, fv_ref, nq_ref, nkv_ref,
                 kcmp_ref, vcmp_ref):
    tm = PREP_TM
    g_fq, g_fk, g_nq = gains_ref[0:1, :], gains_ref[1:2, :], gains_ref[2:3, :]
    g_ks, g_kw, g_kc = gains_ref[3:4, :], gains_ref[4:5, :], gains_ref[5:6, :]
    for h in range(N_HEADS):
        sl = slice(h * HD, (h + 1) * HD)
        fq_ref[:, sl] = _rms_lanes(aq_ref[:, sl], g_fq).astype(BF16)
        kn = _rms_lanes(ak_ref[:, sl], g_fk)
        foxkv_ref[pl.ds(h, tm, stride=2 * N_HEADS), :] = kn
        fk_ref[:, sl] = kn.astype(BF16)
        v = av_ref[:, sl]
        foxkv_ref[pl.ds(N_HEADS + h, tm, stride=2 * N_HEADS), :] = v
        fv_ref[:, sl] = v.astype(BF16)
        nq_ref[:, sl] = _rms_lanes(dq_ref[:, sl], g_nq).astype(BF16)
    kc = dkv_ref[:, 0:128]
    vc = dkv_ref[:, 128:256]
    ks = _rms_lanes(dkv_ref[:, 256:384], g_ks)
    vs = dkv_ref[:, 384:512]
    kw = _rms_lanes(dkv_ref[:, 512:640], g_kw)
    vw = dkv_ref[:, 640:768]
    for j, a in enumerate((kc, vc, ks, vs)):
        nsakv_ref[pl.ds(j, tm, stride=4), :] = a
    win_ref[pl.ds(0, tm, stride=2), :] = kw
    win_ref[pl.ds(1, tm, stride=2), :] = vw
    for j, a in enumerate((kc, vc, ks, vs, kw, vw)):
        nkv_ref[:, j * HD:(j + 1) * HD] = a.astype(BF16)
    nb = tm // CMP_BLOCK
    kcs = jnp.sum(kc.reshape(nb, CMP_BLOCK, HD) * wk_ref[...][None], axis=1)
    kcmp_ref[...] = _rms_lanes(kcs, g_kc)
    vcmp_ref[...] = jnp.sum(vc.reshape(nb, CMP_BLOCK, HD) * wv_ref[...][None], axis=1)
    y = sm_ref[...] + bias_ref[...]
    lane = lax.broadcasted_iota(jnp.int32, y.shape, 1)
    is_ls = (lane < SM_BI) | ((lane >= SM_BF) & (lane < SM_CA))
    is_sg = (lane >= SM_DG) & (lane < SM_DG + 12)
    so_ref[...] = jnp.where(is_ls, _log_sigmoid(y), jnp.where(is_sg, _sigmoid(y), y))


def _prep(h2d, gains, bias, wk_b, wv_b, layer, depth, prev):
    m = h2d.shape[0]
    tm = PREP_TM

    def col(width, off):
        blk = off // width
        return pl.BlockSpec((tm, width), lambda i, blk=blk: (i, blk))

    def full(shape):
        return pl.BlockSpec(shape, lambda i: (0, 0))

    def rows(width):
        return pl.BlockSpec((tm, width), lambda i: (i, 0))

    def stacked(slots):
        return pl.BlockSpec((None, tm * slots, HD), lambda i: (layer, i, 0))

    out_shapes = (
        jax.ShapeDtypeStruct((depth, m * 8, HD), F32),
        jax.ShapeDtypeStruct((depth, m * 4, HD), F32),
        jax.ShapeDtypeStruct((m * 2, HD), F32),
        jax.ShapeDtypeStruct((m, 128), F32),
        jax.ShapeDtypeStruct((m, 512), BF16),
        jax.ShapeDtypeStruct((m, 512), BF16),
        jax.ShapeDtypeStruct((m, 512), BF16),
        jax.ShapeDtypeStruct((m, 512), BF16),
        jax.ShapeDtypeStruct((m, 768), BF16),
        jax.ShapeDtypeStruct((m // CMP_BLOCK, 128), F32),
        jax.ShapeDtypeStruct((m // CMP_BLOCK, 128), F32),
    )
    out_specs = (stacked(8), stacked(4), pl.BlockSpec((tm * 2, HD), lambda i: (i, 0)),
                 rows(128), rows(512), rows(512), rows(512), rows(512), rows(768),
                 pl.BlockSpec((tm // CMP_BLOCK, 128), lambda i: (i, 0)),
                 pl.BlockSpec((tm // CMP_BLOCK, 128), lambda i: (i, 0)))
    c_specs, c_args, aliases = _carry_args(prev, N_PREP_IN, (0, 1))
    return pl.pallas_call(
        _with_carried_outputs(_prep_kernel, N_PREP_IN, len(c_args)),
        grid=(m // tm,),
        in_specs=[col(512, COL_AQ), col(512, COL_AK), col(512, COL_AV), col(512, COL_DQ),
                  col(768, COL_DKV), col(128, COL_SMALL),
                  full((8, 128)), full((1, 128)), full((CMP_BLOCK, 128)), full((CMP_BLOCK, 128))] + c_specs,
        out_specs=out_specs,
        out_shape=out_shapes,
        input_output_aliases=aliases,
        compiler_params=_cparams(("parallel",), 32),
        name="prep",
    )(h2d, h2d, h2d, h2d, h2d, h2d, gains, bias, wk_b, wv_b, *c_args)


def _cumsum_rows_kernel(x_ref, o_ref):
    o_ref[...] = _cumsum_lanes(x_ref[...])


def _cumsum_rows(x):
    r, n = x.shape
    return pl.pallas_call(
        _cumsum_rows_kernel,
        grid=(1,),
        in_specs=[pl.BlockSpec((r, n), lambda i: (0, 0))],
        out_specs=pl.BlockSpec((r, n), lambda i: (0, 0)),
        out_shape=jax.ShapeDtypeStruct((r, n), F32),
        name="fox_cumsum",
    )(x)


FOX_TQ = 256


def _fox_prompt_kernel(q_ref, k_ref, v_ref, fc_ref, fr_ref, o_ref, *, t_len):
    tq = FOX_TQ
    i = pl.program_id(1)
    n_cls = -(-t_len // KEY_CLASS)
    cls = (i * tq + tq - 1) // KEY_CLASS

    def run(kl):
        qpos = i * tq + lax.broadcasted_iota(jnp.int32, (tq, kl), 0)
        kpos = lax.broadcasted_iota(jnp.int32, (tq, kl), 1)
        mask = kpos <= qpos
        for h in range(N_HEADS):
            sl = slice(h * HD, (h + 1) * HD)
            s = _dot_nt(q_ref[:, sl], k_ref[0:kl, sl]) * ATT_SCALE
            s = s + fc_ref[:, h:h + 1] - fr_ref[h:h + 1, 0:kl]
            (e,), inv = _softmax_parts([s], [mask])
            o_ref[:, sl] = _dot(e.astype(BF16), v_ref[0:kl, sl]) * inv

    for c in range(n_cls):
        pl.when(cls == c)(functools.partial(run, min((c + 1) * KEY_CLASS, t_len)))


def _fox_prompt(fq, fk, fv, f_col, f_row):
    b, t, _ = fq.shape
    tq = FOX_TQ
    return pl.pallas_call(
        functools.partial(_fox_prompt_kernel, t_len=t),
        grid=(b, t // tq),
        in_specs=[pl.BlockSpec((None, tq, MIX_W), lambda bi, i: (bi, i, 0)),
                  pl.BlockSpec((None, t, MIX_W), lambda bi, i: (bi, 0, 0)),
                  pl.BlockSpec((None, t, MIX_W), lambda bi, i: (bi, 0, 0)),
                  pl.BlockSpec((None, tq, N_HEADS), lambda bi, i: (bi, i, 0)),
                  pl.BlockSpec((None, N_HEADS, t), lambda bi, i: (bi, 0, 0))],
        out_specs=pl.BlockSpec((None, tq, MIX_W), lambda bi, i: (bi, i, 0)),
        out_shape=jax.ShapeDtypeStruct((b, t, MIX_W), F32),
        compiler_params=_cparams(("parallel", "arbitrary"), 48),
        name="fox_prompt",
    )(fq, fk, fv, f_col, f_row)


PAGE_GROUP = 2


def _fox_sample_kernel(pt_ref, q_ref, kn_ref, vn_ref, so_ref, *rest, n_pages, t_new):
    del pt_ref
    kv_pages = rest[:n_pages]
    lf_pages = rest[n_pages:2 * n_pages]
    o_ref = rest[2 * n_pages]
    past = n_pages * PAGE_SIZE
    n_grp = n_pages // PAGE_GROUP
    gw = PAGE_GROUP * PAGE_SIZE
    local = [_cumsum_lanes(lf_pages[p][...]) for p in range(n_pages)]
    carry = jnp.zeros((N_HEADS, 1), F32)
    f_parts = []
    for p in range(n_pages):
        f_parts.append(local[p] + carry)
        carry = carry + local[p][:, PAGE_SIZE - 1:PAGE_SIZE]
    f_grp = [jnp.concatenate(f_parts[g * PAGE_GROUP:(g + 1) * PAGE_GROUP], axis=1) for g in range(n_grp)]
    f_new_all = _dot_exact(_tril_ones(t_new), so_ref[...])
    tpos = lax.broadcasted_iota(jnp.int32, (t_new, gw), 0) + past
    kpos0 = lax.broadcasted_iota(jnp.int32, (t_new, gw), 1)
    rr = lax.broadcasted_iota(jnp.int32, (t_new, t_new), 0)
    cc = lax.broadcasted_iota(jnp.int32, (t_new, t_new), 1)

    def page_rows(g, slot):
        parts = [kv_pages[g * PAGE_GROUP + u][pl.ds(slot, PAGE_SIZE, stride=2 * N_HEADS), :].astype(BF16)
                 for u in range(PAGE_GROUP)]
        return jnp.concatenate(parts, axis=0)

    for h in range(N_HEADS):
        sl = slice(h * HD, (h + 1) * HD)
        q = q_ref[:, sl]
        fq = f_new_all[:, SM_AF + h:SM_AF + h + 1] + carry[h:h + 1, :]
        fk_new = _col_to_row(fq)
        s_list, m_list = [], []
        for g in range(n_grp):
            s = _dot_nt(q, page_rows(g, h)) * ATT_SCALE + fq - f_grp[g][h:h + 1, :]
            s_list.append(s)
            m_list.append(kpos0 + g * gw <= tpos)
        s_new = _dot_nt(q.astype(F32), kn_ref[:, sl].astype(F32)) * ATT_SCALE + fq - fk_new
        s_list.append(s_new)
        m_list.append(cc <= rr)
        es, inv = _softmax_parts(s_list, m_list)
        acc = _dot(es[n_grp], vn_ref[:, sl].astype(F32))
        for g in range(n_grp):
            acc = acc + _dot(es[g].astype(BF16), page_rows(g, N_HEADS + h))
        o_ref[:, sl] = acc * inv


def _fox_sample(layer, page_table, fq, fk, fv, so, cache_kv4, cache_lf4):
    b, t, _ = fq.shape
    n_pages = page_table.shape[1]

    def page_spec(rows, p):
        return pl.BlockSpec((None, None, rows, 128), lambda bi, pt, p=p: (layer, pt[bi, p], 0, 0))

    tok = pl.BlockSpec((None, t, MIX_W), lambda bi, pt: (bi, 0, 0))
    in_specs = [tok, tok, tok, pl.BlockSpec((None, t, 128), lambda bi, pt: (bi, 0, 0))]
    in_specs += [page_spec(PAGE_SIZE * 2 * N_HEADS, p) for p in range(n_pages)]
    in_specs += [page_spec(N_HEADS, p) for p in range(n_pages)]
    grid_spec = pltpu.PrefetchScalarGridSpec(
        num_scalar_prefetch=1, grid=(b,), in_specs=in_specs,
        out_specs=pl.BlockSpec((None, t, MIX_W), lambda bi, pt: (bi, 0, 0)))
    return pl.pallas_call(
        functools.partial(_fox_sample_kernel, n_pages=n_pages, t_new=t),
        grid_spec=grid_spec,
        out_shape=jax.ShapeDtypeStruct((b, t, MIX_W), F32),
        compiler_params=_cparams(("arbitrary",), 48),
        name="fox_sample",
    )(page_table, fq, fk, fv, so, *([cache_kv4] * n_pages), *([cache_lf4] * n_pages))


N_MLSTM_IN = 9
BATCH_UNROLL = 4


def _mlstm_kernel(q_ref, k_ref, v_ref, og_ref, so_ref, hg_ref, c0_ref, n0_ref, m0_ref,
                  o_ref, c_ref, n_ref, m_ref, *, bb, chunk):
    L = chunk
    Lp = max(L, BF16_ROWS)

    @pl.when(pl.program_id(1) == 0)
    def _():
        c_ref[...] = c0_ref[...]
        n_ref[...] = n0_ref[...]
        m_ref[...] = m0_ref[...]

    tril = _tril_ones(Lp)
    causal = tril > 0.5
    hg = hg_ref[...]
    lane = lax.broadcasted_iota(jnp.int32, (Lp - L, 128), 1) if Lp > L else None

    def one(bi, states):
        so = so_ref[bi]
        if Lp > L:
            pad = jnp.where((lane >= SM_BI) & (lane < SM_BF), -1e30, 0.0)
            so = jnp.concatenate([so, pad], axis=0)
        csum = _dot_exact(tril, so)
        heads = range(N_HEADS)
        sls = [slice(h * HD, (h + 1) * HD) for h in heads]
        q = [_pad_rows(q_ref[bi, :, sl], Lp) for sl in sls]
        k = [_pad_rows(k_ref[bi, :, sl], Lp) * ATT_SCALE for sl in sls]
        vb = [_pad_rows(v_ref[bi, :, sl], Lp).astype(BF16) for sl in sls]
        qb = [x.astype(BF16) for x in q]
        ig = [so[:, SM_BI + h:SM_BI + h + 1] for h in heads]
        bcol = [csum[:, SM_BF + h:SM_BF + h + 1] for h in heads]
        c_st = [states[h][0] for h in heads]
        n_st = [states[h][1] for h in heads]
        m_st = [states[h][2] for h in heads]
        qk_raw = [_dot_nt(qb[h], k[h].astype(BF16)) for h in heads]
        q_c = [_dot(qb[h], c_st[h].astype(BF16)) for h in heads]
        dmat = [jnp.where(causal, bcol[h] - _col_to_row(bcol[h]) + _col_to_row(ig[h]), NEG_INF) for h in heads]
        inter = [bcol[h] + m_st[h] for h in heads]
        m_t = [jnp.maximum(inter[h], dmat[h].max(axis=-1, keepdims=True)) for h in heads]
        w_inter = [jnp.exp(inter[h] - m_t[h]) for h in heads]
        qk = [qk_raw[h] * jnp.exp(dmat[h] - m_t[h]) for h in heads]
        qk_v = [_dot(qk[h].astype(BF16), vb[h]) for h in heads]
        b_last = [bcol[h][Lp - 1:Lp, :] for h in heads]
        g_end = [b_last[h] - bcol[h] + ig[h] for h in heads]
        m_new = [jnp.maximum(b_last[h] + m_st[h], g_end[h].max(axis=0, keepdims=True)) for h in heads]
        a_prev = [jnp.exp(b_last[h] + m_st[h] - m_new[h]) for h in heads]
        kw = [k[h] * jnp.exp(g_end[h] - m_new[h]) for h in heads]
        k_v = [_dot_tn(kw[h].astype(BF16), vb[h]) for h in heads]
        new_states = []
        for h in heads:
            num = w_inter[h] * q_c[h] + qk_v[h]
            den = (w_inter[h] * jnp.sum(q[h] * n_st[h], axis=-1, keepdims=True)
                   + jnp.sum(qk[h], axis=-1, keepdims=True))
            hout = num / jnp.maximum(jnp.abs(den), jnp.exp(-m_t[h]))
            new_states.append((a_prev[h] * c_st[h] + k_v[h],
                               a_prev[h] * n_st[h] + jnp.sum(kw[h], axis=0, keepdims=True),
                               jnp.broadcast_to(m_new[h], (1, HD))))
            hn = _rms_lanes(hout[0:L], hg)
            o_ref[bi, :, sls[h]] = hn * _sigmoid(og_ref[bi, :, sls[h]])
        return new_states

    unroll = min(BATCH_UNROLL, bb)

    def group(g, carry):
        idx = [g * unroll + u for u in range(unroll)]
        old = [[(c_ref[bi, h], n_ref[bi, h:h + 1, :], m_ref[bi, h:h + 1, 0:1]) for h in range(N_HEADS)]
               for bi in idx]
        new = [one(bi, st) for bi, st in zip(idx, old)]
        for bi, st in zip(idx, new):
            for h, (c_new, n_new, m_new) in enumerate(st):
                c_ref[bi, h] = c_new
                n_ref[bi, h:h + 1, :] = n_new
                m_ref[bi, h:h + 1, :] = m_new
        return carry

    if bb == unroll:
        group(0, 0)
    else:
        lax.fori_loop(0, bb // unroll, group, 0)


def _mlstm(h3d, so3d, hg, c0, n0, m0, state_layer, layer, depth, prev, bb, chunk):
    b, t, _ = h3d.shape
    nc = t // chunk

    def col(off):
        blk = off // MIX_W
        return pl.BlockSpec((bb, chunk, MIX_W), lambda bi, c, blk=blk: (bi, c, blk))

    def st_c(li):
        return pl.BlockSpec((None, bb, N_HEADS, HD, HD), lambda bi, c: (li, bi, 0, 0, 0))

    def st_n(li):
        return pl.BlockSpec((None, bb, N_HEADS, HD), lambda bi, c: (li, bi, 0, 0))

    c_specs, c_args, aliases = _carry_args(prev, N_MLSTM_IN, (1, 2, 3))
    return pl.pallas_call(
        _with_carried_outputs(functools.partial(_mlstm_kernel, bb=bb, chunk=chunk), N_MLSTM_IN, len(c_args)),
        grid=(b // bb, nc),
        in_specs=[col(COL_BQ), col(COL_BK), col(COL_BV), col(COL_BO),
                  pl.BlockSpec((bb, chunk, 128), lambda bi, c: (bi, c, 0)),
                  pl.BlockSpec((1, HD), lambda bi, c: (0, 0)),
                  st_c(state_layer), st_n(state_layer), st_n(state_layer)] + c_specs,
        out_specs=(pl.BlockSpec((bb, chunk, MIX_W), lambda bi, c: (bi, c, 0)),
                   st_c(layer), st_n(layer), st_n(layer)),
        out_shape=(jax.ShapeDtypeStruct((b, t, MIX_W), F32),
                   jax.ShapeDtypeStruct((depth, b, N_HEADS, HD, HD), F32),
                   jax.ShapeDtypeStruct((depth, b, N_HEADS, HD), F32),
                   jax.ShapeDtypeStruct((depth, b, N_HEADS, HD), F32)),
        input_output_aliases=aliases,
        compiler_params=_cparams(("parallel", "arbitrary"), 40),
        name="mlstm",
    )(h3d, h3d, h3d, h3d, so3d, hg.reshape(1, HD), c0, n0, m0, *c_args)


N_GLA_IN = 8
GLA_SUB = 16


def _gla_kernel(q_ref, k_ref, v_ref, so_ref, w2_ref, b2_ref, hg_ref, s0_ref, o_ref, s_ref, *, bb, chunk):
    L = chunk
    Lp = max(L, BF16_ROWS)
    sub = GLA_SUB

    @pl.when(pl.program_id(1) == 0)
    def _():
        s_ref[...] = s0_ref[...]

    tril = _tril_ones(Lp)
    hg = hg_ref[...]
    w2 = w2_ref[...].astype(BF16)
    b2 = b2_ref[...]
    q_scale = GLA_DK ** -0.5
    real_row = lax.broadcasted_iota(jnp.int32, (Lp, 1), 0) < L

    heads = range(N_HEADS)
    dkw = N_HEADS * GLA_DK
    head_of_lane = lax.broadcasted_iota(jnp.int32, (1, dkw), 1) >> GLA_DK_SHIFT
    zero_blk = jnp.zeros((GLA_DK, GLA_DV), F32)

    def one(bi, s_old):
        pre = _dot(_pad_rows(so_ref[bi], Lp).astype(BF16), w2) + b2
        la = jnp.where(real_row, _log_sigmoid(pre) / GLA_TAU, 0.0)
        bcs = _dot_exact(tril, la)
        q = _pad_rows(q_ref[bi], Lp) * q_scale
        k = _pad_rows(k_ref[bi], Lp)
        vb = _pad_rows(v_ref[bi], Lp).astype(BF16)
        s_bd = jnp.concatenate(
            [jnp.concatenate([s_old[h] if g == h else zero_blk for g in heads], axis=1) for h in heads], axis=0)
        inter = _dot((q * jnp.exp(bcs)).astype(BF16), s_bd.astype(BF16))
        for i in range(Lp // sub):
            r0 = i * sub
            hi = r0 + sub
            base = bcs[r0 - 1:r0, :] if i > 0 else jnp.zeros((1, dkw), F32)
            qi = q[r0:hi, :] * jnp.exp(bcs[r0:hi, :] - base)
            ke = (k[0:hi, :] * jnp.exp(base - bcs[0:hi, :])).astype(BF16)
            q_heads = jnp.concatenate([jnp.where(head_of_lane == h, qi, 0.0) for h in heads], axis=0)
            a = _dot_nt(q_heads.astype(BF16), ke)
            rr = (lax.broadcasted_iota(jnp.int32, (N_HEADS * sub, hi), 0) & (sub - 1)) + r0
            cc = lax.broadcasted_iota(jnp.int32, (N_HEADS * sub, hi), 1)
            a = jnp.where(cc <= rr, a, 0.0).astype(BF16)
            rows = min(hi, L) - r0
            for h in heads:
                vsl = slice(h * GLA_DV, (h + 1) * GLA_DV)
                oi = inter[r0:hi, vsl] + _dot(a[h * sub:(h + 1) * sub, :], vb[0:hi, vsl])
                o_ref[bi, r0:r0 + rows, vsl] = _rms_lanes(oi[0:rows], hg)
        b_end = bcs[Lp - 1:Lp, :]
        kd = (k * jnp.exp(b_end - bcs)).astype(BF16)
        upd = _dot_tn(kd, vb)
        dcol = _row_to_col(jnp.exp(b_end))
        return [dcol[h * GLA_DK:(h + 1) * GLA_DK, :] * s_old[h]
                + upd[h * GLA_DK:(h + 1) * GLA_DK, h * GLA_DV:(h + 1) * GLA_DV] for h in heads]

    unroll = min(BATCH_UNROLL, bb)

    def group(g, carry):
        idx = [g * unroll + u for u in range(unroll)]
        old = [[s_ref[bi, h] for h in heads] for bi in idx]
        new = [one(bi, s_old) for bi, s_old in zip(idx, old)]
        for bi, s_new in zip(idx, new):
            for h in heads:
                s_ref[bi, h] = s_new[h]
        return carry

    if bb == unroll:
        group(0, 0)
    else:
        lax.fori_loop(0, bb // unroll, group, 0)


def _gla(h3d, so3d, w2pad, b2, hg, s0, state_layer, layer, depth, prev, bb, chunk):
    b, t, _ = h3d.shape
    nc = t // chunk

    def st(li):
        return pl.BlockSpec((None, bb, N_HEADS, GLA_DK, GLA_DV), lambda bi, c: (li, bi, 0, 0, 0))

    c_specs, c_args, aliases = _carry_args(prev, N_GLA_IN, (1,))
    return pl.pallas_call(
        _with_carried_outputs(functools.partial(_gla_kernel, bb=bb, chunk=chunk), N_GLA_IN, len(c_args)),
        grid=(b // bb, nc),
        in_specs=[pl.BlockSpec((bb, chunk, 256), lambda bi, c: (bi, c, COL_CQ // 256)),
                  pl.BlockSpec((bb, chunk, 256), lambda bi, c: (bi, c, COL_CK // 256)),
                  pl.BlockSpec((bb, chunk, MIX_W), lambda bi, c: (bi, c, COL_CV // MIX_W)),
                  pl.BlockSpec((bb, chunk, 128), lambda bi, c: (bi, c, 0)),
                  pl.BlockSpec((128, 256), lambda bi, c: (0, 0)),
                  pl.BlockSpec((1, 256), lambda bi, c: (0, 0)),
                  pl.BlockSpec((1, GLA_DV), lambda bi, c: (0, 0)),
                  st(state_layer)] + c_specs,
        out_specs=(pl.BlockSpec((bb, chunk, MIX_W), lambda bi, c: (bi, c, 0)), st(layer)),
        out_shape=(jax.ShapeDtypeStruct((b, t, MIX_W), F32),
                   jax.ShapeDtypeStruct((depth, b, N_HEADS, GLA_DK, GLA_DV), F32)),
        input_output_aliases=aliases,
        compiler_params=_cparams(("parallel", "arbitrary"), 40),
        name="gla",
    )(h3d, h3d, h3d, so3d, w2pad, b2.reshape(1, 256), hg.reshape(1, GLA_DV), s0, *c_args)


def _select_blocks(imp, qpos_col, ns):
    r = imp.shape[0]
    j = lax.broadcasted_iota(jnp.int32, (r, 128), 1)
    cur = qpos_col >> SEL_SHIFT
    valid = j <= cur
    forced = (j == 0) | (valid & (j >= cur - 1))
    val = jnp.where(forced, 1e4, jnp.where(valid, imp, -1e4))
    val = jnp.where(j < ns, val, -3e38)
    rank = jnp.zeros((r, 128), F32)
    for i in range(ns):
        ci = val[:, i:i + 1]
        beats = (ci > val) | ((ci == val) & (j > i))
        rank = rank + jnp.where(beats, 1.0, 0.0)
    return (rank < float(min(N_SEL, ns))) & (j < ns)


def _pair_matrix(nc):
    c = jnp.arange(nc)[:, None]
    j = jnp.arange(128)[None, :]
    return (c // (SEL_BLOCK // CMP_BLOCK) == j).astype(F32)


def _expand_matrix(tk):
    j = jnp.arange(128)[:, None]
    s = jnp.arange(tk)[None, :]
    return (s // SEL_BLOCK == j).astype(BF16)


NSA_TQ = 128


def _nsa_prompt_kernel(q_ref, kcmp_ref, vcmp_ref, kv_ref, win_ref, so_ref, e_ref, pair_ref, o_ref, *, t_len):
    tq = NSA_TQ
    nc = t_len // CMP_BLOCK
    ns = -(-t_len // SEL_BLOCK)
    i = pl.program_id(1)
    start = pl.multiple_of(i * tq, tq)
    qpos_col = start + lax.broadcasted_iota(jnp.int32, (tq, 1), 0)
    so = so_ref[...]
    q_all = jnp.concatenate([q_ref[:, h * HD:(h + 1) * HD] for h in range(N_HEADS)], axis=0)
    sc = _dot_nt(q_all, kcmp_ref[...].astype(BF16)) * ATT_SCALE
    qpos4 = start + (lax.broadcasted_iota(jnp.int32, (N_HEADS * tq, 1), 0) & (tq - 1))
    cend = lax.broadcasted_iota(jnp.int32, (N_HEADS * tq, nc), 1) * CMP_BLOCK + (CMP_BLOCK - 1)
    (ec,), invc = _softmax_parts([sc], [cend <= qpos4])
    pc = ec * invc
    o_cmp = _dot(pc.astype(BF16), vcmp_ref[...].astype(BF16))
    imp_c = pc[0:tq]
    for h in range(1, N_HEADS):
        imp_c = imp_c + pc[h * tq:(h + 1) * tq]
    imp = _dot_exact(imp_c, pair_ref[...])
    sel_b = jnp.where(_select_blocks(imp, qpos_col, ns), 1.0, 0.0).astype(BF16)
    band = WINDOW + tq
    kwin = win_ref[pl.ds(start, band), 0:HD]
    vwin = win_ref[pl.ds(start, band), HD:2 * HD]
    wpos = start - WINDOW + lax.broadcasted_iota(jnp.int32, (tq, band), 1)
    wmask = (wpos <= qpos_col) & (wpos > qpos_col - WINDOW) & (wpos >= 0)
    for h in range(N_HEADS):
        sl = slice(h * HD, (h + 1) * HD)
        (ew,), invw = _softmax_parts([_dot_nt(q_ref[:, sl], kwin) * ATT_SCALE], [wmask])
        g0 = so[:, SM_DG + 3 * h:SM_DG + 3 * h + 1]
        g2 = so[:, SM_DG + 3 * h + 2:SM_DG + 3 * h + 3]
        o_ref[:, sl] = g0 * o_cmp[h * tq:(h + 1) * tq] + g2 * (_dot(ew.astype(BF16), vwin) * invw)

    n_cls = -(-t_len // KEY_CLASS)
    cls = (start + tq - 1) // KEY_CLASS

    def run_selected(kl):
        sel_keys = _dot(sel_b, e_ref[:, 0:kl]) > 0.5
        kpos = lax.broadcasted_iota(jnp.int32, (tq, kl), 1)
        smask = sel_keys & (kpos <= qpos_col)
        ks = kv_ref[0:kl, 2 * HD:3 * HD]
        vs = kv_ref[0:kl, 3 * HD:4 * HD]
        for h in range(N_HEADS):
            sl = slice(h * HD, (h + 1) * HD)
            (es,), invs = _softmax_parts([_dot_nt(q_ref[:, sl], ks) * ATT_SCALE], [smask])
            g1 = so[:, SM_DG + 3 * h + 1:SM_DG + 3 * h + 2]
            o_ref[:, sl] = o_ref[:, sl] + g1 * (_dot(es.astype(BF16), vs) * invs)

    for c in range(n_cls):
        pl.when(cls == c)(functools.partial(run_selected, min((c + 1) * KEY_CLASS, t_len)))


def _nsa_prompt(nq, kcmp, vcmp, nkv, win_pad, so):
    b, t, _ = nq.shape
    tq = NSA_TQ
    nc = t // CMP_BLOCK
    e_mat = _expand_matrix(t)
    pair = _pair_matrix(nc)
    return pl.pallas_call(
        functools.partial(_nsa_prompt_kernel, t_len=t),
        grid=(b, t // tq),
        in_specs=[pl.BlockSpec((None, tq, MIX_W), lambda bi, i: (bi, i, 0)),
                  pl.BlockSpec((None, nc, HD), lambda bi, i: (bi, 0, 0)),
                  pl.BlockSpec((None, nc, HD), lambda bi, i: (bi, 0, 0)),
                  pl.BlockSpec((None, t, 768), lambda bi, i: (bi, 0, 0)),
                  pl.BlockSpec((None, t + WINDOW, 256), lambda bi, i: (bi, 0, 0)),
                  pl.BlockSpec((None, tq, 128), lambda bi, i: (bi, i, 0)),
                  pl.BlockSpec((128, t), lambda bi, i: (0, 0)),
                  pl.BlockSpec((nc, 128), lambda bi, i: (0, 0))],
        out_specs=pl.BlockSpec((None, tq, MIX_W), lambda bi, i: (bi, i, 0)),
        out_shape=jax.ShapeDtypeStruct((b, t, MIX_W), F32),
        compiler_params=_cparams(("parallel", "arbitrary"), 48),
        name="nsa_prompt",
    )(nq, kcmp, vcmp, nkv, win_pad, so, e_mat, pair)


N_NSA_S_IN = 11


def _nsa_sample_kernel(pt_ref, q_ref, kvn_ref, winn_ref, so_ref, win_ref, wk_ref, wv_ref, kcg_ref,
                       e_ref, pair_ref, *rest, n_pages, t_new):
    del pt_ref
    pages = rest[:n_pages]
    o_ref, wout_ref = rest[n_pages], rest[n_pages + 1]
    cmp_k, cmp_v = rest[n_pages + 2], rest[n_pages + 3]
    past = n_pages * PAGE_SIZE
    nc = past // CMP_BLOCK
    ns = -(-(past + t_new) // SEL_BLOCK)
    per_page = PAGE_SIZE // CMP_BLOCK
    n_grp = n_pages // PAGE_GROUP
    gw = PAGE_GROUP * PAGE_SIZE
    rows = N_HEADS * t_new
    so = so_ref[...]

    def page_rows(p, j):
        return pages[p][pl.ds(j, PAGE_SIZE, stride=4), :]

    def group_rows(g, j):
        return jnp.concatenate([page_rows(g * PAGE_GROUP + u, j).astype(BF16) for u in range(PAGE_GROUP)], axis=0)

    for p in range(n_pages):
        kc = page_rows(p, 0).reshape(per_page, CMP_BLOCK, HD)
        vc = page_rows(p, 1).reshape(per_page, CMP_BLOCK, HD)
        cmp_k[p * per_page:(p + 1) * per_page, :] = jnp.sum(kc * wk_ref[...][None], axis=1)
        cmp_v[p * per_page:(p + 1) * per_page, :] = jnp.sum(vc * wv_ref[...][None], axis=1)
    kcmp = _rms_lanes(cmp_k[...], kcg_ref[...]).astype(BF16)
    vcmp = cmp_v[...].astype(BF16)

    q_all = jnp.concatenate([q_ref[:, h * HD:(h + 1) * HD].astype(F32) for h in range(N_HEADS)], axis=0)
    q_b = q_all.astype(BF16)
    trow = lax.broadcasted_iota(jnp.int32, (rows, 1), 0) & (t_new - 1)
    qpos4 = past + trow
    qpos_col = past + lax.broadcasted_iota(jnp.int32, (t_new, 1), 0)
    sc = _dot_nt(q_b, kcmp) * ATT_SCALE
    cend = lax.broadcasted_iota(jnp.int32, (rows, nc), 1) * CMP_BLOCK + (CMP_BLOCK - 1)
    (ec,), invc = _softmax_parts([sc], [cend <= qpos4])
    pc = ec * invc
    o_cmp = _dot(pc.astype(BF16), vcmp)
    imp_c = pc[0:t_new]
    for h in range(1, N_HEADS):
        imp_c = imp_c + pc[h * t_new:(h + 1) * t_new]
    imp = _dot_exact(imp_c, pair_ref[...])
    sel = _select_blocks(imp, qpos_col, ns)
    self32 = jnp.where(sel, 1.0, 0.0)
    sel_past = _dot(self32.astype(BF16), e_ref[...])
    sel_past4 = jnp.concatenate([sel_past] * N_HEADS, axis=0) > 0.5
    new_blk = past // SEL_BLOCK
    sel_new = jnp.concatenate([self32[:, new_blk:new_blk + 1]] * N_HEADS, axis=0) > 0.5
    kvn = kvn_ref[...].astype(F32)
    s_list, m_list = [], []
    for g in range(n_grp):
        s_list.append(_dot_nt(q_b, group_rows(g, 2)) * ATT_SCALE)
        kpos = g * gw + lax.broadcasted_iota(jnp.int32, (rows, gw), 1)
        m_list.append(sel_past4[:, g * gw:(g + 1) * gw] & (kpos <= qpos4))
    npos = past + lax.broadcasted_iota(jnp.int32, (rows, t_new), 1)
    causal_new = npos <= qpos4
    s_list.append(_dot_nt(q_all, kvn[:, 2 * HD:3 * HD]) * ATT_SCALE)
    m_list.append(sel_new & causal_new)
    es, invs = _softmax_parts(s_list, m_list)
    o_sel = _dot(es[n_grp], kvn[:, 3 * HD:4 * HD])
    for g in range(n_grp):
        o_sel = o_sel + _dot(es[g].astype(BF16), group_rows(g, 3))
    o_sel = o_sel * invs
    w_buf = win_ref.shape[0] // 2
    kwb = win_ref[pl.ds(0, w_buf, stride=2), :].astype(BF16)
    vwb = win_ref[pl.ds(1, w_buf, stride=2), :].astype(BF16)
    wpos = past - w_buf + lax.broadcasted_iota(jnp.int32, (rows, w_buf), 1)
    wmask = (wpos <= qpos4) & (wpos > qpos4 - WINDOW) & (wpos >= 0)
    sw_list = [_dot_nt(q_b, kwb) * ATT_SCALE, _dot_nt(q_all, kvn[:, 4 * HD:5 * HD]) * ATT_SCALE]
    ew, invw = _softmax_parts(sw_list, [wmask, causal_new & (npos > qpos4 - WINDOW)])
    o_win = (_dot(ew[0].astype(BF16), vwb) + _dot(ew[1], kvn[:, 5 * HD:6 * HD])) * invw
    for h in range(N_HEADS):
        r = slice(h * t_new, (h + 1) * t_new)
        g0 = so[:, SM_DG + 3 * h:SM_DG + 3 * h + 1]
        g1 = so[:, SM_DG + 3 * h + 1:SM_DG + 3 * h + 2]
        g2 = so[:, SM_DG + 3 * h + 2:SM_DG + 3 * h + 3]
        o_ref[:, h * HD:(h + 1) * HD] = g0 * o_cmp[r] + g1 * o_sel[r] + g2 * o_win[r]
    keep = 2 * (w_buf - t_new)
    wout_ref[0:keep, :] = win_ref[2 * t_new:2 * w_buf, :]
    wout_ref[keep:2 * w_buf, :] = winn_ref[...]


def _nsa_sample(layer, depth, prev, page_table, nq, nkv, win_new, so, cache4, win_state4, wk_b, wv_b, kcg):
    b, t, _ = nq.shape
    n_pages = page_table.shape[1]
    past = n_pages * PAGE_SIZE
    nc = past // CMP_BLOCK
    w2 = win_state4.shape[2]
    e_mat = _expand_matrix(past)
    pair = _pair_matrix(nc)

    def tok(rows, width):
        return pl.BlockSpec((None, rows, width), lambda bi, pt: (bi, 0, 0))

    def const(shape):
        return pl.BlockSpec(shape, lambda bi, pt: (0, 0))

    in_specs = [tok(t, MIX_W), tok(t, 768), tok(2 * t, HD), tok(t, 128),
                pl.BlockSpec((None, None, w2, 128), lambda bi, pt: (layer, bi, 0, 0)),
                const((CMP_BLOCK, 128)), const((CMP_BLOCK, 128)), const((1, 128)),
                const((128, past)), const((nc, 128))]
    in_specs += [pl.BlockSpec((None, None, PAGE_SIZE * 4, 128), lambda bi, pt, p=p: (layer, pt[bi, p], 0, 0))
                 for p in range(n_pages)]
    n_in = N_NSA_S_IN + n_pages
    c_specs, c_args, aliases = _carry_args(prev, n_in, (1,))
    grid_spec = pltpu.PrefetchScalarGridSpec(
        num_scalar_prefetch=1, grid=(b,), in_specs=in_specs + c_specs,
        out_specs=(pl.BlockSpec((None, t, MIX_W), lambda bi, pt: (bi, 0, 0)),
                   pl.BlockSpec((None, None, w2, 128), lambda bi, pt: (layer, bi, 0, 0))),
        scratch_shapes=[pltpu.VMEM((nc, HD), F32), pltpu.VMEM((nc, HD), F32)])
    return pl.pallas_call(
        _with_carried_outputs(functools.partial(_nsa_sample_kernel, n_pages=n_pages, t_new=t), n_in, len(c_args)),
        grid_spec=grid_spec,
        out_shape=(jax.ShapeDtypeStruct((b, t, MIX_W), F32),
                   jax.ShapeDtypeStruct((depth, b, w2, 128), F32)),
        input_output_aliases=aliases,
        compiler_params=_cparams(("arbitrary",), 48),
        name="nsa_sample",
    )(page_table, nq, nkv, win_new, so, win_state4, wk_b, wv_b, kcg, e_mat, pair,
      *([cache4] * n_pages), *c_args)


OUT_TM = 256


def _out_kernel(x_ref, oa_ref, ob_ref, oc_ref, od_ref, z_ref, g0_ref, g1_ref, g2_ref, g3_ref,
                wb_ref, wo_ref, y_ref):
    acc = None
    for g, (o_r, g_r) in enumerate(((oa_ref, g0_ref), (ob_ref, g1_ref), (oc_ref, g2_ref), (od_ref, g3_ref))):
        z = z_ref[:, g * MIX_W:(g + 1) * MIX_W]
        br = (o_r[...] * (z * _sigmoid(z))).astype(BF16)
        term = _sigmoid(g_r[...]) * _dot(br, wb_ref[g])
        acc = term if acc is None else acc + term
    y_ref[...] = x_ref[...] + _dot(acc.astype(BF16), wo_ref[...])


def _out_proj(x2d, oa, ob, oc, od, h2d, wb_all, wo_all, layer):
    m = x2d.shape[0]
    tm = OUT_TM

    def rows(width):
        return pl.BlockSpec((tm, width), lambda i: (i, 0))

    def gate(g):
        return pl.BlockSpec((tm, D_MODEL), lambda i, g=g: (i, COL_GATE // D_MODEL + g))

    single = pl.Buffered(1)
    return pl.pallas_call(
        _out_kernel,
        grid=(m // tm,),
        in_specs=[rows(D_MODEL), rows(MIX_W), rows(MIX_W), rows(MIX_W), rows(MIX_W),
                  rows(D_MODEL), gate(0), gate(1), gate(2), gate(3),
                  pl.BlockSpec((None, N_HEADS, MIX_W, D_MODEL), lambda i: (layer, 0, 0, 0), pipeline_mode=single),
                  pl.BlockSpec((None, D_MODEL, D_MODEL), lambda i: (layer, 0, 0), pipeline_mode=single)],
        out_specs=rows(D_MODEL),
        out_shape=jax.ShapeDtypeStruct((m, D_MODEL), F32),
        compiler_params=_cparams(("parallel",), 56),
        name="out_proj",
    )(x2d, oa, ob, oc, od, h2d, h2d, h2d, h2d, h2d, wb_all, wo_all)


def _layer_params(l, p):
    zeros = jnp.zeros((2, HD), F32)
    gains = jnp.concatenate([p["fox_qg"][l][None], p["fox_kg"][l][None], p["nsa_qg"][l][None],
                             p["nsa_ksg"][l][None], p["nsa_kwg"][l][None], p["nsa_kcg"][l][None], zeros], axis=0)
    bias = jnp.concatenate([p["fox_bf"][l], p["mlstm_bi"][l], p["mlstm_bf"][l], jnp.zeros((GLA_RANK,), F32),
                            p["nsa_bg"][l].reshape(-1), jnp.zeros((128 - SM_DG - 12,), F32)]).reshape(1, 128)
    w2pad = jnp.zeros((128, 256), F32).at[SM_CA:SM_CA + GLA_RANK].set(p["gla_w2"][l])
    return dict(
        norm_g=p["norm_g"][l], gains=gains, bias=bias,
        wk_b=jnp.broadcast_to(p["nsa_wk"][l][:, None], (CMP_BLOCK, 128)),
        wv_b=jnp.broadcast_to(p["nsa_wv"][l][:, None], (CMP_BLOCK, 128)),
        kcg=p["nsa_kcg"][l].reshape(1, HD),
        mlstm_hg=p["mlstm_hg"][l], w2pad=w2pad, gla_b2=p["gla_b2"][l], gla_hg=p["gla_hg"][l])


def _layer(x3d, lp, shared, st, prev, *, is_prompt, layer, depth):
    b, t, _ = x3d.shape
    m = b * t
    x2d = x3d.reshape(m, D_MODEL)
    xn = _rmsnorm(x2d, lp["norm_g"])
    h2d = _in_proj(xn, shared["w_in"], layer)
    h3d = h2d.reshape(b, t, N_PROJ)
    (foxkv, nsakv, win, so, fq, fk, fv, nq, nkv, kcmp, vcmp) = _prep(
        h2d, lp["gains"], lp["bias"], lp["wk_b"], lp["wv_b"], layer, depth,
        None if prev is None else (prev["fox_kv"], prev["nsa_kv"]))
    so3 = so.reshape(b, t, 128)
    fq3, fk3, fv3 = fq.reshape(b, t, MIX_W), fk.reshape(b, t, MIX_W), fv.reshape(b, t, MIX_W)
    nq3 = nq.reshape(b, t, MIX_W)
    nkv3 = nkv.reshape(b, t, 768)
    chunk = min(64, t)
    carried = dict(fox_kv=foxkv, nsa_kv=nsakv)
    if is_prompt:
        lf_t = so3[:, :, SM_AF:SM_AF + N_HEADS].transpose(0, 2, 1).reshape(b * N_HEADS, t)
        f_row = _cumsum_rows(lf_t).reshape(b, N_HEADS, t)
        f_col = f_row.transpose(0, 2, 1)
        o_a = _fox_prompt(fq3, fk3, fv3, f_col, f_row)
        win_pad = jnp.pad(nkv3[:, :, 4 * HD:], ((0, 0), (WINDOW, 0), (0, 0)))
        o_d = _nsa_prompt(nq3, kcmp.reshape(b, t // CMP_BLOCK, HD), vcmp.reshape(b, t // CMP_BLOCK, HD),
                          nkv3, win_pad, so3)
        win_out = win.reshape(b, t, 2, HD)[:, -min(WINDOW, t):]
        bb = min(BATCH_UNROLL, b)
    else:
        o_a = _fox_sample(layer, shared["page_table"], fq3, fk3, fv3, so3, shared["fox_kv4"], shared["fox_lf4"])
        o_d, win_out = _nsa_sample(layer, depth, None if prev is None else (prev["win"],),
                                   shared["page_table"], nq3, nkv3, win.reshape(b, 2 * t, HD), so3,
                                   shared["nsa_kv4"], shared["nsa_win4"], lp["wk_b"], lp["wv_b"], lp["kcg"])
        bb = 2 * BATCH_UNROLL
    o_b, c_new, n_new, m_new = _mlstm(
        h3d, so3, lp["mlstm_hg"], st["C"], st["n"], st["m"], st["layer"], layer, depth,
        None if prev is None else (prev["C"], prev["n"], prev["m"]), bb, chunk)
    o_c, s_new = _gla(h3d, so3, lp["w2pad"], lp["gla_b2"], lp["gla_hg"], st["S"], st["layer"], layer, depth,
                      None if prev is None else (prev["S"],), bb, chunk)
    y2d = _out_proj(x2d, o_a.reshape(m, MIX_W), o_b.reshape(m, MIX_W), o_c.reshape(m, MIX_W),
                    o_d.reshape(m, MIX_W), h2d, shared["w_branch"], shared["w_out"], layer)
    carried.update(win=win_out, C=c_new, n=n_new, m=m_new, S=s_new)
    return y2d.reshape(b, t, D_MODEL), carried, so3[:, :, SM_AF:SM_AF + N_HEADS]


def kernel(x_prompt, x_sample, cache_fox_kv, cache_fox_logf, cache_nsa_kv, page_table, state_nsa_win, state_mlstm_C, state_mlstm_n, state_mlstm_m, state_gla_S, norm_g, w_in, fox_qg, fox_kg, fox_bf, mlstm_bi, mlstm_bf, mlstm_hg, gla_w2, gla_b2, gla_hg, nsa_qg, nsa_kcg, nsa_ksg, nsa_kwg, nsa_wk, nsa_wv, nsa_bg, w_branch, w_out):
    params = dict(norm_g=norm_g, fox_qg=fox_qg, fox_kg=fox_kg, fox_bf=fox_bf, mlstm_bi=mlstm_bi,
                  mlstm_bf=mlstm_bf, mlstm_hg=mlstm_hg, gla_w2=gla_w2, gla_b2=gla_b2, gla_hg=gla_hg,
                  nsa_qg=nsa_qg, nsa_kcg=nsa_kcg, nsa_ksg=nsa_ksg, nsa_kwg=nsa_kwg, nsa_wk=nsa_wk,
                  nsa_wv=nsa_wv, nsa_bg=nsa_bg)
    depth = w_in.shape[0]
    bp, tp, _ = x_prompt.shape
    db, ts, _ = x_sample.shape
    n_pool = cache_fox_kv.shape[1]
    w_buf = state_nsa_win.shape[2]
    shared = dict(
        w_in=_permute_w_in(w_in), w_branch=w_branch.astype(BF16), w_out=w_out.astype(BF16),
        page_table=page_table,
        fox_kv4=cache_fox_kv.reshape(depth, n_pool, PAGE_SIZE * 2 * N_HEADS, HD),
        fox_lf4=cache_fox_logf.transpose(0, 1, 3, 2),
        nsa_kv4=cache_nsa_kv.reshape(depth, n_pool, PAGE_SIZE * 4, HD),
        nsa_win4=state_nsa_win.reshape(depth, db, w_buf * 2, HD))
    st_p = dict(C=jnp.zeros((1, bp, N_HEADS, HD, HD), F32), n=jnp.zeros((1, bp, N_HEADS, HD), F32),
                m=jnp.zeros((1, bp, N_HEADS, HD), F32), S=jnp.zeros((1, bp, N_HEADS, GLA_DK, GLA_DV), F32),
                layer=0)
    m_s = jnp.broadcast_to(state_mlstm_m[..., None], state_mlstm_m.shape + (HD,))
    y_p, y_s = x_prompt, x_sample
    prev_p = prev_s = None
    lf_p, lf_s, win_p = [], [], []
    for l in range(depth):
        lp = _layer_params(l, params)
        st_s = dict(C=state_mlstm_C, n=state_mlstm_n, m=m_s, S=state_gla_S, layer=l)
        y_p, prev_p, lf = _layer(y_p, lp, shared, st_p, prev_p, is_prompt=True, layer=l, depth=depth)
        lf_p.append(lf)
        win_p.append(prev_p["win"])
        y_s, prev_s, lf = _layer(y_s, lp, shared, st_s, prev_s, is_prompt=False, layer=l, depth=depth)
        lf_s.append(lf)

    def finish(c, b, t, win, lf):
        return (c["fox_kv"].reshape(depth, b, t, 2, N_HEADS, HD), jnp.stack(lf, axis=0),
                c["nsa_kv"].reshape(depth, b, t, 4, HD), win,
                c["C"], c["n"], c["m"][..., 0], c["S"])

    out_p = finish(prev_p, bp, tp, jnp.stack(win_p, axis=0), lf_p)
    out_s = finish(prev_s, db, ts, prev_s["win"].reshape(depth, db, w_buf, 2, HD), lf_s)
    return (y_p, y_s) + out_p + out_s
```

```python
import functools

import jax
import jax.numpy as jnp
from jax import lax
from jax.experimental import pallas as pl
from jax.experimental.pallas import tpu as pltpu

F32 = jnp.float32
BF16 = jnp.bfloat16
HIGHEST = lax.Precision.HIGHEST

D_MODEL = 2048
MIX_W = 512
HD = 128
N_HEADS = 4
GLA_DK = 64
GLA_DK_SHIFT = 6
GLA_DV = 128
GLA_RANK = 16
GLA_TAU = 16.0
CMP_BLOCK = 32
SEL_BLOCK = 64
SEL_SHIFT = 6
N_SEL = 16
WINDOW = 512
PAGE_SIZE = 128
EPS = 1e-6
ATT_SCALE = HD ** -0.5
NEG_INF = float("-inf")
KEY_CLASS = 512
BF16_ROWS = 16

_REF_LAYOUT = (
    ("a_q", 512), ("a_k", 512), ("a_v", 512), ("a_f", 4), ("a_z", 512),
    ("b_q", 512), ("b_k", 512), ("b_v", 512), ("b_i", 4), ("b_f", 4), ("b_o", 512), ("b_z", 512),
    ("c_q", 256), ("c_k", 256), ("c_v", 512), ("c_a", 16), ("c_z", 512),
    ("d_q", 512), ("d_kv", 768), ("d_g", 12), ("d_z", 512),
    ("gate", 8192),
)
N_REF = sum(w for _, w in _REF_LAYOUT)
_MY_LAYOUT = (
    "a_z", "b_z", "c_z", "d_z", "gate",
    "a_q", "a_k", "a_v", "d_q", "b_q", "b_k", "b_v", "b_o", "c_v", "c_q", "c_k", "d_kv",
    "a_f", "b_i", "b_f", "c_a", "d_g",
)
N_PROJ = 16384
COL_Z = 0
COL_GATE = 2048
COL_AQ, COL_AK, COL_AV, COL_DQ = 10240, 10752, 11264, 11776
COL_BQ, COL_BK, COL_BV, COL_BO = 12288, 12800, 13312, 13824
COL_CV, COL_CQ, COL_CK, COL_DKV, COL_SMALL = 14336, 14848, 15104, 15360, 16128
SM_AF, SM_BI, SM_BF, SM_CA, SM_DG = 0, 4, 8, 12, 28


def _cparams(sem, vmem_mb):
    return pltpu.CompilerParams(dimension_semantics=sem, vmem_limit_bytes=vmem_mb * 1024 * 1024)


def _log_sigmoid(x):
    return jnp.minimum(x, 0.0) - jnp.log(1.0 + jnp.exp(-jnp.abs(x)))


def _sigmoid(x):
    return 0.5 * jnp.tanh(0.5 * x) + 0.5


def _rms_lanes(x, g):
    return (x * lax.rsqrt(jnp.mean(x * x, axis=-1, keepdims=True) + EPS)) * g


def _dot(a, b):
    return jnp.dot(a, b, preferred_element_type=F32)


def _dot_nt(a, b):
    return lax.dot_general(a, b, (((1,), (1,)), ((), ())), preferred_element_type=F32)


def _dot_tn(a, b):
    return lax.dot_general(a, b, (((0,), (0,)), ((), ())), preferred_element_type=F32)


def _dot_exact(a, b):
    return jnp.dot(a, b, precision=HIGHEST, preferred_element_type=F32)


def _eye_mask(n):
    return lax.broadcasted_iota(jnp.int32, (n, n), 0) == lax.broadcasted_iota(jnp.int32, (n, n), 1)


def _col_to_row(c):
    n = c.shape[0]
    return jnp.sum(jnp.where(_eye_mask(n), c, 0.0), axis=0, keepdims=True)


def _row_to_col(r):
    n = r.shape[1]
    return jnp.sum(jnp.where(_eye_mask(n), r, 0.0), axis=1, keepdims=True)


def _tril_ones(n):
    r = lax.broadcasted_iota(jnp.int32, (n, n), 0)
    c = lax.broadcasted_iota(jnp.int32, (n, n), 1)
    return jnp.where(c <= r, 1.0, 0.0).astype(F32)


def _cumsum_lanes(x):
    n = x.shape[-1]
    lane = lax.broadcasted_iota(jnp.int32, x.shape, x.ndim - 1)
    s = 1
    while s < n:
        x = x + jnp.where(lane >= s, pltpu.roll(x, s, axis=x.ndim - 1), 0.0)
        s *= 2
    return x


def _pad_rows(x, rows):
    if x.shape[0] == rows:
        return x
    return jnp.concatenate([x, jnp.zeros((rows - x.shape[0], x.shape[1]), x.dtype)], axis=0)


def _softmax_parts(s_list, mask_list):
    masked = [jnp.where(mk, s, NEG_INF) for s, mk in zip(s_list, mask_list)]
    m = masked[0].max(axis=-1, keepdims=True)
    for s in masked[1:]:
        m = jnp.maximum(m, s.max(axis=-1, keepdims=True))
    m = jnp.where(m > NEG_INF, m, 0.0)
    es = [jnp.exp(s - m) for s in masked]
    tot = es[0].sum(axis=-1, keepdims=True)
    for e in es[1:]:
        tot = tot + e.sum(axis=-1, keepdims=True)
    return es, 1.0 / jnp.maximum(tot, 1e-30)


def _with_carried_outputs(kernel_fn, n_in, n_carried):
    if not n_carried:
        return kernel_fn

    def wrapped(*refs):
        return kernel_fn(*refs[:n_in], *refs[n_in + n_carried:])

    return wrapped


def _carry_args(prev, first_in_index, out_indices):
    if prev is None:
        return [], [], {}
    specs = [pl.BlockSpec(memory_space=pl.ANY) for _ in prev]
    aliases = {first_in_index + k: oi for k, oi in enumerate(out_indices)}
    return specs, list(prev), aliases


def _segments():
    offs = {}
    off = 0
    for name, width in _REF_LAYOUT:
        offs[name] = (off, width)
        off += width
    segs = []
    dst = 0
    for name in _MY_LAYOUT:
        src, width = offs[name]
        segs.append((src, dst, width))
        dst += width
    return segs, dst


def _permute_kernel(wt_ref, o_ref, small_ref):
    segs, _ = _segments()
    for src, dst, width in segs:
        if width >= 128:
            o_ref[dst:dst + width, :] = wt_ref[src:src + width, :].astype(BF16)
    small_ref[...] = jnp.zeros(small_ref.shape, F32)
    for src, dst, width in segs:
        if width < 128:
            small_ref[dst - COL_SMALL:dst - COL_SMALL + width, :] = wt_ref[src:src + width, :]
    o_ref[COL_SMALL:N_PROJ, :] = small_ref[...].astype(BF16)


def _permute_w_in(w_in):
    depth, d, n = w_in.shape
    tk = 128
    wt = jnp.transpose(w_in, (0, 2, 1))
    return pl.pallas_call(
        _permute_kernel,
        grid=(depth, d // tk),
        in_specs=[pl.BlockSpec((None, n, tk), lambda l, i: (l, 0, i))],
        out_specs=pl.BlockSpec((None, N_PROJ, tk), lambda l, i: (l, 0, i)),
        out_shape=jax.ShapeDtypeStruct((depth, N_PROJ, d), BF16),
        scratch_shapes=[pltpu.VMEM((N_PROJ - COL_SMALL, tk), F32)],
        compiler_params=_cparams(("parallel", "parallel"), 40),
        name="permute_w_in",
    )(wt)


def _norm_kernel(x_ref, g_ref, o_ref):
    o_ref[...] = _rms_lanes(x_ref[...], g_ref[...]).astype(o_ref.dtype)


def _rmsnorm(x2d, g):
    m, d = x2d.shape
    tm = min(512, m)
    return pl.pallas_call(
        _norm_kernel,
        grid=(m // tm,),
        in_specs=[pl.BlockSpec((tm, d), lambda i: (i, 0)), pl.BlockSpec((1, d), lambda i: (0, 0))],
        out_specs=pl.BlockSpec((tm, d), lambda i: (i, 0)),
        out_shape=jax.ShapeDtypeStruct((m, d), BF16),
        compiler_params=_cparams(("parallel",), 32),
        name="rmsnorm",
    )(x2d, g.reshape(1, d))


def _mm_kernel(a_ref, bt_ref, o_ref):
    o_ref[...] = _dot_nt(a_ref[...], bt_ref[...])


def _in_proj(a, wt_all, layer):
    m, k = a.shape
    n = wt_all.shape[1]
    tm, tn = min(1024, m), 1024
    return pl.pallas_call(
        _mm_kernel,
        grid=(m // tm, n // tn),
        in_specs=[pl.BlockSpec((tm, k), lambda i, j: (i, 0)),
                  pl.BlockSpec((None, tn, k), lambda i, j: (layer, j, 0))],
        out_specs=pl.BlockSpec((tm, tn), lambda i, j: (i, j)),
        out_shape=jax.ShapeDtypeStruct((m, n), F32),
        compiler_params=_cparams(("parallel", "arbitrary"), 48),
        name="in_proj",
    )(a, wt_all)


PREP_TM = 256
N_PREP_IN = 10


def _prep_kernel(aq_ref, ak_ref, av_ref, dq_ref, dkv_ref, sm_ref, gains_ref, bias_ref, wk_ref, wv_ref,
                 foxkv_ref, nsakv_ref, win_ref, so_ref, fq_ref, fk_ref, fv_ref, nq_ref, nkv_ref,
                 kcmp_ref, vcmp_ref):
    tm = PREP_TM
    g_fq, g_fk, g_nq = gains_ref[0:1, :], gains_ref[1:2, :], gains_ref[2:3, :]
    g_ks, g_kw, g_kc = gains_ref[3:4, :], gains_ref[4:5, :], gains_ref[5:6, :]
    for h in range(N_HEADS):
        sl = slice(h * HD, (h + 1) * HD)
        fq_ref[:, sl] = _rms_lanes(aq_ref[:, sl], g_fq).astype(BF16)
        kn = _rms_lanes(ak_ref[:, sl], g_fk)
        foxkv_ref[pl.ds(h, tm, stride=2 * N_HEADS), :] = kn
        fk_ref[:, sl] = kn.astype(BF16)
        v = av_ref[:, sl]
        foxkv_ref[pl.ds(N_HEADS + h, tm, stride=2 * N_HEADS), :] = v
        fv_ref[:, sl] = v.astype(BF16)
        nq_ref[:, sl] = _rms_lanes(dq_ref[:, sl], g_nq).astype(BF16)
    kc = dkv_ref[:, 0:128]
    vc = dkv_ref[:, 128:256]
    ks = _rms_lanes(dkv_ref[:, 256:384], g_ks)
    vs = dkv_ref[:, 384:512]
    kw = _rms_lanes(dkv_ref[:, 512:640], g_kw)
    vw = dkv_ref[:, 640:768]
    for j, a in enumerate((kc, vc, ks, vs)):
        nsakv_ref[pl.ds(j, tm, stride=4), :] = a
    win_ref[pl.ds(0, tm, stride=2), :] = kw
    win_ref[pl.ds(1, tm, stride=2), :] = vw
    for j, a in enumerate((kc, vc, ks, vs, kw, vw)):
        nkv_ref[:, j * HD:(j + 1) * HD] = a.astype(BF16)
    nb = tm // CMP_BLOCK
    kcs = jnp.sum(kc.reshape(nb, CMP_BLOCK, HD) * wk_ref[...][None], axis=1)
    kcmp_ref[...] = _rms_lanes(kcs, g_kc)
    vcmp_ref[...] = jnp.sum(vc.reshape(nb, CMP_BLOCK, HD) * wv_ref[...][None], axis=1)
    y = sm_ref[...] + bias_ref[...]
    lane = lax.broadcasted_iota(jnp.int32, y.shape, 1)
    is_ls = (lane < SM_BI) | ((lane >= SM_BF) & (lane < SM_CA))
    is_sg = (lane >= SM_DG) & (lane < SM_DG + 12)
    so_ref[...] = jnp.where(is_ls, _log_sigmoid(y), jnp.where(is_sg, _sigmoid(y), y))


def _prep(h2d, gains, bias, wk_b, wv_b, layer, depth, prev):
    m = h2d.shape[0]
    tm = PREP_TM

    def col(width, off):
        blk = off // width
        return pl.BlockSpec((tm, width), lambda i, blk=blk: (i, blk))

    def full(shape):
        return pl.BlockSpec(shape, lambda i: (0, 0))

    def rows(width):
        return pl.BlockSpec((tm, width), lambda i: (i, 0))

    def stacked(slots):
        return pl.BlockSpec((None, tm * slots, HD), lambda i: (layer, i, 0))

    out_shapes = (
        jax.ShapeDtypeStruct((depth, m * 8, HD), F32),
        jax.ShapeDtypeStruct((depth, m * 4, HD), F32),
        jax.ShapeDtypeStruct((m * 2, HD), F32),
        jax.ShapeDtypeStruct((m, 128), F32),
        jax.ShapeDtypeStruct((m, 512), BF16),
        jax.ShapeDtypeStruct((m, 512), BF16),
        jax.ShapeDtypeStruct((m, 512), BF16),
        jax.ShapeDtypeStruct((m, 512), BF16),
        jax.ShapeDtypeStruct((m, 768), BF16),
        jax.ShapeDtypeStruct((m // CMP_BLOCK, 128), F32),
        jax.ShapeDtypeStruct((m // CMP_BLOCK, 128), F32),
    )
    out_specs = (stacked(8), stacked(4), pl.BlockSpec((tm * 2, HD), lambda i: (i, 0)),
                 rows(128), rows(512), rows(512), rows(512), rows(512), rows(768),
                 pl.BlockSpec((tm // CMP_BLOCK, 128), lambda i: (i, 0)),
                 pl.BlockSpec((tm // CMP_BLOCK, 128), lambda i: (i, 0)))
    c_specs, c_args, aliases = _carry_args(prev, N_PREP_IN, (0, 1))
    return pl.pallas_call(
        _with_carried_outputs(_prep_kernel, N_PREP_IN, len(c_args)),
        grid=(m // tm,),
        in_specs=[col(512, COL_AQ), col(512, COL_AK), col(512, COL_AV), col(512, COL_DQ),
                  col(768, COL_DKV), col(128, COL_SMALL),
                  full((8, 128)), full((1, 128)), full((CMP_BLOCK, 128)), full((CMP_BLOCK, 128))] + c_specs,
        out_specs=out_specs,
        out_shape=out_shapes,
        input_output_aliases=aliases,
        compiler_params=_cparams(("parallel",), 32),
        name="prep",
    )(h2d, h2d, h2d, h2d, h2d, h2d, gains, bias, wk_b, wv_b, *c_args)


def _cumsum_rows_kernel(x_ref, o_ref):
    o_ref[...] = _cumsum_lanes(x_ref[...])


def _cumsum_rows(x):
    r, n = x.shape
    return pl.pallas_call(
        _cumsum_rows_kernel,
        grid=(1,),
        in_specs=[pl.BlockSpec((r, n), lambda i: (0, 0))],
        out_specs=pl.BlockSpec((r, n), lambda i: (0, 0)),
        out_shape=jax.ShapeDtypeStruct((r, n), F32),
        name="fox_cumsum",
    )(x)


FOX_TQ = 256


def _fox_prompt_kernel(q_ref, k_ref, v_ref, fc_ref, fr_ref, o_ref, *, t_len):
    tq = FOX_TQ
    i = pl.program_id(1)
    n_cls = -(-t_len // KEY_CLASS)
    cls = (i * tq + tq - 1) // KEY_CLASS

    def run(kl):
        qpos = i * tq + lax.broadcasted_iota(jnp.int32, (tq, kl), 0)
        kpos = lax.broadcasted_iota(jnp.int32, (tq, kl), 1)
        mask = kpos <= qpos
        for h in range(N_HEADS):
            sl = slice(h * HD, (h + 1) * HD)
            s = _dot_nt(q_ref[:, sl], k_ref[0:kl, sl]) * ATT_SCALE
            s = s + fc_ref[:, h:h + 1] - fr_ref[h:h + 1, 0:kl]
            (e,), inv = _softmax_parts([s], [mask])
            o_ref[:, sl] = _dot(e.astype(BF16), v_ref[0:kl, sl]) * inv

    for c in range(n_cls):
        pl.when(cls == c)(functools.partial(run, min((c + 1) * KEY_CLASS, t_len)))


def _fox_prompt(fq, fk, fv, f_col, f_row):
    b, t, _ = fq.shape
    tq = FOX_TQ
    return pl.pallas_call(
        functools.partial(_fox_prompt_kernel, t_len=t),
        grid=(b, t // tq),
        in_specs=[pl.BlockSpec((None, tq, MIX_W), lambda bi, i: (bi, i, 0)),
                  pl.BlockSpec((None, t, MIX_W), lambda bi, i: (bi, 0, 0)),
                  pl.BlockSpec((None, t, MIX_W), lambda bi, i: (bi, 0, 0)),
                  pl.BlockSpec((None, tq, N_HEADS), lambda bi, i: (bi, i, 0)),
                  pl.BlockSpec((None, N_HEADS, t), lambda bi, i: (bi, 0, 0))],
        out_specs=pl.BlockSpec((None, tq, MIX_W), lambda bi, i: (bi, i, 0)),
        out_shape=jax.ShapeDtypeStruct((b, t, MIX_W), F32),
        compiler_params=_cparams(("parallel", "arbitrary"), 48),
        name="fox_prompt",
    )(fq, fk, fv, f_col, f_row)


PAGE_GROUP = 2


def _fox_sample_kernel(pt_ref, q_ref, kn_ref, vn_ref, so_ref, spread_ref, *rest, n_pages, t_new):
    del pt_ref
    kv_pages = rest[:n_pages]
    lf_pages = rest[n_pages:2 * n_pages]
    o_ref = rest[2 * n_pages]
    lf_scr = rest[2 * n_pages + 1]
    past = n_pages * PAGE_SIZE
    w2 = 2 * PAGE_SIZE
    n_rows = n_pages * N_HEADS
    for p in range(n_pages):
        lf_scr[p * N_HEADS:(p + 1) * N_HEADS, :] = lf_pages[p][...]
    lf_rows = lf_scr[...]
    local = _dot_exact(lf_rows, spread_ref[...])
    tot = _dot_exact(lf_rows, jnp.ones((PAGE_SIZE, 128), F32))
    rr = lax.broadcasted_iota(jnp.int32, (n_rows, n_rows), 0)
    cc = lax.broadcasted_iota(jnp.int32, (n_rows, n_rows), 1)
    same_head = (rr & (N_HEADS - 1)) == (cc & (N_HEADS - 1))
    earlier = jnp.where(same_head & (cc < rr), 1.0, 0.0)
    up_to = jnp.where(same_head & (cc <= rr), 1.0, 0.0)
    f_int = local + _dot_exact(earlier, tot)[:, 0:1]
    f_total = _dot_exact(up_to, tot)[n_rows - N_HEADS:n_rows, 0:1]
    f_new_all = _dot_exact(_tril_ones(t_new), so_ref[...])
    lane = lax.broadcasted_iota(jnp.int32, (t_new, w2), 1)
    is_key = (lane & 1) == 0
    kpos0 = lane >> 1
    tpos = lax.broadcasted_iota(jnp.int32, (t_new, w2), 0) + past
    rn = lax.broadcasted_iota(jnp.int32, (t_new, t_new), 0)
    cn = lax.broadcasted_iota(jnp.int32, (t_new, t_new), 1)

    def kv_rows(p, h):
        return kv_pages[p][pl.ds(h, w2, stride=N_HEADS), :].astype(BF16)

    heads = range(N_HEADS)
    sls = [slice(h * HD, (h + 1) * HD) for h in heads]
    qs = [q_ref[:, sl] for sl in sls]
    qk = [[_dot_nt(qs[h], kv_rows(p, h)) for p in range(n_pages)] for h in heads]
    qk_new = [_dot_nt(qs[h].astype(F32), kn_ref[:, sls[h]].astype(F32)) for h in heads]
    m_list = [is_key & (kpos0 + p * PAGE_SIZE <= tpos) for p in range(n_pages)] + [cn <= rn]
    weights = []
    for h in heads:
        fq = f_new_all[:, SM_AF + h:SM_AF + h + 1] + f_total[h:h + 1, :]
        s_list = [qk[h][p] * ATT_SCALE + fq - f_int[p * N_HEADS + h:p * N_HEADS + h + 1, :]
                  for p in range(n_pages)]
        s_list.append(qk_new[h] * ATT_SCALE + fq - _col_to_row(fq))
        weights.append(_softmax_parts(s_list, m_list))
    for h in heads:
        es, inv = weights[h]
        acc = _dot(es[n_pages], vn_ref[:, sls[h]].astype(F32))
        for p in range(n_pages):
            acc = acc + _dot(pltpu.roll(es[p], 1, axis=1).astype(BF16), kv_rows(p, h))
        o_ref[:, sls[h]] = acc * inv


def _spread_matrix():
    r = jnp.arange(PAGE_SIZE)[:, None]
    c = jnp.arange(2 * PAGE_SIZE)[None, :]
    return (2 * r <= c).astype(F32)


def _fox_sample(layer, page_table, fq, fk, fv, so, cache_kv4, cache_lf4):
    b, t, _ = fq.shape
    n_pages = page_table.shape[1]

    def page_spec(rows, p):
        return pl.BlockSpec((None, None, rows, 128), lambda bi, pt, p=p: (layer, pt[bi, p], 0, 0))

    tok = pl.BlockSpec((None, t, MIX_W), lambda bi, pt: (bi, 0, 0))
    in_specs = [tok, tok, tok, pl.BlockSpec((None, t, 128), lambda bi, pt: (bi, 0, 0)),
                pl.BlockSpec((PAGE_SIZE, 2 * PAGE_SIZE), lambda bi, pt: (0, 0))]
    in_specs += [page_spec(PAGE_SIZE * 2 * N_HEADS, p) for p in range(n_pages)]
    in_specs += [page_spec(N_HEADS, p) for p in range(n_pages)]
    grid_spec = pltpu.PrefetchScalarGridSpec(
        num_scalar_prefetch=1, grid=(b,), in_specs=in_specs,
        out_specs=pl.BlockSpec((None, t, MIX_W), lambda bi, pt: (bi, 0, 0)),
        scratch_shapes=[pltpu.VMEM((n_pages * N_HEADS, PAGE_SIZE), F32)])
    return pl.pallas_call(
        functools.partial(_fox_sample_kernel, n_pages=n_pages, t_new=t),
        grid_spec=grid_spec,
        out_shape=jax.ShapeDtypeStruct((b, t, MIX_W), F32),
        compiler_params=_cparams(("arbitrary",), 48),
        name="fox_sample",
    )(page_table, fq, fk, fv, so, _spread_matrix(), *([cache_kv4] * n_pages), *([cache_lf4] * n_pages))


N_MLSTM_IN = 9
BATCH_UNROLL = 4


def _mlstm_kernel(q_ref, k_ref, v_ref, og_ref, so_ref, hg_ref, c0_ref, n0_ref, m0_ref,
                  o_ref, c_ref, n_ref, m_ref, *, bb, chunk):
    L = chunk
    Lp = max(L, BF16_ROWS)

    @pl.when(pl.program_id(1) == 0)
    def _():
        c_ref[...] = c0_ref[...]
        n_ref[...] = n0_ref[...]
        m_ref[...] = m0_ref[...]

    tril = _tril_ones(Lp)
    causal = tril > 0.5
    hg = hg_ref[...]
    lane = lax.broadcasted_iota(jnp.int32, (Lp - L, 128), 1) if Lp > L else None

    def one(bi, states):
        so = so_ref[bi]
        if Lp > L:
            pad = jnp.where((lane >= SM_BI) & (lane < SM_BF), -1e30, 0.0)
            so = jnp.concatenate([so, pad], axis=0)
        csum = _dot_exact(tril, so)
        heads = range(N_HEADS)
        sls = [slice(h * HD, (h + 1) * HD) for h in heads]
        q = [_pad_rows(q_ref[bi, :, sl], Lp) for sl in sls]
        k = [_pad_rows(k_ref[bi, :, sl], Lp) * ATT_SCALE for sl in sls]
        vb = [_pad_rows(v_ref[bi, :, sl], Lp).astype(BF16) for sl in sls]
        qb = [x.astype(BF16) for x in q]
        ig = [so[:, SM_BI + h:SM_BI + h + 1] for h in heads]
        bcol = [csum[:, SM_BF + h:SM_BF + h + 1] for h in heads]
        c_st = [states[h][0] for h in heads]
        n_st = [states[h][1] for h in heads]
        m_st = [states[h][2] for h in heads]
        qk_raw = [_dot_nt(qb[h], k[h].astype(BF16)) for h in heads]
        q_c = [_dot(qb[h], c_st[h].astype(BF16)) for h in heads]
        dmat = [jnp.where(causal, bcol[h] - _col_to_row(bcol[h]) + _col_to_row(ig[h]), NEG_INF) for h in heads]
        inter = [bcol[h] + m_st[h] for h in heads]
        m_t = [jnp.maximum(inter[h], dmat[h].max(axis=-1, keepdims=True)) for h in heads]
        w_inter = [jnp.exp(inter[h] - m_t[h]) for h in heads]
        qk = [qk_raw[h] * jnp.exp(dmat[h] - m_t[h]) for h in heads]
        qk_v = [_dot(qk[h].astype(BF16), vb[h]) for h in heads]
        b_last = [bcol[h][Lp - 1:Lp, :] for h in heads]
        g_end = [b_last[h] - bcol[h] + ig[h] for h in heads]
        m_new = [jnp.maximum(b_last[h] + m_st[h], g_end[h].max(axis=0, keepdims=True)) for h in heads]
        a_prev = [jnp.exp(b_last[h] + m_st[h] - m_new[h]) for h in heads]
        kw = [k[h] * jnp.exp(g_end[h] - m_new[h]) for h in heads]
        k_v = [_dot_tn(kw[h].astype(BF16), vb[h]) for h in heads]
        new_states = []
        for h in heads:
            num = w_inter[h] * q_c[h] + qk_v[h]
            den = (w_inter[h] * jnp.sum(q[h] * n_st[h], axis=-1, keepdims=True)
                   + jnp.sum(qk[h], axis=-1, keepdims=True))
            hout = num / jnp.maximum(jnp.abs(den), jnp.exp(-m_t[h]))
            new_states.append((a_prev[h] * c_st[h] + k_v[h],
                               a_prev[h] * n_st[h] + jnp.sum(kw[h], axis=0, keepdims=True),
                               jnp.broadcast_to(m_new[h], (1, HD))))
            hn = _rms_lanes(hout[0:L], hg)
            o_ref[bi, :, sls[h]] = hn * _sigmoid(og_ref[bi, :, sls[h]])
        return new_states

    unroll = min(BATCH_UNROLL, bb)

    def group(g, carry):
        idx = [g * unroll + u for u in range(unroll)]
        old = [[(c_ref[bi, h], n_ref[bi, h:h + 1, :], m_ref[bi, h:h + 1, 0:1]) for h in range(N_HEADS)]
               for bi in idx]
        new = [one(bi, st) for bi, st in zip(idx, old)]
        for bi, st in zip(idx, new):
            for h, (c_new, n_new, m_new) in enumerate(st):
                c_ref[bi, h] = c_new
                n_ref[bi, h:h + 1, :] = n_new
                m_ref[bi, h:h + 1, :] = m_new
        return carry

    if bb == unroll:
        group(0, 0)
    else:
        lax.fori_loop(0, bb // unroll, group, 0)


def _mlstm(h3d, so3d, hg, c0, n0, m0, state_layer, layer, depth, prev, bb, chunk):
    b, t, _ = h3d.shape
    nc = t // chunk

    def col(off):
        blk = off // MIX_W
        return pl.BlockSpec((bb, chunk, MIX_W), lambda bi, c, blk=blk: (bi, c, blk))

    def st_c(li):
        return pl.BlockSpec((None, bb, N_HEADS, HD, HD), lambda bi, c: (li, bi, 0, 0, 0))

    def st_n(li):
        return pl.BlockSpec((None, bb, N_HEADS, HD), lambda bi, c: (li, bi, 0, 0))

    c_specs, c_args, aliases = _carry_args(prev, N_MLSTM_IN, (1, 2, 3))
    return pl.pallas_call(
        _with_carried_outputs(functools.partial(_mlstm_kernel, bb=bb, chunk=chunk), N_MLSTM_IN, len(c_args)),
        grid=(b // bb, nc),
        in_specs=[col(COL_BQ), col(COL_BK), col(COL_BV), col(COL_BO),
                  pl.BlockSpec((bb, chunk, 128), lambda bi, c: (bi, c, 0)),
                  pl.BlockSpec((1, HD), lambda bi, c: (0, 0)),
                  st_c(state_layer), st_n(state_layer), st_n(state_layer)] + c_specs,
        out_specs=(pl.BlockSpec((bb, chunk, MIX_W), lambda bi, c: (bi, c, 0)),
                   st_c(layer), st_n(layer), st_n(layer)),
        out_shape=(jax.ShapeDtypeStruct((b, t, MIX_W), F32),
                   jax.ShapeDtypeStruct((depth, b, N_HEADS, HD, HD), F32),
                   jax.ShapeDtypeStruct((depth, b, N_HEADS, HD), F32),
                   jax.ShapeDtypeStruct((depth, b, N_HEADS, HD), F32)),
        input_output_aliases=aliases,
        compiler_params=_cparams(("parallel", "arbitrary"), 40),
        name="mlstm",
    )(h3d, h3d, h3d, h3d, so3d, hg.reshape(1, HD), c0, n0, m0, *c_args)


N_GLA_IN = 8
GLA_SUB = 16


def _gla_kernel(q_ref, k_ref, v_ref, so_ref, w2_ref, b2_ref, hg_ref, s0_ref, o_ref, s_ref, *, bb, chunk):
    L = chunk
    Lp = max(L, BF16_ROWS)
    sub = GLA_SUB

    @pl.when(pl.program_id(1) == 0)
    def _():
        s_ref[...] = s0_ref[...]

    tril = _tril_ones(Lp)
    hg = hg_ref[...]
    w2 = w2_ref[...].astype(BF16)
    b2 = b2_ref[...]
    q_scale = GLA_DK ** -0.5
    real_row = lax.broadcasted_iota(jnp.int32, (Lp, 1), 0) < L

    heads = range(N_HEADS)
    dkw = N_HEADS * GLA_DK
    head_of_lane = lax.broadcasted_iota(jnp.int32, (1, dkw), 1) >> GLA_DK_SHIFT
    zero_blk = jnp.zeros((GLA_DK, GLA_DV), F32)

    def one(bi, s_old):
        pre = _dot(_pad_rows(so_ref[bi], Lp).astype(BF16), w2) + b2
        la = jnp.where(real_row, _log_sigmoid(pre) / GLA_TAU, 0.0)
        bcs = _dot_exact(tril, la)
        q = _pad_rows(q_ref[bi], Lp) * q_scale
        k = _pad_rows(k_ref[bi], Lp)
        vb = _pad_rows(v_ref[bi], Lp).astype(BF16)
        s_bd = jnp.concatenate(
            [jnp.concatenate([s_old[h] if g == h else zero_blk for g in heads], axis=1) for h in heads], axis=0)
        inter = _dot((q * jnp.exp(bcs)).astype(BF16), s_bd.astype(BF16))
        for i in range(Lp // sub):
            r0 = i * sub
            hi = r0 + sub
            base = bcs[r0 - 1:r0, :] if i > 0 else jnp.zeros((1, dkw), F32)
            qi = q[r0:hi, :] * jnp.exp(bcs[r0:hi, :] - base)
            ke = (k[0:hi, :] * jnp.exp(base - bcs[0:hi, :])).astype(BF16)
            q_heads = jnp.concatenate([jnp.where(head_of_lane == h, qi, 0.0) for h in heads], axis=0)
            a = _dot_nt(q_heads.astype(BF16), ke)
            rr = (lax.broadcasted_iota(jnp.int32, (N_HEADS * sub, hi), 0) & (sub - 1)) + r0
            cc = lax.broadcasted_iota(jnp.int32, (N_HEADS * sub, hi), 1)
            a = jnp.where(cc <= rr, a, 0.0).astype(BF16)
            rows = min(hi, L) - r0
            for h in heads:
                vsl = slice(h * GLA_DV, (h + 1) * GLA_DV)
                oi = inter[r0:hi, vsl] + _dot(a[h * sub:(h + 1) * sub, :], vb[0:hi, vsl])
                o_ref[bi, r0:r0 + rows, vsl] = _rms_lanes(oi[0:rows], hg)
        b_end = bcs[Lp - 1:Lp, :]
        kd = (k * jnp.exp(b_end - bcs)).astype(BF16)
        upd = _dot_tn(kd, vb)
        dcol = _row_to_col(jnp.exp(b_end))
        return [dcol[h * GLA_DK:(h + 1) * GLA_DK, :] * s_old[h]
                + upd[h * GLA_DK:(h + 1) * GLA_DK, h * GLA_DV:(h + 1) * GLA_DV] for h in heads]

    unroll = min(BATCH_UNROLL, bb)

    def group(g, carry):
        idx = [g * unroll + u for u in range(unroll)]
        old = [[s_ref[bi, h] for h in heads] for bi in idx]
        new = [one(bi, s_old) for bi, s_old in zip(idx, old)]
        for bi, s_new in zip(idx, new):
            for h in heads:
                s_ref[bi, h] = s_new[h]
        return carry

    if bb == unroll:
        group(0, 0)
    else:
        lax.fori_loop(0, bb // unroll, group, 0)


def _gla(h3d, so3d, w2pad, b2, hg, s0, state_layer, layer, depth, prev, bb, chunk):
    b, t, _ = h3d.shape
    nc = t // chunk

    def st(li):
        return pl.BlockSpec((None, bb, N_HEADS, GLA_DK, GLA_DV), lambda bi, c: (li, bi, 0, 0, 0))

    c_specs, c_args, aliases = _carry_args(prev, N_GLA_IN, (1,))
    return pl.pallas_call(
        _with_carried_outputs(functools.partial(_gla_kernel, bb=bb, chunk=chunk), N_GLA_IN, len(c_args)),
        grid=(b // bb, nc),
        in_specs=[pl.BlockSpec((bb, chunk, 256), lambda bi, c: (bi, c, COL_CQ // 256)),
                  pl.BlockSpec((bb, chunk, 256), lambda bi, c: (bi, c, COL_CK // 256)),
                  pl.BlockSpec((bb, chunk, MIX_W), lambda bi, c: (bi, c, COL_CV // MIX_W)),
                  pl.BlockSpec((bb, chunk, 128), lambda bi, c: (bi, c, 0)),
                  pl.BlockSpec((128, 256), lambda bi, c: (0, 0)),
                  pl.BlockSpec((1, 256), lambda bi, c: (0, 0)),
                  pl.BlockSpec((1, GLA_DV), lambda bi, c: (0, 0)),
                  st(state_layer)] + c_specs,
        out_specs=(pl.BlockSpec((bb, chunk, MIX_W), lambda bi, c: (bi, c, 0)), st(layer)),
        out_shape=(jax.ShapeDtypeStruct((b, t, MIX_W), F32),
                   jax.ShapeDtypeStruct((depth, b, N_HEADS, GLA_DK, GLA_DV), F32)),
        input_output_aliases=aliases,
        compiler_params=_cparams(("parallel", "arbitrary"), 40),
        name="gla",
    )(h3d, h3d, h3d, so3d, w2pad, b2.reshape(1, 256), hg.reshape(1, GLA_DV), s0, *c_args)


def _select_blocks(imp, qpos_col, ns):
    r = imp.shape[0]
    j = lax.broadcasted_iota(jnp.int32, (r, 128), 1)
    cur = qpos_col >> SEL_SHIFT
    valid = j <= cur
    forced = (j == 0) | (valid & (j >= cur - 1))
    val = jnp.where(forced, 1e4, jnp.where(valid, imp, -1e4))
    val = jnp.where(j < ns, val, -3e38)
    rank = jnp.zeros((r, 128), F32)
    for i in range(ns):
        ci = val[:, i:i + 1]
        beats = (ci > val) | ((ci == val) & (j > i))
        rank = rank + jnp.where(beats, 1.0, 0.0)
    return (rank < float(min(N_SEL, ns))) & (j < ns)


def _pair_matrix(nc):
    c = jnp.arange(nc)[:, None]
    j = jnp.arange(128)[None, :]
    return (c // (SEL_BLOCK // CMP_BLOCK) == j).astype(F32)


def _expand_matrix(tk):
    j = jnp.arange(128)[:, None]
    s = jnp.arange(tk)[None, :]
    return (s // SEL_BLOCK == j).astype(BF16)


NSA_TQ = 128


def _nsa_prompt_kernel(q_ref, kcmp_ref, vcmp_ref, kv_ref, win_ref, so_ref, e_ref, pair_ref, o_ref, *, t_len):
    tq = NSA_TQ
    nc = t_len // CMP_BLOCK
    ns = -(-t_len // SEL_BLOCK)
    i = pl.program_id(1)
    start = pl.multiple_of(i * tq, tq)
    qpos_col = start + lax.broadcasted_iota(jnp.int32, (tq, 1), 0)
    so = so_ref[...]
    q_all = jnp.concatenate([q_ref[:, h * HD:(h + 1) * HD] for h in range(N_HEADS)], axis=0)
    sc = _dot_nt(q_all, kcmp_ref[...].astype(BF16)) * ATT_SCALE
    qpos4 = start + (lax.broadcasted_iota(jnp.int32, (N_HEADS * tq, 1), 0) & (tq - 1))
    cend = lax.broadcasted_iota(jnp.int32, (N_HEADS * tq, nc), 1) * CMP_BLOCK + (CMP_BLOCK - 1)
    (ec,), invc = _softmax_parts([sc], [cend <= qpos4])
    pc = ec * invc
    o_cmp = _dot(pc.astype(BF16), vcmp_ref[...].astype(BF16))
    imp_c = pc[0:tq]
    for h in range(1, N_HEADS):
        imp_c = imp_c + pc[h * tq:(h + 1) * tq]
    imp = _dot_exact(imp_c, pair_ref[...])
    sel_b = jnp.where(_select_blocks(imp, qpos_col, ns), 1.0, 0.0).astype(BF16)
    band = WINDOW + tq
    kwin = win_ref[pl.ds(start, band), 0:HD]
    vwin = win_ref[pl.ds(start, band), HD:2 * HD]
    wpos = start - WINDOW + lax.broadcasted_iota(jnp.int32, (tq, band), 1)
    wmask = (wpos <= qpos_col) & (wpos > qpos_col - WINDOW) & (wpos >= 0)
    for h in range(N_HEADS):
        sl = slice(h * HD, (h + 1) * HD)
        (ew,), invw = _softmax_parts([_dot_nt(q_ref[:, sl], kwin) * ATT_SCALE], [wmask])
        g0 = so[:, SM_DG + 3 * h:SM_DG + 3 * h + 1]
        g2 = so[:, SM_DG + 3 * h + 2:SM_DG + 3 * h + 3]
        o_ref[:, sl] = g0 * o_cmp[h * tq:(h + 1) * tq] + g2 * (_dot(ew.astype(BF16), vwin) * invw)

    n_cls = -(-t_len // KEY_CLASS)
    cls = (start + tq - 1) // KEY_CLASS

    def run_selected(kl):
        sel_keys = _dot(sel_b, e_ref[:, 0:kl]) > 0.5
        kpos = lax.broadcasted_iota(jnp.int32, (tq, kl), 1)
        smask = sel_keys & (kpos <= qpos_col)
        ks = kv_ref[0:kl, 2 * HD:3 * HD]
        vs = kv_ref[0:kl, 3 * HD:4 * HD]
        for h in range(N_HEADS):
            sl = slice(h * HD, (h + 1) * HD)
            (es,), invs = _softmax_parts([_dot_nt(q_ref[:, sl], ks) * ATT_SCALE], [smask])
            g1 = so[:, SM_DG + 3 * h + 1:SM_DG + 3 * h + 2]
            o_ref[:, sl] = o_ref[:, sl] + g1 * (_dot(es.astype(BF16), vs) * invs)

    for c in range(n_cls):
        pl.when(cls == c)(functools.partial(run_selected, min((c + 1) * KEY_CLASS, t_len)))


def _nsa_prompt(nq, kcmp, vcmp, nkv, win_pad, so):
    b, t, _ = nq.shape
    tq = NSA_TQ
    nc = t // CMP_BLOCK
    e_mat = _expand_matrix(t)
    pair = _pair_matrix(nc)
    return pl.pallas_call(
        functools.partial(_nsa_prompt_kernel, t_len=t),
        grid=(b, t // tq),
        in_specs=[pl.BlockSpec((None, tq, MIX_W), lambda bi, i: (bi, i, 0)),
                  pl.BlockSpec((None, nc, HD), lambda bi, i: (bi, 0, 0)),
                  pl.BlockSpec((None, nc, HD), lambda bi, i: (bi, 0, 0)),
                  pl.BlockSpec((None, t, 768), lambda bi, i: (bi, 0, 0)),
                  pl.BlockSpec((None, t + WINDOW, 256), lambda bi, i: (bi, 0, 0)),
                  pl.BlockSpec((None, tq, 128), lambda bi, i: (bi, i, 0)),
                  pl.BlockSpec((128, t), lambda bi, i: (0, 0)),
                  pl.BlockSpec((nc, 128), lambda bi, i: (0, 0))],
        out_specs=pl.BlockSpec((None, tq, MIX_W), lambda bi, i: (bi, i, 0)),
        out_shape=jax.ShapeDtypeStruct((b, t, MIX_W), F32),
        compiler_params=_cparams(("parallel", "arbitrary"), 48),
        name="nsa_prompt",
    )(nq, kcmp, vcmp, nkv, win_pad, so, e_mat, pair)


N_NSA_S_IN = 11


def _nsa_sample_kernel(pt_ref, q_ref, kvn_ref, winn_ref, so_ref, win_ref, wk_ref, wv_ref, kcg_ref,
                       e_ref, pair_ref, *rest, n_pages, t_new):
    del pt_ref
    pages = rest[:n_pages]
    o_ref, wout_ref = rest[n_pages], rest[n_pages + 1]
    cmp_k, cmp_v = rest[n_pages + 2], rest[n_pages + 3]
    past = n_pages * PAGE_SIZE
    nc = past // CMP_BLOCK
    ns = -(-(past + t_new) // SEL_BLOCK)
    per_page = PAGE_SIZE // CMP_BLOCK
    n_grp = n_pages // PAGE_GROUP
    gw = PAGE_GROUP * PAGE_SIZE
    rows = N_HEADS * t_new
    so = so_ref[...]

    def page_rows(p, j):
        return pages[p][pl.ds(j, PAGE_SIZE, stride=4), :]

    def group_rows(g, j):
        return jnp.concatenate([page_rows(g * PAGE_GROUP + u, j).astype(BF16) for u in range(PAGE_GROUP)], axis=0)

    for p in range(n_pages):
        kc = page_rows(p, 0).reshape(per_page, CMP_BLOCK, HD)
        vc = page_rows(p, 1).reshape(per_page, CMP_BLOCK, HD)
        cmp_k[p * per_page:(p + 1) * per_page, :] = jnp.sum(kc * wk_ref[...][None], axis=1)
        cmp_v[p * per_page:(p + 1) * per_page, :] = jnp.sum(vc * wv_ref[...][None], axis=1)
    kcmp = _rms_lanes(cmp_k[...], kcg_ref[...]).astype(BF16)
    vcmp = cmp_v[...].astype(BF16)

    q_all = jnp.concatenate([q_ref[:, h * HD:(h + 1) * HD].astype(F32) for h in range(N_HEADS)], axis=0)
    q_b = q_all.astype(BF16)
    trow = lax.broadcasted_iota(jnp.int32, (rows, 1), 0) & (t_new - 1)
    qpos4 = past + trow
    qpos_col = past + lax.broadcasted_iota(jnp.int32, (t_new, 1), 0)
    sc = _dot_nt(q_b, kcmp) * ATT_SCALE
    cend = lax.broadcasted_iota(jnp.int32, (rows, nc), 1) * CMP_BLOCK + (CMP_BLOCK - 1)
    (ec,), invc = _softmax_parts([sc], [cend <= qpos4])
    pc = ec * invc
    o_cmp = _dot(pc.astype(BF16), vcmp)
    imp_c = pc[0:t_new]
    for h in range(1, N_HEADS):
        imp_c = imp_c + pc[h * t_new:(h + 1) * t_new]
    imp = _dot_exact(imp_c, pair_ref[...])
    sel = _select_blocks(imp, qpos_col, ns)
    self32 = jnp.where(sel, 1.0, 0.0)
    sel_past = _dot(self32.astype(BF16), e_ref[...])
    sel_past4 = jnp.concatenate([sel_past] * N_HEADS, axis=0) > 0.5
    new_blk = past // SEL_BLOCK
    sel_new = jnp.concatenate([self32[:, new_blk:new_blk + 1]] * N_HEADS, axis=0) > 0.5
    kvn = kvn_ref[...].astype(F32)
    s_list, m_list = [], []
    for g in range(n_grp):
        s_list.append(_dot_nt(q_b, group_rows(g, 2)) * ATT_SCALE)
        kpos = g * gw + lax.broadcasted_iota(jnp.int32, (rows, gw), 1)
        m_list.append(sel_past4[:, g * gw:(g + 1) * gw] & (kpos <= qpos4))
    npos = past + lax.broadcasted_iota(jnp.int32, (rows, t_new), 1)
    causal_new = npos <= qpos4
    s_list.append(_dot_nt(q_all, kvn[:, 2 * HD:3 * HD]) * ATT_SCALE)
    m_list.append(sel_new & causal_new)
    es, invs = _softmax_parts(s_list, m_list)
    o_sel = _dot(es[n_grp], kvn[:, 3 * HD:4 * HD])
    for g in range(n_grp):
        o_sel = o_sel + _dot(es[g].astype(BF16), group_rows(g, 3))
    o_sel = o_sel * invs
    w_buf = win_ref.shape[0] // 2
    kwb = win_ref[pl.ds(0, w_buf, stride=2), :].astype(BF16)
    vwb = win_ref[pl.ds(1, w_buf, stride=2), :].astype(BF16)
    wpos = past - w_buf + lax.broadcasted_iota(jnp.int32, (rows, w_buf), 1)
    wmask = (wpos <= qpos4) & (wpos > qpos4 - WINDOW) & (wpos >= 0)
    sw_list = [_dot_nt(q_b, kwb) * ATT_SCALE, _dot_nt(q_all, kvn[:, 4 * HD:5 * HD]) * ATT_SCALE]
    ew, invw = _softmax_parts(sw_list, [wmask, causal_new & (npos > qpos4 - WINDOW)])
    o_win = (_dot(ew[0].astype(BF16), vwb) + _dot(ew[1], kvn[:, 5 * HD:6 * HD])) * invw
    for h in range(N_HEADS):
        r = slice(h * t_new, (h + 1) * t_new)
        g0 = so[:, SM_DG + 3 * h:SM_DG + 3 * h + 1]
        g1 = so[:, SM_DG + 3 * h + 1:SM_DG + 3 * h + 2]
        g2 = so[:, SM_DG + 3 * h + 2:SM_DG + 3 * h + 3]
        o_ref[:, h * HD:(h + 1) * HD] = g0 * o_cmp[r] + g1 * o_sel[r] + g2 * o_win[r]
    keep = 2 * (w_buf - t_new)
    wout_ref[0:keep, :] = win_ref[2 * t_new:2 * w_buf, :]
    wout_ref[keep:2 * w_buf, :] = winn_ref[...]


def _nsa_sample(layer, depth, prev, page_table, nq, nkv, win_new, so, cache4, win_state4, wk_b, wv_b, kcg):
    b, t, _ = nq.shape
    n_pages = page_table.shape[1]
    past = n_pages * PAGE_SIZE
    nc = past // CMP_BLOCK
    w2 = win_state4.shape[2]
    e_mat = _expand_matrix(past)
    pair = _pair_matrix(nc)

    def tok(rows, width):
        return pl.BlockSpec((None, rows, width), lambda bi, pt: (bi, 0, 0))

    def const(shape):
        return pl.BlockSpec(shape, lambda bi, pt: (0, 0))

    in_specs = [tok(t, MIX_W), tok(t, 768), tok(2 * t, HD), tok(t, 128),
                pl.BlockSpec((None, None, w2, 128), lambda bi, pt: (layer, bi, 0, 0)),
                const((CMP_BLOCK, 128)), const((CMP_BLOCK, 128)), const((1, 128)),
                const((128, past)), const((nc, 128))]
    in_specs += [pl.BlockSpec((None, None, PAGE_SIZE * 4, 128), lambda bi, pt, p=p: (layer, pt[bi, p], 0, 0))
                 for p in range(n_pages)]
    n_in = N_NSA_S_IN + n_pages
    c_specs, c_args, aliases = _carry_args(prev, n_in, (1,))
    grid_spec = pltpu.PrefetchScalarGridSpec(
        num_scalar_prefetch=1, grid=(b,), in_specs=in_specs + c_specs,
        out_specs=(pl.BlockSpec((None, t, MIX_W), lambda bi, pt: (bi, 0, 0)),
                   pl.BlockSpec((None, None, w2, 128), lambda bi, pt: (layer, bi, 0, 0))),
        scratch_shapes=[pltpu.VMEM((nc, HD), F32), pltpu.VMEM((nc, HD), F32)])
    return pl.pallas_call(
        _with_carried_outputs(functools.partial(_nsa_sample_kernel, n_pages=n_pages, t_new=t), n_in, len(c_args)),
        grid_spec=grid_spec,
        out_shape=(jax.ShapeDtypeStruct((b, t, MIX_W), F32),
                   jax.ShapeDtypeStruct((depth, b, w2, 128), F32)),
        input_output_aliases=aliases,
        compiler_params=_cparams(("arbitrary",), 48),
        name="nsa_sample",
    )(page_table, nq, nkv, win_new, so, win_state4, wk_b, wv_b, kcg, e_mat, pair,
      *([cache4] * n_pages), *c_args)


OUT_TM = 256


def _out_kernel(x_ref, oa_ref, ob_ref, oc_ref, od_ref, z_ref, g0_ref, g1_ref, g2_ref, g3_ref,
                wb_ref, wo_ref, y_ref):
    acc = None
    for g, (o_r, g_r) in enumerate(((oa_ref, g0_ref), (ob_ref, g1_ref), (oc_ref, g2_ref), (od_ref, g3_ref))):
        z = z_ref[:, g * MIX_W:(g + 1) * MIX_W]
        br = (o_r[...] * (z * _sigmoid(z))).astype(BF16)
        term = _sigmoid(g_r[...]) * _dot(br, wb_ref[g])
        acc = term if acc is None else acc + term
    y_ref[...] = x_ref[...] + _dot(acc.astype(BF16), wo_ref[...])


def _out_proj(x2d, oa, ob, oc, od, h2d, wb_all, wo_all, layer):
    m = x2d.shape[0]
    tm = OUT_TM

    def rows(width):
        return pl.BlockSpec((tm, width), lambda i: (i, 0))

    def gate(g):
        return pl.BlockSpec((tm, D_MODEL), lambda i, g=g: (i, COL_GATE // D_MODEL + g))

    single = pl.Buffered(1)
    return pl.pallas_call(
        _out_kernel,
        grid=(m // tm,),
        in_specs=[rows(D_MODEL), rows(MIX_W), rows(MIX_W), rows(MIX_W), rows(MIX_W),
                  rows(D_MODEL), gate(0), gate(1), gate(2), gate(3),
                  pl.BlockSpec((None, N_HEADS, MIX_W, D_MODEL), lambda i: (layer, 0, 0, 0), pipeline_mode=single),
                  pl.BlockSpec((None, D_MODEL, D_MODEL), lambda i: (layer, 0, 0), pipeline_mode=single)],
        out_specs=rows(D_MODEL),
        out_shape=jax.ShapeDtypeStruct((m, D_MODEL), F32),
        compiler_params=_cparams(("parallel",), 56),
        name="out_proj",
    )(x2d, oa, ob, oc, od, h2d, h2d, h2d, h2d, h2d, wb_all, wo_all)


def _layer_params(l, p):
    zeros = jnp.zeros((2, HD), F32)
    gains = jnp.concatenate([p["fox_qg"][l][None], p["fox_kg"][l][None], p["nsa_qg"][l][None],
                             p["nsa_ksg"][l][None], p["nsa_kwg"][l][None], p["nsa_kcg"][l][None], zeros], axis=0)
    bias = jnp.concatenate([p["fox_bf"][l], p["mlstm_bi"][l], p["mlstm_bf"][l], jnp.zeros((GLA_RANK,), F32),
                            p["nsa_bg"][l].reshape(-1), jnp.zeros((128 - SM_DG - 12,), F32)]).reshape(1, 128)
    w2pad = jnp.zeros((128, 256), F32).at[SM_CA:SM_CA + GLA_RANK].set(p["gla_w2"][l])
    return dict(
        norm_g=p["norm_g"][l], gains=gains, bias=bias,
        wk_b=jnp.broadcast_to(p["nsa_wk"][l][:, None], (CMP_BLOCK, 128)),
        wv_b=jnp.broadcast_to(p["nsa_wv"][l][:, None], (CMP_BLOCK, 128)),
        kcg=p["nsa_kcg"][l].reshape(1, HD),
        mlstm_hg=p["mlstm_hg"][l], w2pad=w2pad, gla_b2=p["gla_b2"][l], gla_hg=p["gla_hg"][l])


def _layer(x3d, lp, shared, st, prev, *, is_prompt, layer, depth):
    b, t, _ = x3d.shape
    m = b * t
    x2d = x3d.reshape(m, D_MODEL)
    xn = _rmsnorm(x2d, lp["norm_g"])
    h2d = _in_proj(xn, shared["w_in"], layer)
    h3d = h2d.reshape(b, t, N_PROJ)
    (foxkv, nsakv, win, so, fq, fk, fv, nq, nkv, kcmp, vcmp) = _prep(
        h2d, lp["gains"], lp["bias"], lp["wk_b"], lp["wv_b"], layer, depth,
        None if prev is None else (prev["fox_kv"], prev["nsa_kv"]))
    so3 = so.reshape(b, t, 128)
    fq3, fk3, fv3 = fq.reshape(b, t, MIX_W), fk.reshape(b, t, MIX_W), fv.reshape(b, t, MIX_W)
    nq3 = nq.reshape(b, t, MIX_W)
    nkv3 = nkv.reshape(b, t, 768)
    chunk = min(64, t)
    carried = dict(fox_kv=foxkv, nsa_kv=nsakv)
    if is_prompt:
        lf_t = so3[:, :, SM_AF:SM_AF + N_HEADS].transpose(0, 2, 1).reshape(b * N_HEADS, t)
        f_row = _cumsum_rows(lf_t).reshape(b, N_HEADS, t)
        f_col = f_row.transpose(0, 2, 1)
        o_a = _fox_prompt(fq3, fk3, fv3, f_col, f_row)
        win_pad = jnp.pad(nkv3[:, :, 4 * HD:], ((0, 0), (WINDOW, 0), (0, 0)))
        o_d = _nsa_prompt(nq3, kcmp.reshape(b, t // CMP_BLOCK, HD), vcmp.reshape(b, t // CMP_BLOCK, HD),
                          nkv3, win_pad, so3)
        win_out = win.reshape(b, t, 2, HD)[:, -min(WINDOW, t):]
        bb = min(BATCH_UNROLL, b)
    else:
        o_a = _fox_sample(layer, shared["page_table"], fq3, fk3, fv3, so3, shared["fox_kv4"], shared["fox_lf4"])
        o_d, win_out = _nsa_sample(layer, depth, None if prev is None else (prev["win"],),
                                   shared["page_table"], nq3, nkv3, win.reshape(b, 2 * t, HD), so3,
                                   shared["nsa_kv4"], shared["nsa_win4"], lp["wk_b"], lp["wv_b"], lp["kcg"])
        bb = 2 * BATCH_UNROLL
    o_b, c_new, n_new, m_new = _mlstm(
        h3d, so3, lp["mlstm_hg"], st["C"], st["n"], st["m"], st["layer"], layer, depth,
        None if prev is None else (prev["C"], prev["n"], prev["m"]), bb, chunk)
    o_c, s_new = _gla(h3d, so3, lp["w2pad"], lp["gla_b2"], lp["gla_hg"], st["S"], st["layer"], layer, depth,
                      None if prev is None else (prev["S"],), bb, chunk)
    y2d = _out_proj(x2d, o_a.reshape(m, MIX_W), o_b.reshape(m, MIX_W), o_c.reshape(m, MIX_W),
                    o_d.reshape(m, MIX_W), h2d, shared["w_branch"], shared["w_out"], layer)
    carried.update(win=win_out, C=c_new, n=n_new, m=m_new, S=s_new)
    return y2d.reshape(b, t, D_MODEL), carried, so3[:, :, SM_AF:SM_AF + N_HEADS]


def kernel(x_prompt, x_sample, cache_fox_kv, cache_fox_logf, cache_nsa_kv, page_table, state_nsa_win, state_mlstm_C, state_mlstm_n, state_mlstm_m, state_gla_S, norm_g, w_in, fox_qg, fox_kg, fox_bf, mlstm_bi, mlstm_bf, mlstm_hg, gla_w2, gla_b2, gla_hg, nsa_qg, nsa_kcg, nsa_ksg, nsa_kwg, nsa_wk, nsa_wv, nsa_bg, w_branch, w_out):
    params = dict(norm_g=norm_g, fox_qg=fox_qg, fox_kg=fox_kg, fox_bf=fox_bf, mlstm_bi=mlstm_bi,
                  mlstm_bf=mlstm_bf, mlstm_hg=mlstm_hg, gla_w2=gla_w2, gla_b2=gla_b2, gla_hg=gla_hg,
                  nsa_qg=nsa_qg, nsa_kcg=nsa_kcg, nsa_ksg=nsa_ksg, nsa_kwg=nsa_kwg, nsa_wk=nsa_wk,
                  nsa_wv=nsa_wv, nsa_bg=nsa_bg)
    depth = w_in.shape[0]
    bp, tp, _ = x_prompt.shape
    db, ts, _ = x_sample.shape
    n_pool = cache_fox_kv.shape[1]
    w_buf = state_nsa_win.shape[2]
    shared = dict(
        w_in=_permute_w_in(w_in), w_branch=w_branch.astype(BF16), w_out=w_out.astype(BF16),
        page_table=page_table,
        fox_kv4=cache_fox_kv.reshape(depth, n_pool, PAGE_SIZE * 2 * N_HEADS, HD),
        fox_lf4=cache_fox_logf.transpose(0, 1, 3, 2),
        nsa_kv4=cache_nsa_kv.reshape(depth, n_pool, PAGE_SIZE * 4, HD),
        nsa_win4=state_nsa_win.reshape(depth, db, w_buf * 2, HD))
    st_p = dict(C=jnp.zeros((1, bp, N_HEADS, HD, HD), F32), n=jnp.zeros((1, bp, N_HEADS, HD), F32),
                m=jnp.zeros((1, bp, N_HEADS, HD), F32), S=jnp.zeros((1, bp, N_HEADS, GLA_DK, GLA_DV), F32),
                layer=0)
    m_s = jnp.broadcast_to(state_mlstm_m[..., None], state_mlstm_m.shape + (HD,))
    y_p, y_s = x_prompt, x_sample
    prev_p = prev_s = None
    lf_p, lf_s, win_p = [], [], []
    for l in range(depth):
        lp = _layer_params(l, params)
        st_s = dict(C=state_mlstm_C, n=state_mlstm_n, m=m_s, S=state_gla_S, layer=l)
        y_p, prev_p, lf = _layer(y_p, lp, shared, st_p, prev_p, is_prompt=True, layer=l, depth=depth)
        lf_p.append(lf)
        win_p.append(prev_p["win"])
        y_s, prev_s, lf = _layer(y_s, lp, shared, st_s, prev_s, is_prompt=False, layer=l, depth=depth)
        lf_s.append(lf)

    def finish(c, b, t, win, lf):
        return (c["fox_kv"].reshape(depth, b, t, 2, N_HEADS, HD), jnp.stack(lf, axis=0),
                c["nsa_kv"].reshape(depth, b, t, 4, HD), win,
                c["C"], c["n"], c["m"][..., 0], c["S"])

    out_p = finish(prev_p, bp, tp, jnp.stack(win_p, axis=0), lf_p)
    out_s = finish(prev_s, db, ts, prev_s["win"].reshape(depth, db, w_buf, 2, HD), lf_s)
    return (y_p, y_s) + out_p + out_s
```

```python
import functools

import jax
import jax.numpy as jnp
from jax import lax
from jax.experimental import pallas as pl
from jax.experimental.pallas import tpu as pltpu

F32 = jnp.float32
BF16 = jnp.bfloat16
HIGHEST = lax.Precision.HIGHEST

D_MODEL = 2048
MIX_W = 512
HD = 128
N_HEADS = 4
GLA_DK = 64
GLA_DK_SHIFT = 6
GLA_DV = 128
GLA_RANK = 16
GLA_TAU = 16.0
CMP_BLOCK = 32
SEL_BLOCK = 64
SEL_SHIFT = 6
N_SEL = 16
WINDOW = 512
PAGE_SIZE = 128
EPS = 1e-6
ATT_SCALE = HD ** -0.5
NEG_INF = float("-inf")
KEY_CLASS = 512
BF16_ROWS = 16

_REF_LAYOUT = (
    ("a_q", 512), ("a_k", 512), ("a_v", 512), ("a_f", 4), ("a_z", 512),
    ("b_q", 512), ("b_k", 512), ("b_v", 512), ("b_i", 4), ("b_f", 4), ("b_o", 512), ("b_z", 512),
    ("c_q", 256), ("c_k", 256), ("c_v", 512), ("c_a", 16), ("c_z", 512),
    ("d_q", 512), ("d_kv", 768), ("d_g", 12), ("d_z", 512),
    ("gate", 8192),
)
N_REF = sum(w for _, w in _REF_LAYOUT)
_MY_LAYOUT = (
    "a_z", "b_z", "c_z", "d_z", "gate",
    "a_q", "a_k", "a_v", "d_q", "b_q", "b_k", "b_v", "b_o", "c_v", "c_q", "c_k", "d_kv",
    "a_f", "b_i", "b_f", "c_a", "d_g",
)
N_PROJ = 16384
COL_Z = 0
COL_GATE = 2048
COL_AQ, COL_AK, COL_AV, COL_DQ = 10240, 10752, 11264, 11776
COL_BQ, COL_BK, COL_BV, COL_BO = 12288, 12800, 13312, 13824
COL_CV, COL_CQ, COL_CK, COL_DKV, COL_SMALL = 14336, 14848, 15104, 15360, 16128
SM_AF, SM_BI, SM_BF, SM_CA, SM_DG = 0, 4, 8, 12, 28


def _cparams(sem, vmem_mb):
    return pltpu.CompilerParams(dimension_semantics=sem, vmem_limit_bytes=vmem_mb * 1024 * 1024)


def _log_sigmoid(x):
    return jnp.minimum(x, 0.0) - jnp.log(1.0 + jnp.exp(-jnp.abs(x)))


def _sigmoid(x):
    return 0.5 * jnp.tanh(0.5 * x) + 0.5


def _rms_lanes(x, g):
    return (x * lax.rsqrt(jnp.mean(x * x, axis=-1, keepdims=True) + EPS)) * g


def _dot(a, b):
    return jnp.dot(a, b, preferred_element_type=F32)


def _dot_nt(a, b):
    return lax.dot_general(a, b, (((1,), (1,)), ((), ())), preferred_element_type=F32)


def _dot_tn(a, b):
    return lax.dot_general(a, b, (((0,), (0,)), ((), ())), preferred_element_type=F32)


def _dot_exact(a, b):
    return jnp.dot(a, b, precision=HIGHEST, preferred_element_type=F32)


def _eye_mask(n):
    return lax.broadcasted_iota(jnp.int32, (n, n), 0) == lax.broadcasted_iota(jnp.int32, (n, n), 1)


def _col_to_row(c):
    n = c.shape[0]
    return jnp.sum(jnp.where(_eye_mask(n), c, 0.0), axis=0, keepdims=True)


def _row_to_col(r):
    n = r.shape[1]
    return jnp.sum(jnp.where(_eye_mask(n), r, 0.0), axis=1, keepdims=True)


def _tril_ones(n):
    r = lax.broadcasted_iota(jnp.int32, (n, n), 0)
    c = lax.broadcasted_iota(jnp.int32, (n, n), 1)
    return jnp.where(c <= r, 1.0, 0.0).astype(F32)


def _cumsum_lanes(x):
    n = x.shape[-1]
    lane = lax.broadcasted_iota(jnp.int32, x.shape, x.ndim - 1)
    s = 1
    while s < n:
        x = x + jnp.where(lane >= s, pltpu.roll(x, s, axis=x.ndim - 1), 0.0)
        s *= 2
    return x


def _pad_rows(x, rows):
    if x.shape[0] == rows:
        return x
    return jnp.concatenate([x, jnp.zeros((rows - x.shape[0], x.shape[1]), x.dtype)], axis=0)


def _softmax_parts(s_list, mask_list):
    masked = [jnp.where(mk, s, NEG_INF) for s, mk in zip(s_list, mask_list)]
    m = masked[0].max(axis=-1, keepdims=True)
    for s in masked[1:]:
        m = jnp.maximum(m, s.max(axis=-1, keepdims=True))
    m = jnp.where(m > NEG_INF, m, 0.0)
    es = [jnp.exp(s - m) for s in masked]
    tot = es[0].sum(axis=-1, keepdims=True)
    for e in es[1:]:
        tot = tot + e.sum(axis=-1, keepdims=True)
    return es, 1.0 / jnp.maximum(tot, 1e-30)


def _with_carried_outputs(kernel_fn, n_in, n_carried):
    if not n_carried:
        return kernel_fn

    def wrapped(*refs):
        return kernel_fn(*refs[:n_in], *refs[n_in + n_carried:])

    return wrapped


def _carry_args(prev, first_in_index, out_indices):
    if prev is None:
        return [], [], {}
    specs = [pl.BlockSpec(memory_space=pl.ANY) for _ in prev]
    aliases = {first_in_index + k: oi for k, oi in enumerate(out_indices)}
    return specs, list(prev), aliases


def _segments():
    offs = {}
    off = 0
    for name, width in _REF_LAYOUT:
        offs[name] = (off, width)
        off += width
    segs = []
    dst = 0
    for name in _MY_LAYOUT:
        src, width = offs[name]
        segs.append((src, dst, width))
        dst += width
    return segs, dst


def _permute_kernel(wt_ref, o_ref, small_ref):
    segs, _ = _segments()
    for src, dst, width in segs:
        if width >= 128:
            w = wt_ref[src:src + width, :]
            if dst == COL_GATE:
                w = w * 0.5
            o_ref[dst:dst + width, :] = w.astype(BF16)
    small_ref[...] = jnp.zeros(small_ref.shape, F32)
    for src, dst, width in segs:
        if width < 128:
            small_ref[dst - COL_SMALL:dst - COL_SMALL + width, :] = wt_ref[src:src + width, :]
    o_ref[COL_SMALL:N_PROJ, :] = small_ref[...].astype(BF16)


def _permute_w_in(w_in):
    depth, d, n = w_in.shape
    tk = 128
    wt = jnp.transpose(w_in, (0, 2, 1))
    return pl.pallas_call(
        _permute_kernel,
        grid=(depth, d // tk),
        in_specs=[pl.BlockSpec((None, n, tk), lambda l, i: (l, 0, i))],
        out_specs=pl.BlockSpec((None, N_PROJ, tk), lambda l, i: (l, 0, i)),
        out_shape=jax.ShapeDtypeStruct((depth, N_PROJ, d), BF16),
        scratch_shapes=[pltpu.VMEM((N_PROJ - COL_SMALL, tk), F32)],
        compiler_params=_cparams(("parallel", "parallel"), 40),
        name="permute_w_in",
    )(wt)


def _norm_kernel(x_ref, g_ref, o_ref):
    o_ref[...] = _rms_lanes(x_ref[...], g_ref[...]).astype(o_ref.dtype)


def _rmsnorm(x2d, g):
    m, d = x2d.shape
    tm = min(512, m)
    return pl.pallas_call(
        _norm_kernel,
        grid=(m // tm,),
        in_specs=[pl.BlockSpec((tm, d), lambda i: (i, 0)), pl.BlockSpec((1, d), lambda i: (0, 0))],
        out_specs=pl.BlockSpec((tm, d), lambda i: (i, 0)),
        out_shape=jax.ShapeDtypeStruct((m, d), BF16),
        compiler_params=_cparams(("parallel",), 32),
        name="rmsnorm",
    )(x2d, g.reshape(1, d))


def _mm_kernel(a_ref, bt_ref, o_ref):
    o_ref[...] = _dot_nt(a_ref[...], bt_ref[...])


def _in_proj(a, wt_all, layer):
    m, k = a.shape
    n = wt_all.shape[1]
    tm, tn = min(1024, m), 1024
    return pl.pallas_call(
        _mm_kernel,
        grid=(m // tm, n // tn),
        in_specs=[pl.BlockSpec((tm, k), lambda i, j: (i, 0)),
                  pl.BlockSpec((None, tn, k), lambda i, j: (layer, j, 0))],
        out_specs=pl.BlockSpec((tm, tn), lambda i, j: (i, j)),
        out_shape=jax.ShapeDtypeStruct((m, n), F32),
        compiler_params=_cparams(("parallel", "arbitrary"), 48),
        name="in_proj",
    )(a, wt_all)


PREP_TM = 256
N_PREP_IN = 10


def _prep_kernel(aq_ref, ak_ref, av_ref, dq_ref, dkv_ref, sm_ref, gains_ref, bias_ref, wk_ref, wv_ref,
                 foxkv_ref, nsakv_ref, win_ref, so_ref, fq_ref, fk_ref, fv_ref, nq_ref, nkv_ref,
                 kcmp_ref, vcmp_ref):
    tm = PREP_TM
    g_fq, g_fk, g_nq = gains_ref[0:1, :], gains_ref[1:2, :], gains_ref[2:3, :]
    g_ks, g_kw, g_kc = gains_ref[3:4, :], gains_ref[4:5, :], gains_ref[5:6, :]
    for h in range(N_HEADS):
        sl = slice(h * HD, (h + 1) * HD)
        fq_ref[:, sl] = _rms_lanes(aq_ref[:, sl], g_fq).astype(BF16)
        kn = _rms_lanes(ak_ref[:, sl], g_fk)
        foxkv_ref[pl.ds(h, tm, stride=2 * N_HEADS), :] = kn
        fk_ref[:, sl] = kn.astype(BF16)
        v = av_ref[:, sl]
        foxkv_ref[pl.ds(N_HEADS + h, tm, stride=2 * N_HEADS), :] = v
        fv_ref[:, sl] = v.astype(BF16)
        nq_ref[:, sl] = _rms_lanes(dq_ref[:, sl], g_nq).astype(BF16)
    kc = dkv_ref[:, 0:128]
    vc = dkv_ref[:, 128:256]
    ks = _rms_lanes(dkv_ref[:, 256:384], g_ks)
    vs = dkv_ref[:, 384:512]
    kw = _rms_lanes(dkv_ref[:, 512:640], g_kw)
    vw = dkv_ref[:, 640:768]
    for j, a in enumerate((kc, vc, ks, vs)):
        nsakv_ref[pl.ds(j, tm, stride=4), :] = a
    win_ref[pl.ds(0, tm, stride=2), :] = kw
    win_ref[pl.ds(1, tm, stride=2), :] = vw
    for j, a in enumerate((kc, vc, ks, vs, kw, vw)):
        nkv_ref[:, j * HD:(j + 1) * HD] = a.astype(BF16)
    nb = tm // CMP_BLOCK
    kcs = jnp.sum(kc.reshape(nb, CMP_BLOCK, HD) * wk_ref[...][None], axis=1)
    kcmp_ref[...] = _rms_lanes(kcs, g_kc)
    vcmp_ref[...] = jnp.sum(vc.reshape(nb, CMP_BLOCK, HD) * wv_ref[...][None], axis=1)
    y = sm_ref[...] + bias_ref[...]
    lane = lax.broadcasted_iota(jnp.int32, y.shape, 1)
    is_ls = (lane < SM_BI) | ((lane >= SM_BF) & (lane < SM_CA))
    is_sg = (lane >= SM_DG) & (lane < SM_DG + 12)
    so_ref[...] = jnp.where(is_ls, _log_sigmoid(y), jnp.where(is_sg, _sigmoid(y), y))


def _prep(h2d, gains, bias, wk_b, wv_b, layer, depth, prev):
    m = h2d.shape[0]
    tm = PREP_TM

    def col(width, off):
        blk = off // width
        return pl.BlockSpec((tm, width), lambda i, blk=blk: (i, blk))

    def full(shape):
        return pl.BlockSpec(shape, lambda i: (0, 0))

    def rows(width):
        return pl.BlockSpec((tm, width), lambda i: (i, 0))

    def stacked(slots):
        return pl.BlockSpec((None, tm * slots, HD), lambda i: (layer, i, 0))

    out_shapes = (
        jax.ShapeDtypeStruct((depth, m * 8, HD), F32),
        jax.ShapeDtypeStruct((depth, m * 4, HD), F32),
        jax.ShapeDtypeStruct((m * 2, HD), F32),
        jax.ShapeDtypeStruct((m, 128), F32),
        jax.ShapeDtypeStruct((m, 512), BF16),
        jax.ShapeDtypeStruct((m, 512), BF16),
        jax.ShapeDtypeStruct((m, 512), BF16),
        jax.ShapeDtypeStruct((m, 512), BF16),
        jax.ShapeDtypeStruct((m, 768), BF16),
        jax.ShapeDtypeStruct((m // CMP_BLOCK, 128), F32),
        jax.ShapeDtypeStruct((m // CMP_BLOCK, 128), F32),
    )
    out_specs = (stacked(8), stacked(4), pl.BlockSpec((tm * 2, HD), lambda i: (i, 0)),
                 rows(128), rows(512), rows(512), rows(512), rows(512), rows(768),
                 pl.BlockSpec((tm // CMP_BLOCK, 128), lambda i: (i, 0)),
                 pl.BlockSpec((tm // CMP_BLOCK, 128), lambda i: (i, 0)))
    c_specs, c_args, aliases = _carry_args(prev, N_PREP_IN, (0, 1))
    return pl.pallas_call(
        _with_carried_outputs(_prep_kernel, N_PREP_IN, len(c_args)),
        grid=(m // tm,),
        in_specs=[col(512, COL_AQ), col(512, COL_AK), col(512, COL_AV), col(512, COL_DQ),
                  col(768, COL_DKV), col(128, COL_SMALL),
                  full((8, 128)), full((1, 128)), full((CMP_BLOCK, 128)), full((CMP_BLOCK, 128))] + c_specs,
        out_specs=out_specs,
        out_shape=out_shapes,
        input_output_aliases=aliases,
        compiler_params=_cparams(("parallel",), 32),
        name="prep",
    )(h2d, h2d, h2d, h2d, h2d, h2d, gains, bias, wk_b, wv_b, *c_args)


def _cumsum_rows_kernel(x_ref, o_ref):
    o_ref[...] = _cumsum_lanes(x_ref[...])


def _cumsum_rows(x):
    r, n = x.shape
    return pl.pallas_call(
        _cumsum_rows_kernel,
        grid=(1,),
        in_specs=[pl.BlockSpec((r, n), lambda i: (0, 0))],
        out_specs=pl.BlockSpec((r, n), lambda i: (0, 0)),
        out_shape=jax.ShapeDtypeStruct((r, n), F32),
        name="fox_cumsum",
    )(x)


FOX_TQ = 256


def _fox_prompt_kernel(q_ref, k_ref, v_ref, fc_ref, fr_ref, o_ref, *, t_len):
    tq = FOX_TQ
    i = pl.program_id(1)
    n_cls = -(-t_len // KEY_CLASS)
    cls = (i * tq + tq - 1) // KEY_CLASS

    def run(kl):
        qpos = i * tq + lax.broadcasted_iota(jnp.int32, (tq, kl), 0)
        kpos = lax.broadcasted_iota(jnp.int32, (tq, kl), 1)
        mask = kpos <= qpos
        for h in range(N_HEADS):
            sl = slice(h * HD, (h + 1) * HD)
            s = _dot_nt(q_ref[:, sl], k_ref[0:kl, sl]) * ATT_SCALE
            s = s + fc_ref[:, h:h + 1] - fr_ref[h:h + 1, 0:kl]
            (e,), inv = _softmax_parts([s], [mask])
            o_ref[:, sl] = _dot(e.astype(BF16), v_ref[0:kl, sl]) * inv

    for c in range(n_cls):
        pl.when(cls == c)(functools.partial(run, min((c + 1) * KEY_CLASS, t_len)))


def _fox_prompt(fq, fk, fv, f_col, f_row):
    b, t, _ = fq.shape
    tq = FOX_TQ
    return pl.pallas_call(
        functools.partial(_fox_prompt_kernel, t_len=t),
        grid=(b, t // tq),
        in_specs=[pl.BlockSpec((None, tq, MIX_W), lambda bi, i: (bi, i, 0)),
                  pl.BlockSpec((None, t, MIX_W), lambda bi, i: (bi, 0, 0)),
                  pl.BlockSpec((None, t, MIX_W), lambda bi, i: (bi, 0, 0)),
                  pl.BlockSpec((None, tq, N_HEADS), lambda bi, i: (bi, i, 0)),
                  pl.BlockSpec((None, N_HEADS, t), lambda bi, i: (bi, 0, 0))],
        out_specs=pl.BlockSpec((None, tq, MIX_W), lambda bi, i: (bi, i, 0)),
        out_shape=jax.ShapeDtypeStruct((b, t, MIX_W), F32),
        compiler_params=_cparams(("parallel", "arbitrary"), 48),
        name="fox_prompt",
    )(fq, fk, fv, f_col, f_row)


PAGE_GROUP = 2


def _fox_sample_kernel(pt_ref, q_ref, kn_ref, vn_ref, so_ref, spread_ref, *rest, n_pages, t_new):
    del pt_ref
    kv_pages = rest[:n_pages]
    lf_pages = rest[n_pages:2 * n_pages]
    o_ref = rest[2 * n_pages]
    lf_scr = rest[2 * n_pages + 1]
    past = n_pages * PAGE_SIZE
    w2 = 2 * PAGE_SIZE
    n_rows = n_pages * N_HEADS
    for p in range(n_pages):
        lf_scr[p * N_HEADS:(p + 1) * N_HEADS, :] = lf_pages[p][...]
    lf_rows = lf_scr[...]
    local = _dot_exact(lf_rows, spread_ref[...])
    tot = _dot_exact(lf_rows, jnp.ones((PAGE_SIZE, 128), F32))
    rr = lax.broadcasted_iota(jnp.int32, (n_rows, n_rows), 0)
    cc = lax.broadcasted_iota(jnp.int32, (n_rows, n_rows), 1)
    same_head = (rr & (N_HEADS - 1)) == (cc & (N_HEADS - 1))
    earlier = jnp.where(same_head & (cc < rr), 1.0, 0.0)
    up_to = jnp.where(same_head & (cc <= rr), 1.0, 0.0)
    f_int = local + _dot_exact(earlier, tot)[:, 0:1]
    f_total = _dot_exact(up_to, tot)[n_rows - N_HEADS:n_rows, 0:1]
    f_new_all = _dot_exact(_tril_ones(t_new), so_ref[...])
    lane = lax.broadcasted_iota(jnp.int32, (t_new, w2), 1)
    is_key = (lane & 1) == 0
    kpos0 = lane >> 1
    tpos = lax.broadcasted_iota(jnp.int32, (t_new, w2), 0) + past
    rn = lax.broadcasted_iota(jnp.int32, (t_new, t_new), 0)
    cn = lax.broadcasted_iota(jnp.int32, (t_new, t_new), 1)

    def kv_rows(p, h):
        return kv_pages[p][pl.ds(h, w2, stride=N_HEADS), :].astype(BF16)

    heads = range(N_HEADS)
    sls = [slice(h * HD, (h + 1) * HD) for h in heads]
    qs = [q_ref[:, sl] for sl in sls]
    qk = [[_dot_nt(qs[h], kv_rows(p, h)) for p in range(n_pages)] for h in heads]
    qk_new = [_dot_nt(qs[h].astype(F32), kn_ref[:, sls[h]].astype(F32)) for h in heads]
    m_list = [is_key & (kpos0 + p * PAGE_SIZE <= tpos) for p in range(n_pages)] + [cn <= rn]
    weights = []
    for h in heads:
        fq = f_new_all[:, SM_AF + h:SM_AF + h + 1] + f_total[h:h + 1, :]
        s_list = [qk[h][p] * ATT_SCALE + fq - f_int[p * N_HEADS + h:p * N_HEADS + h + 1, :]
                  for p in range(n_pages)]
        s_list.append(qk_new[h] * ATT_SCALE + fq - _col_to_row(fq))
        weights.append(_softmax_parts(s_list, m_list))
    for h in heads:
        es, inv = weights[h]
        acc = _dot(es[n_pages], vn_ref[:, sls[h]].astype(F32))
        for p in range(n_pages):
            acc = acc + _dot(pltpu.roll(es[p], 1, axis=1).astype(BF16), kv_rows(p, h))
        o_ref[:, sls[h]] = acc * inv


def _spread_matrix():
    r = jnp.arange(PAGE_SIZE)[:, None]
    c = jnp.arange(2 * PAGE_SIZE)[None, :]
    return (2 * r <= c).astype(F32)


def _fox_sample(layer, page_table, fq, fk, fv, so, cache_kv4, cache_lf4):
    b, t, _ = fq.shape
    n_pages = page_table.shape[1]

    def page_spec(rows, p):
        return pl.BlockSpec((None, None, rows, 128), lambda bi, pt, p=p: (layer, pt[bi, p], 0, 0))

    tok = pl.BlockSpec((None, t, MIX_W), lambda bi, pt: (bi, 0, 0))
    in_specs = [tok, tok, tok, pl.BlockSpec((None, t, 128), lambda bi, pt: (bi, 0, 0)),
                pl.BlockSpec((PAGE_SIZE, 2 * PAGE_SIZE), lambda bi, pt: (0, 0))]
    in_specs += [page_spec(PAGE_SIZE * 2 * N_HEADS, p) for p in range(n_pages)]
    in_specs += [page_spec(N_HEADS, p) for p in range(n_pages)]
    grid_spec = pltpu.PrefetchScalarGridSpec(
        num_scalar_prefetch=1, grid=(b,), in_specs=in_specs,
        out_specs=pl.BlockSpec((None, t, MIX_W), lambda bi, pt: (bi, 0, 0)),
        scratch_shapes=[pltpu.VMEM((n_pages * N_HEADS, PAGE_SIZE), F32)])
    return pl.pallas_call(
        functools.partial(_fox_sample_kernel, n_pages=n_pages, t_new=t),
        grid_spec=grid_spec,
        out_shape=jax.ShapeDtypeStruct((b, t, MIX_W), F32),
        compiler_params=_cparams(("arbitrary",), 48),
        name="fox_sample",
    )(page_table, fq, fk, fv, so, _spread_matrix(), *([cache_kv4] * n_pages), *([cache_lf4] * n_pages))


N_MLSTM_IN = 9
BATCH_UNROLL = 4


def _mlstm_kernel(q_ref, k_ref, v_ref, og_ref, so_ref, hg_ref, c0_ref, n0_ref, m0_ref,
                  o_ref, c_ref, n_ref, m_ref, *, bb, chunk, wide_chains):
    L = chunk
    Lp = max(L, BF16_ROWS)

    @pl.when(pl.program_id(1) == 0)
    def _():
        c_ref[...] = c0_ref[...]
        n_ref[...] = n0_ref[...]
        m_ref[...] = m0_ref[...]

    tril = _tril_ones(Lp)
    causal = tril > 0.5
    hg = hg_ref[...]
    lane = lax.broadcasted_iota(jnp.int32, (Lp - L, 128), 1) if Lp > L else None

    def run_chains(idx):
        so_of, csum_of = [], []
        for bi in idx:
            so = so_ref[bi]
            if Lp > L:
                pad = jnp.where((lane >= SM_BI) & (lane < SM_BF), -1e30, 0.0)
                so = jnp.concatenate([so, pad], axis=0)
            so_of.append(so)
            csum_of.append(_dot_exact(tril, so))
        ch = [(bi, h) for bi in idx for h in range(N_HEADS)]
        seq = [u for u in range(len(idx)) for _ in range(N_HEADS)]
        n = range(len(ch))
        sls = [slice(h * HD, (h + 1) * HD) for _, h in ch]
        q = [_pad_rows(q_ref[bi, :, sls[c]], Lp) for c, (bi, _) in enumerate(ch)]
        k = [_pad_rows(k_ref[bi, :, sls[c]], Lp) * ATT_SCALE for c, (bi, _) in enumerate(ch)]
        vb = [_pad_rows(v_ref[bi, :, sls[c]], Lp).astype(BF16) for c, (bi, _) in enumerate(ch)]
        qb = [x.astype(BF16) for x in q]
        ig = [so_of[seq[c]][:, SM_BI + h:SM_BI + h + 1] for c, (_, h) in enumerate(ch)]
        bcol = [csum_of[seq[c]][:, SM_BF + h:SM_BF + h + 1] for c, (_, h) in enumerate(ch)]
        c_st = [c_ref[bi, h] for bi, h in ch]
        n_st = [n_ref[bi, h:h + 1, :] for bi, h in ch]
        m_st = [m_ref[bi, h:h + 1, 0:1] for bi, h in ch]
        qk_raw = [_dot_nt(qb[c], k[c].astype(BF16)) for c in n]
        q_c = [_dot(qb[c], c_st[c].astype(BF16)) for c in n]
        dmat = [jnp.where(causal, bcol[c] - _col_to_row(bcol[c]) + _col_to_row(ig[c]), NEG_INF) for c in n]
        inter = [bcol[c] + m_st[c] for c in n]
        m_t = [jnp.maximum(inter[c], dmat[c].max(axis=-1, keepdims=True)) for c in n]
        w_inter = [jnp.exp(inter[c] - m_t[c]) for c in n]
        qk = [qk_raw[c] * jnp.exp(dmat[c] - m_t[c]) for c in n]
        qk_v = [_dot(qk[c].astype(BF16), vb[c]) for c in n]
        b_last = [bcol[c][Lp - 1:Lp, :] for c in n]
        g_end = [b_last[c] - bcol[c] + ig[c] for c in n]
        m_new = [jnp.maximum(b_last[c] + m_st[c], g_end[c].max(axis=0, keepdims=True)) for c in n]
        a_prev = [jnp.exp(b_last[c] + m_st[c] - m_new[c]) for c in n]
        kw = [k[c] * jnp.exp(g_end[c] - m_new[c]) for c in n]
        k_v = [_dot_tn(kw[c].astype(BF16), vb[c]) for c in n]
        new_states = []
        for c, (bi, h) in enumerate(ch):
            num = w_inter[c] * q_c[c] + qk_v[c]
            den = (w_inter[c] * jnp.sum(q[c] * n_st[c], axis=-1, keepdims=True)
                   + jnp.sum(qk[c], axis=-1, keepdims=True))
            hout = num / jnp.maximum(jnp.abs(den), jnp.exp(-m_t[c]))
            new_states.append((a_prev[c] * c_st[c] + k_v[c],
                               a_prev[c] * n_st[c] + jnp.sum(kw[c], axis=0, keepdims=True),
                               jnp.broadcast_to(m_new[c], (1, HD))))
            hn = _rms_lanes(hout[0:L], hg)
            o_ref[bi, :, sls[c]] = hn * _sigmoid(og_ref[bi, :, sls[c]])
        for (bi, h), (c_new, n_new, m_nw) in zip(ch, new_states):
            c_ref[bi, h] = c_new
            n_ref[bi, h:h + 1, :] = n_new
            m_ref[bi, h:h + 1, :] = m_nw

    unroll = min(BATCH_UNROLL, bb)
    wide = min(wide_chains, unroll)

    def group(g, carry):
        for u0 in range(0, unroll, wide):
            run_chains([g * unroll + u0 + u for u in range(wide)])
        return carry

    if bb == unroll:
        group(0, 0)
    else:
        lax.fori_loop(0, bb // unroll, group, 0)


def _mlstm(h3d, so3d, hg, c0, n0, m0, state_layer, layer, depth, prev, bb, chunk):
    b, t, _ = h3d.shape
    nc = t // chunk
    wide_chains = 4 if chunk >= 64 else 2

    def col(off):
        blk = off // MIX_W
        return pl.BlockSpec((bb, chunk, MIX_W), lambda bi, c, blk=blk: (bi, c, blk))

    def st_c(li):
        return pl.BlockSpec((None, bb, N_HEADS, HD, HD), lambda bi, c: (li, bi, 0, 0, 0))

    def st_n(li):
        return pl.BlockSpec((None, bb, N_HEADS, HD), lambda bi, c: (li, bi, 0, 0))

    c_specs, c_args, aliases = _carry_args(prev, N_MLSTM_IN, (1, 2, 3))
    return pl.pallas_call(
        _with_carried_outputs(functools.partial(_mlstm_kernel, bb=bb, chunk=chunk, wide_chains=wide_chains),
                              N_MLSTM_IN, len(c_args)),
        grid=(b // bb, nc),
        in_specs=[col(COL_BQ), col(COL_BK), col(COL_BV), col(COL_BO),
                  pl.BlockSpec((bb, chunk, 128), lambda bi, c: (bi, c, 0)),
                  pl.BlockSpec((1, HD), lambda bi, c: (0, 0)),
                  st_c(state_layer), st_n(state_layer), st_n(state_layer)] + c_specs,
        out_specs=(pl.BlockSpec((bb, chunk, MIX_W), lambda bi, c: (bi, c, 0)),
                   st_c(layer), st_n(layer), st_n(layer)),
        out_shape=(jax.ShapeDtypeStruct((b, t, MIX_W), F32),
                   jax.ShapeDtypeStruct((depth, b, N_HEADS, HD, HD), F32),
                   jax.ShapeDtypeStruct((depth, b, N_HEADS, HD), F32),
                   jax.ShapeDtypeStruct((depth, b, N_HEADS, HD), F32)),
        input_output_aliases=aliases,
        compiler_params=_cparams(("parallel", "arbitrary"), 40),
        name="mlstm",
    )(h3d, h3d, h3d, h3d, so3d, hg.reshape(1, HD), c0, n0, m0, *c_args)


N_GLA_IN = 8
GLA_SUB = 16


def _gla_kernel(q_ref, k_ref, v_ref, so_ref, w2_ref, b2_ref, hg_ref, s0_ref, o_ref, s_ref, *, bb, chunk,
                wide_seqs):
    L = chunk
    Lp = max(L, BF16_ROWS)
    sub = GLA_SUB

    @pl.when(pl.program_id(1) == 0)
    def _():
        s_ref[...] = s0_ref[...]

    tril = _tril_ones(Lp)
    hg = hg_ref[...]
    w2 = w2_ref[...].astype(BF16)
    b2 = b2_ref[...]
    q_scale = GLA_DK ** -0.5
    real_row = lax.broadcasted_iota(jnp.int32, (Lp, 1), 0) < L

    heads = range(N_HEADS)
    dkw = N_HEADS * GLA_DK
    head_of_lane = lax.broadcasted_iota(jnp.int32, (1, dkw), 1) >> GLA_DK_SHIFT
    zero_blk = jnp.zeros((GLA_DK, GLA_DV), F32)

    def run_seqs(idx):
        n = range(len(idx))
        s_old = [[s_ref[bi, h] for h in heads] for bi in idx]
        pre = [_dot(_pad_rows(so_ref[bi], Lp).astype(BF16), w2) + b2 for bi in idx]
        la = [jnp.where(real_row, _log_sigmoid(pre[u]) / GLA_TAU, 0.0) for u in n]
        bcs = [_dot_exact(tril, la[u]) for u in n]
        q = [_pad_rows(q_ref[bi], Lp) * q_scale for bi in idx]
        k = [_pad_rows(k_ref[bi], Lp) for bi in idx]
        vb = [_pad_rows(v_ref[bi], Lp).astype(BF16) for bi in idx]
        s_bd = [jnp.concatenate(
            [jnp.concatenate([s_old[u][h] if g == h else zero_blk for g in heads], axis=1) for h in heads],
            axis=0) for u in n]
        inter = [_dot((q[u] * jnp.exp(bcs[u])).astype(BF16), s_bd[u].astype(BF16)) for u in n]
        for i in range(Lp // sub):
            r0 = i * sub
            hi = r0 + sub
            rows = min(hi, L) - r0
            rr = (lax.broadcasted_iota(jnp.int32, (N_HEADS * sub, hi), 0) & (sub - 1)) + r0
            cc = lax.broadcasted_iota(jnp.int32, (N_HEADS * sub, hi), 1)
            base = [bcs[u][r0 - 1:r0, :] if i > 0 else jnp.zeros((1, dkw), F32) for u in n]
            qi = [q[u][r0:hi, :] * jnp.exp(bcs[u][r0:hi, :] - base[u]) for u in n]
            ke = [(k[u][0:hi, :] * jnp.exp(base[u] - bcs[u][0:hi, :])).astype(BF16) for u in n]
            q_heads = [jnp.concatenate([jnp.where(head_of_lane == h, qi[u], 0.0) for h in heads], axis=0)
                       for u in n]
            a = [jnp.where(cc <= rr, _dot_nt(q_heads[u].astype(BF16), ke[u]), 0.0).astype(BF16) for u in n]
            for u, bi in enumerate(idx):
                for h in heads:
                    vsl = slice(h * GLA_DV, (h + 1) * GLA_DV)
                    oi = inter[u][r0:hi, vsl] + _dot(a[u][h * sub:(h + 1) * sub, :], vb[u][0:hi, vsl])
                    o_ref[bi, r0:r0 + rows, vsl] = _rms_lanes(oi[0:rows], hg)
        b_end = [bcs[u][Lp - 1:Lp, :] for u in n]
        kd = [(k[u] * jnp.exp(b_end[u] - bcs[u])).astype(BF16) for u in n]
        upd = [_dot_tn(kd[u], vb[u]) for u in n]
        dcol = [_row_to_col(jnp.exp(b_end[u])) for u in n]
        for u, bi in enumerate(idx):
            for h in heads:
                s_ref[bi, h] = (dcol[u][h * GLA_DK:(h + 1) * GLA_DK, :] * s_old[u][h]
                                + upd[u][h * GLA_DK:(h + 1) * GLA_DK, h * GLA_DV:(h + 1) * GLA_DV])

    unroll = min(BATCH_UNROLL, bb)
    wide = min(wide_seqs, unroll)

    def group(g, carry):
        for u0 in range(0, unroll, wide):
            run_seqs([g * unroll + u0 + u for u in range(wide)])
        return carry

    if bb == unroll:
        group(0, 0)
    else:
        lax.fori_loop(0, bb // unroll, group, 0)


def _gla(h3d, so3d, w2pad, b2, hg, s0, state_layer, layer, depth, prev, bb, chunk):
    b, t, _ = h3d.shape
    nc = t // chunk

    def st(li):
        return pl.BlockSpec((None, bb, N_HEADS, GLA_DK, GLA_DV), lambda bi, c: (li, bi, 0, 0, 0))

    c_specs, c_args, aliases = _carry_args(prev, N_GLA_IN, (1,))
    return pl.pallas_call(
        _with_carried_outputs(functools.partial(_gla_kernel, bb=bb, chunk=chunk, wide_seqs=4),
                              N_GLA_IN, len(c_args)),
        grid=(b // bb, nc),
        in_specs=[pl.BlockSpec((bb, chunk, 256), lambda bi, c: (bi, c, COL_CQ // 256)),
                  pl.BlockSpec((bb, chunk, 256), lambda bi, c: (bi, c, COL_CK // 256)),
                  pl.BlockSpec((bb, chunk, MIX_W), lambda bi, c: (bi, c, COL_CV // MIX_W)),
                  pl.BlockSpec((bb, chunk, 128), lambda bi, c: (bi, c, 0)),
                  pl.BlockSpec((128, 256), lambda bi, c: (0, 0)),
                  pl.BlockSpec((1, 256), lambda bi, c: (0, 0)),
                  pl.BlockSpec((1, GLA_DV), lambda bi, c: (0, 0)),
                  st(state_layer)] + c_specs,
        out_specs=(pl.BlockSpec((bb, chunk, MIX_W), lambda bi, c: (bi, c, 0)), st(layer)),
        out_shape=(jax.ShapeDtypeStruct((b, t, MIX_W), F32),
                   jax.ShapeDtypeStruct((depth, b, N_HEADS, GLA_DK, GLA_DV), F32)),
        input_output_aliases=aliases,
        compiler_params=_cparams(("parallel", "arbitrary"), 40),
        name="gla",
    )(h3d, h3d, h3d, so3d, w2pad, b2.reshape(1, 256), hg.reshape(1, GLA_DV), s0, *c_args)


def _select_blocks(imp, qpos_col, ns):
    r = imp.shape[0]
    j = lax.broadcasted_iota(jnp.int32, (r, 128), 1)
    cur = qpos_col >> SEL_SHIFT
    valid = j <= cur
    forced = (j == 0) | (valid & (j >= cur - 1))
    val = jnp.where(forced, 1e4, jnp.where(valid, imp, -1e4))
    val = jnp.where(j < ns, val, -3e38)
    rank = jnp.zeros((r, 128), F32)
    for i in range(ns):
        ci = val[:, i:i + 1]
        beats = (ci > val) | ((ci == val) & (j > i))
        rank = rank + jnp.where(beats, 1.0, 0.0)
    return (rank < float(min(N_SEL, ns))) & (j < ns)


def _pair_matrix(nc):
    c = jnp.arange(nc)[:, None]
    j = jnp.arange(128)[None, :]
    return (c // (SEL_BLOCK // CMP_BLOCK) == j).astype(F32)


def _expand_matrix(tk):
    j = jnp.arange(128)[:, None]
    s = jnp.arange(tk)[None, :]
    return (s // SEL_BLOCK == j).astype(BF16)


NSA_TQ = 128


def _nsa_prompt_kernel(q_ref, kcmp_ref, vcmp_ref, kv_ref, win_ref, so_ref, e_ref, pair_ref, o_ref, *, t_len):
    tq = NSA_TQ
    nc = t_len // CMP_BLOCK
    ns = -(-t_len // SEL_BLOCK)
    i = pl.program_id(1)
    start = pl.multiple_of(i * tq, tq)
    qpos_col = start + lax.broadcasted_iota(jnp.int32, (tq, 1), 0)
    so = so_ref[...]
    q_all = jnp.concatenate([q_ref[:, h * HD:(h + 1) * HD] for h in range(N_HEADS)], axis=0)
    sc = _dot_nt(q_all, kcmp_ref[...].astype(BF16)) * ATT_SCALE
    qpos4 = start + (lax.broadcasted_iota(jnp.int32, (N_HEADS * tq, 1), 0) & (tq - 1))
    cend = lax.broadcasted_iota(jnp.int32, (N_HEADS * tq, nc), 1) * CMP_BLOCK + (CMP_BLOCK - 1)
    (ec,), invc = _softmax_parts([sc], [cend <= qpos4])
    pc = ec * invc
    o_cmp = _dot(pc.astype(BF16), vcmp_ref[...].astype(BF16))
    imp_c = pc[0:tq]
    for h in range(1, N_HEADS):
        imp_c = imp_c + pc[h * tq:(h + 1) * tq]
    imp = _dot_exact(imp_c, pair_ref[...])
    sel_b = jnp.where(_select_blocks(imp, qpos_col, ns), 1.0, 0.0).astype(BF16)
    band = WINDOW + tq
    kwin = win_ref[pl.ds(start, band), 0:HD]
    vwin = win_ref[pl.ds(start, band), HD:2 * HD]
    wpos = start - WINDOW + lax.broadcasted_iota(jnp.int32, (tq, band), 1)
    wmask = (wpos <= qpos_col) & (wpos > qpos_col - WINDOW) & (wpos >= 0)
    for h in range(N_HEADS):
        sl = slice(h * HD, (h + 1) * HD)
        (ew,), invw = _softmax_parts([_dot_nt(q_ref[:, sl], kwin) * ATT_SCALE], [wmask])
        g0 = so[:, SM_DG + 3 * h:SM_DG + 3 * h + 1]
        g2 = so[:, SM_DG + 3 * h + 2:SM_DG + 3 * h + 3]
        o_ref[:, sl] = g0 * o_cmp[h * tq:(h + 1) * tq] + g2 * (_dot(ew.astype(BF16), vwin) * invw)

    n_cls = -(-t_len // KEY_CLASS)
    cls = (start + tq - 1) // KEY_CLASS

    def run_selected(kl):
        sel_keys = _dot(sel_b, e_ref[:, 0:kl]) > 0.5
        kpos = lax.broadcasted_iota(jnp.int32, (tq, kl), 1)
        smask = sel_keys & (kpos <= qpos_col)
        ks = kv_ref[0:kl, 2 * HD:3 * HD]
        vs = kv_ref[0:kl, 3 * HD:4 * HD]
        for h in range(N_HEADS):
            sl = slice(h * HD, (h + 1) * HD)
            (es,), invs = _softmax_parts([_dot_nt(q_ref[:, sl], ks) * ATT_SCALE], [smask])
            g1 = so[:, SM_DG + 3 * h + 1:SM_DG + 3 * h + 2]
            o_ref[:, sl] = o_ref[:, sl] + g1 * (_dot(es.astype(BF16), vs) * invs)

    for c in range(n_cls):
        pl.when(cls == c)(functools.partial(run_selected, min((c + 1) * KEY_CLASS, t_len)))


def _nsa_prompt(nq, kcmp, vcmp, nkv, win_pad, so):
    b, t, _ = nq.shape
    tq = NSA_TQ
    nc = t // CMP_BLOCK
    e_mat = _expand_matrix(t)
    pair = _pair_matrix(nc)
    return pl.pallas_call(
        functools.partial(_nsa_prompt_kernel, t_len=t),
        grid=(b, t // tq),
        in_specs=[pl.BlockSpec((None, tq, MIX_W), lambda bi, i: (bi, i, 0)),
                  pl.BlockSpec((None, nc, HD), lambda bi, i: (bi, 0, 0)),
                  pl.BlockSpec((None, nc, HD), lambda bi, i: (bi, 0, 0)),
                  pl.BlockSpec((None, t, 768), lambda bi, i: (bi, 0, 0)),
                  pl.BlockSpec((None, t + WINDOW, 256), lambda bi, i: (bi, 0, 0)),
                  pl.BlockSpec((None, tq, 128), lambda bi, i: (bi, i, 0)),
                  pl.BlockSpec((128, t), lambda bi, i: (0, 0)),
                  pl.BlockSpec((nc, 128), lambda bi, i: (0, 0))],
        out_specs=pl.BlockSpec((None, tq, MIX_W), lambda bi, i: (bi, i, 0)),
        out_shape=jax.ShapeDtypeStruct((b, t, MIX_W), F32),
        compiler_params=_cparams(("parallel", "arbitrary"), 48),
        name="nsa_prompt",
    )(nq, kcmp, vcmp, nkv, win_pad, so, e_mat, pair)


N_NSA_S_IN = 11


def _nsa_sample_kernel(pt_ref, q_ref, kvn_ref, winn_ref, so_ref, win_ref, wk_ref, wv_ref, kcg_ref,
                       e_ref, pair_ref, *rest, n_pages, t_new):
    del pt_ref
    pages = rest[:n_pages]
    o_ref, wout_ref = rest[n_pages], rest[n_pages + 1]
    cmp_k, cmp_v = rest[n_pages + 2], rest[n_pages + 3]
    past = n_pages * PAGE_SIZE
    nc = past // CMP_BLOCK
    ns = -(-(past + t_new) // SEL_BLOCK)
    per_page = PAGE_SIZE // CMP_BLOCK
    n_grp = n_pages // PAGE_GROUP
    gw = PAGE_GROUP * PAGE_SIZE
    rows = N_HEADS * t_new
    so = so_ref[...]

    def page_rows(p, j):
        return pages[p][pl.ds(j, PAGE_SIZE, stride=4), :]

    def group_rows(g, j):
        return jnp.concatenate([page_rows(g * PAGE_GROUP + u, j).astype(BF16) for u in range(PAGE_GROUP)], axis=0)

    for p in range(n_pages):
        kc = page_rows(p, 0).reshape(per_page, CMP_BLOCK, HD)
        vc = page_rows(p, 1).reshape(per_page, CMP_BLOCK, HD)
        cmp_k[p * per_page:(p + 1) * per_page, :] = jnp.sum(kc * wk_ref[...][None], axis=1)
        cmp_v[p * per_page:(p + 1) * per_page, :] = jnp.sum(vc * wv_ref[...][None], axis=1)
    kcmp = _rms_lanes(cmp_k[...], kcg_ref[...]).astype(BF16)
    vcmp = cmp_v[...].astype(BF16)

    q_all = jnp.concatenate([q_ref[:, h * HD:(h + 1) * HD].astype(F32) for h in range(N_HEADS)], axis=0)
    q_b = q_all.astype(BF16)
    trow = lax.broadcasted_iota(jnp.int32, (rows, 1), 0) & (t_new - 1)
    qpos4 = past + trow
    qpos_col = past + lax.broadcasted_iota(jnp.int32, (t_new, 1), 0)
    kvn = kvn_ref[...].astype(F32)
    npos = past + lax.broadcasted_iota(jnp.int32, (rows, t_new), 1)
    causal_new = npos <= qpos4
    sc = _dot_nt(q_b, kcmp) * ATT_SCALE
    s_list = [_dot_nt(q_b, group_rows(g, 2)) * ATT_SCALE for g in range(n_grp)]
    s_list.append(_dot_nt(q_all, kvn[:, 2 * HD:3 * HD]) * ATT_SCALE)
    cend = lax.broadcasted_iota(jnp.int32, (rows, nc), 1) * CMP_BLOCK + (CMP_BLOCK - 1)
    (ec,), invc = _softmax_parts([sc], [cend <= qpos4])
    pc = ec * invc
    imp_c = pc[0:t_new]
    for h in range(1, N_HEADS):
        imp_c = imp_c + pc[h * t_new:(h + 1) * t_new]
    imp = _dot_exact(imp_c, pair_ref[...])
    o_cmp = _dot(pc.astype(BF16), vcmp)
    w_buf = win_ref.shape[0] // 2
    kwb = win_ref[pl.ds(0, w_buf, stride=2), :].astype(BF16)
    vwb = win_ref[pl.ds(1, w_buf, stride=2), :].astype(BF16)
    wpos = past - w_buf + lax.broadcasted_iota(jnp.int32, (rows, w_buf), 1)
    wmask = (wpos <= qpos4) & (wpos > qpos4 - WINDOW) & (wpos >= 0)
    sw_list = [_dot_nt(q_b, kwb) * ATT_SCALE, _dot_nt(q_all, kvn[:, 4 * HD:5 * HD]) * ATT_SCALE]
    ew, invw = _softmax_parts(sw_list, [wmask, causal_new & (npos > qpos4 - WINDOW)])
    o_win = (_dot(ew[0].astype(BF16), vwb) + _dot(ew[1], kvn[:, 5 * HD:6 * HD])) * invw
    sel = _select_blocks(imp, qpos_col, ns)
    self32 = jnp.where(sel, 1.0, 0.0)
    sel_past = _dot(self32.astype(BF16), e_ref[...])
    sel_past4 = jnp.concatenate([sel_past] * N_HEADS, axis=0) > 0.5
    new_blk = past // SEL_BLOCK
    sel_new = jnp.concatenate([self32[:, new_blk:new_blk + 1]] * N_HEADS, axis=0) > 0.5
    m_list = []
    for g in range(n_grp):
        kpos = g * gw + lax.broadcasted_iota(jnp.int32, (rows, gw), 1)
        m_list.append(sel_past4[:, g * gw:(g + 1) * gw] & (kpos <= qpos4))
    m_list.append(sel_new & causal_new)
    es, invs = _softmax_parts(s_list, m_list)
    o_sel = _dot(es[n_grp], kvn[:, 3 * HD:4 * HD])
    for g in range(n_grp):
        o_sel = o_sel + _dot(es[g].astype(BF16), group_rows(g, 3))
    o_sel = o_sel * invs
    for h in range(N_HEADS):
        r = slice(h * t_new, (h + 1) * t_new)
        g0 = so[:, SM_DG + 3 * h:SM_DG + 3 * h + 1]
        g1 = so[:, SM_DG + 3 * h + 1:SM_DG + 3 * h + 2]
        g2 = so[:, SM_DG + 3 * h + 2:SM_DG + 3 * h + 3]
        o_ref[:, h * HD:(h + 1) * HD] = g0 * o_cmp[r] + g1 * o_sel[r] + g2 * o_win[r]
    keep = 2 * (w_buf - t_new)
    wout_ref[0:keep, :] = win_ref[2 * t_new:2 * w_buf, :]
    wout_ref[keep:2 * w_buf, :] = winn_ref[...]


def _nsa_sample(layer, depth, prev, page_table, nq, nkv, win_new, so, cache4, win_state4, wk_b, wv_b, kcg):
    b, t, _ = nq.shape
    n_pages = page_table.shape[1]
    past = n_pages * PAGE_SIZE
    nc = past // CMP_BLOCK
    w2 = win_state4.shape[2]
    e_mat = _expand_matrix(past)
    pair = _pair_matrix(nc)

    def tok(rows, width):
        return pl.BlockSpec((None, rows, width), lambda bi, pt: (bi, 0, 0))

    def const(shape):
        return pl.BlockSpec(shape, lambda bi, pt: (0, 0))

    in_specs = [tok(t, MIX_W), tok(t, 768), tok(2 * t, HD), tok(t, 128),
                pl.BlockSpec((None, None, w2, 128), lambda bi, pt: (layer, bi, 0, 0)),
                const((CMP_BLOCK, 128)), const((CMP_BLOCK, 128)), const((1, 128)),
                const((128, past)), const((nc, 128))]
    in_specs += [pl.BlockSpec((None, None, PAGE_SIZE * 4, 128), lambda bi, pt, p=p: (layer, pt[bi, p], 0, 0))
                 for p in range(n_pages)]
    n_in = N_NSA_S_IN + n_pages
    c_specs, c_args, aliases = _carry_args(prev, n_in, (1,))
    grid_spec = pltpu.PrefetchScalarGridSpec(
        num_scalar_prefetch=1, grid=(b,), in_specs=in_specs + c_specs,
        out_specs=(pl.BlockSpec((None, t, MIX_W), lambda bi, pt: (bi, 0, 0)),
                   pl.BlockSpec((None, None, w2, 128), lambda bi, pt: (layer, bi, 0, 0))),
        scratch_shapes=[pltpu.VMEM((nc, HD), F32), pltpu.VMEM((nc, HD), F32)])
    return pl.pallas_call(
        _with_carried_outputs(functools.partial(_nsa_sample_kernel, n_pages=n_pages, t_new=t), n_in, len(c_args)),
        grid_spec=grid_spec,
        out_shape=(jax.ShapeDtypeStruct((b, t, MIX_W), F32),
                   jax.ShapeDtypeStruct((depth, b, w2, 128), F32)),
        input_output_aliases=aliases,
        compiler_params=_cparams(("arbitrary",), 48),
        name="nsa_sample",
    )(page_table, nq, nkv, win_new, so, win_state4, wk_b, wv_b, kcg, e_mat, pair,
      *([cache4] * n_pages), *c_args)


OUT_TM = 256


def _out_kernel(x_ref, oa_ref, ob_ref, oc_ref, od_ref, z_ref, g0_ref, g1_ref, g2_ref, g3_ref,
                wb_ref, wo_ref, y_ref):
    acc = None
    for g, (o_r, g_r) in enumerate(((oa_ref, g0_ref), (ob_ref, g1_ref), (oc_ref, g2_ref), (od_ref, g3_ref))):
        z = z_ref[:, g * MIX_W:(g + 1) * MIX_W]
        br = (o_r[...] * (z * _sigmoid(z))).astype(BF16)
        term = (jnp.tanh(g_r[...]) + 1.0) * _dot(br, wb_ref[g])
        acc = term if acc is None else acc + term
    y_ref[...] = x_ref[...] + _dot(acc.astype(BF16), wo_ref[...])


def _out_proj(x2d, oa, ob, oc, od, h2d, wb_all, wo_all, layer):
    m = x2d.shape[0]
    tm = OUT_TM

    def rows(width):
        return pl.BlockSpec((tm, width), lambda i: (i, 0))

    def gate(g):
        return pl.BlockSpec((tm, D_MODEL), lambda i, g=g: (i, COL_GATE // D_MODEL + g))

    single = pl.Buffered(1)
    return pl.pallas_call(
        _out_kernel,
        grid=(m // tm,),
        in_specs=[rows(D_MODEL), rows(MIX_W), rows(MIX_W), rows(MIX_W), rows(MIX_W),
                  rows(D_MODEL), gate(0), gate(1), gate(2), gate(3),
                  pl.BlockSpec((None, N_HEADS, MIX_W, D_MODEL), lambda i: (layer, 0, 0, 0), pipeline_mode=single),
                  pl.BlockSpec((None, D_MODEL, D_MODEL), lambda i: (layer, 0, 0), pipeline_mode=single)],
        out_specs=rows(D_MODEL),
        out_shape=jax.ShapeDtypeStruct((m, D_MODEL), F32),
        compiler_params=_cparams(("parallel",), 56),
        name="out_proj",
    )(x2d, oa, ob, oc, od, h2d, h2d, h2d, h2d, h2d, wb_all, wo_all)


def _layer_params(l, p):
    zeros = jnp.zeros((2, HD), F32)
    gains = jnp.concatenate([p["fox_qg"][l][None], p["fox_kg"][l][None], p["nsa_qg"][l][None],
                             p["nsa_ksg"][l][None], p["nsa_kwg"][l][None], p["nsa_kcg"][l][None], zeros], axis=0)
    bias = jnp.concatenate([p["fox_bf"][l], p["mlstm_bi"][l], p["mlstm_bf"][l], jnp.zeros((GLA_RANK,), F32),
                            p["nsa_bg"][l].reshape(-1), jnp.zeros((128 - SM_DG - 12,), F32)]).reshape(1, 128)
    w2pad = jnp.zeros((128, 256), F32).at[SM_CA:SM_CA + GLA_RANK].set(p["gla_w2"][l])
    return dict(
        norm_g=p["norm_g"][l], gains=gains, bias=bias,
        wk_b=jnp.broadcast_to(p["nsa_wk"][l][:, None], (CMP_BLOCK, 128)),
        wv_b=jnp.broadcast_to(p["nsa_wv"][l][:, None], (CMP_BLOCK, 128)),
        kcg=p["nsa_kcg"][l].reshape(1, HD),
        mlstm_hg=p["mlstm_hg"][l], w2pad=w2pad, gla_b2=p["gla_b2"][l], gla_hg=p["gla_hg"][l])


def _layer(x3d, lp, shared, st, prev, *, is_prompt, layer, depth):
    b, t, _ = x3d.shape
    m = b * t
    x2d = x3d.reshape(m, D_MODEL)
    xn = _rmsnorm(x2d, lp["norm_g"])
    h2d = _in_proj(xn, shared["w_in"], layer)
    h3d = h2d.reshape(b, t, N_PROJ)
    (foxkv, nsakv, win, so, fq, fk, fv, nq, nkv, kcmp, vcmp) = _prep(
        h2d, lp["gains"], lp["bias"], lp["wk_b"], lp["wv_b"], layer, depth,
        None if prev is None else (prev["fox_kv"], prev["nsa_kv"]))
    so3 = so.reshape(b, t, 128)
    fq3, fk3, fv3 = fq.reshape(b, t, MIX_W), fk.reshape(b, t, MIX_W), fv.reshape(b, t, MIX_W)
    nq3 = nq.reshape(b, t, MIX_W)
    nkv3 = nkv.reshape(b, t, 768)
    chunk = min(64, t)
    carried = dict(fox_kv=foxkv, nsa_kv=nsakv)
    if is_prompt:
        lf_t = so3[:, :, SM_AF:SM_AF + N_HEADS].transpose(0, 2, 1).reshape(b * N_HEADS, t)
        f_row = _cumsum_rows(lf_t).reshape(b, N_HEADS, t)
        f_col = f_row.transpose(0, 2, 1)
        o_a = _fox_prompt(fq3, fk3, fv3, f_col, f_row)
        win_pad = jnp.pad(nkv3[:, :, 4 * HD:], ((0, 0), (WINDOW, 0), (0, 0)))
        o_d = _nsa_prompt(nq3, kcmp.reshape(b, t // CMP_BLOCK, HD), vcmp.reshape(b, t // CMP_BLOCK, HD),
                          nkv3, win_pad, so3)
        win_out = win.reshape(b, t, 2, HD)[:, -min(WINDOW, t):]
        bb = min(BATCH_UNROLL, b)
    else:
        o_a = _fox_sample(layer, shared["page_table"], fq3, fk3, fv3, so3, shared["fox_kv4"], shared["fox_lf4"])
        o_d, win_out = _nsa_sample(layer, depth, None if prev is None else (prev["win"],),
                                   shared["page_table"], nq3, nkv3, win.reshape(b, 2 * t, HD), so3,
                                   shared["nsa_kv4"], shared["nsa_win4"], lp["wk_b"], lp["wv_b"], lp["kcg"])
        bb = 2 * BATCH_UNROLL
    o_b, c_new, n_new, m_new = _mlstm(
        h3d, so3, lp["mlstm_hg"], st["C"], st["n"], st["m"], st["layer"], layer, depth,
        None if prev is None else (prev["C"], prev["n"], prev["m"]), bb, chunk)
    o_c, s_new = _gla(h3d, so3, lp["w2pad"], lp["gla_b2"], lp["gla_hg"], st["S"], st["layer"], layer, depth,
                      None if prev is None else (prev["S"],), bb, chunk)
    y2d = _out_proj(x2d, o_a.reshape(m, MIX_W), o_b.reshape(m, MIX_W), o_c.reshape(m, MIX_W),
                    o_d.reshape(m, MIX_W), h2d, shared["w_branch"], shared["w_out"], layer)
    carried.update(win=win_out, C=c_new, n=n_new, m=m_new, S=s_new)
    return y2d.reshape(b, t, D_MODEL), carried, so3[:, :, SM_AF:SM_AF + N_HEADS]


def kernel(x_prompt, x_sample, cache_fox_kv, cache_fox_logf, cache_nsa_kv, page_table, state_nsa_win, state_mlstm_C, state_mlstm_n, state_mlstm_m, state_gla_S, norm_g, w_in, fox_qg, fox_kg, fox_bf, mlstm_bi, mlstm_bf, mlstm_hg, gla_w2, gla_b2, gla_hg, nsa_qg, nsa_kcg, nsa_ksg, nsa_kwg, nsa_wk, nsa_wv, nsa_bg, w_branch, w_out):
    params = dict(norm_g=norm_g, fox_qg=fox_qg, fox_kg=fox_kg, fox_bf=fox_bf, mlstm_bi=mlstm_bi,
                  mlstm_bf=mlstm_bf, mlstm_hg=mlstm_hg, gla_w2=gla_w2, gla_b2=gla_b2, gla_hg=gla_hg,
                  nsa_qg=nsa_qg, nsa_kcg=nsa_kcg, nsa_ksg=nsa_ksg, nsa_kwg=nsa_kwg, nsa_wk=nsa_wk,
                  nsa_wv=nsa_wv, nsa_bg=nsa_bg)
    depth = w_in.shape[0]
    bp, tp, _ = x_prompt.shape
    db, ts, _ = x_sample.shape
    n_pool = cache_fox_kv.shape[1]
    w_buf = state_nsa_win.shape[2]
    shared = dict(
        w_in=_permute_w_in(w_in), w_branch=(0.5 * w_branch).astype(BF16), w_out=w_out.astype(BF16),
        page_table=page_table,
        fox_kv4=cache_fox_kv.reshape(depth, n_pool, PAGE_SIZE * 2 * N_HEADS, HD),
        fox_lf4=cache_fox_logf.transpose(0, 1, 3, 2),
        nsa_kv4=cache_nsa_kv.reshape(depth, n_pool, PAGE_SIZE * 4, HD),
        nsa_win4=state_nsa_win.reshape(depth, db, w_buf * 2, HD))
    st_p = dict(C=jnp.zeros((1, bp, N_HEADS, HD, HD), F32), n=jnp.zeros((1, bp, N_HEADS, HD), F32),
                m=jnp.zeros((1, bp, N_HEADS, HD), F32), S=jnp.zeros((1, bp, N_HEADS, GLA_DK, GLA_DV), F32),
                layer=0)
    m_s = jnp.broadcast_to(state_mlstm_m[..., None], state_mlstm_m.shape + (HD,))
    y_p, y_s = x_prompt, x_sample
    prev_p = prev_s = None
    lf_p, lf_s, win_p = [], [], []
    for l in range(depth):
        lp = _layer_params(l, params)
        st_s = dict(C=state_mlstm_C, n=state_mlstm_n, m=m_s, S=state_gla_S, layer=l)
        y_p, prev_p, lf = _layer(y_p, lp, shared, st_p, prev_p, is_prompt=True, layer=l, depth=depth)
        lf_p.append(lf)
        win_p.append(prev_p["win"])
        y_s, prev_s, lf = _layer(y_s, lp, shared, st_s, prev_s, is_prompt=False, layer=l, depth=depth)
        lf_s.append(lf)

    def finish(c, b, t, win, lf):
        return (c["fox_kv"].reshape(depth, b, t, 2, N_HEADS, HD), jnp.stack(lf, axis=0),
                c["nsa_kv"].reshape(depth, b, t, 4, HD), win,
                c["C"], c["n"], c["m"][..., 0], c["S"])

    out_p = finish(prev_p, bp, tp, jnp.stack(win_p, axis=0), lf_p)
    out_s = finish(prev_s, db, ts, prev_s["win"].reshape(depth, db, w_buf, 2, HD), lf_s)
    return (y_p, y_s) + out_p + out_s
```

```python
import functools

import jax
import jax.numpy as jnp
from jax import lax
from jax.experimental import pallas as pl
from jax.experimental.pallas import tpu as pltpu

F32 = jnp.float32
BF16 = jnp.bfloat16
HIGHEST = lax.Precision.HIGHEST

D_MODEL = 2048
MIX_W = 512
HD = 128
N_HEADS = 4
GLA_DK = 64
GLA_DK_SHIFT = 6
GLA_DV = 128
GLA_RANK = 16
GLA_TAU = 16.0
CMP_BLOCK = 32
SEL_BLOCK = 64
SEL_SHIFT = 6
N_SEL = 16
WINDOW = 512
PAGE_SIZE = 128
EPS = 1e-6
ATT_SCALE = HD ** -0.5
NEG_INF = float("-inf")
KEY_CLASS = 512
BF16_ROWS = 16

_REF_LAYOUT = (
    ("a_q", 512), ("a_k", 512), ("a_v", 512), ("a_f", 4), ("a_z", 512),
    ("b_q", 512), ("b_k", 512), ("b_v", 512), ("b_i", 4), ("b_f", 4), ("b_o", 512), ("b_z", 512),
    ("c_q", 256), ("c_k", 256), ("c_v", 512), ("c_a", 16), ("c_z", 512),
    ("d_q", 512), ("d_kv", 768), ("d_g", 12), ("d_z", 512),
    ("gate", 8192),
)
N_REF = sum(w for _, w in _REF_LAYOUT)
_MY_LAYOUT = (
    "a_z", "b_z", "c_z", "d_z", "gate",
    "a_q", "a_k", "a_v", "d_q", "b_q", "b_k", "b_v", "b_o", "c_v", "c_q", "c_k", "d_kv",
    "a_f", "b_i", "b_f", "c_a", "d_g",
)
N_PROJ = 16384
COL_Z = 0
COL_GATE = 2048
COL_AQ, COL_AK, COL_AV, COL_DQ = 10240, 10752, 11264, 11776
COL_BQ, COL_BK, COL_BV, COL_BO = 12288, 12800, 13312, 13824
COL_CV, COL_CQ, COL_CK, COL_DKV, COL_SMALL = 14336, 14848, 15104, 15360, 16128
SM_AF, SM_BI, SM_BF, SM_CA, SM_DG = 0, 4, 8, 12, 28


def _cparams(sem, vmem_mb):
    return pltpu.CompilerParams(dimension_semantics=sem, vmem_limit_bytes=vmem_mb * 1024 * 1024)


def _log_sigmoid(x):
    return jnp.minimum(x, 0.0) - jnp.log(1.0 + jnp.exp(-jnp.abs(x)))


def _sigmoid(x):
    return 0.5 * jnp.tanh(0.5 * x) + 0.5


def _rms_lanes(x, g):
    return (x * lax.rsqrt(jnp.mean(x * x, axis=-1, keepdims=True) + EPS)) * g


def _dot(a, b):
    return jnp.dot(a, b, preferred_element_type=F32)


def _dot_nt(a, b):
    return lax.dot_general(a, b, (((1,), (1,)), ((), ())), preferred_element_type=F32)


def _dot_tn(a, b):
    return lax.dot_general(a, b, (((0,), (0,)), ((), ())), preferred_element_type=F32)


def _dot_exact(a, b):
    return jnp.dot(a, b, precision=HIGHEST, preferred_element_type=F32)


def _eye_mask(n):
    return lax.broadcasted_iota(jnp.int32, (n, n), 0) == lax.broadcasted_iota(jnp.int32, (n, n), 1)


def _col_to_row(c):
    n = c.shape[0]
    return jnp.sum(jnp.where(_eye_mask(n), c, 0.0), axis=0, keepdims=True)


def _row_to_col(r):
    n = r.shape[1]
    return jnp.sum(jnp.where(_eye_mask(n), r, 0.0), axis=1, keepdims=True)


def _tril_ones(n):
    r = lax.broadcasted_iota(jnp.int32, (n, n), 0)
    c = lax.broadcasted_iota(jnp.int32, (n, n), 1)
    return jnp.where(c <= r, 1.0, 0.0).astype(F32)


def _cumsum_lanes(x):
    n = x.shape[-1]
    lane = lax.broadcasted_iota(jnp.int32, x.shape, x.ndim - 1)
    s = 1
    while s < n:
        x = x + jnp.where(lane >= s, pltpu.roll(x, s, axis=x.ndim - 1), 0.0)
        s *= 2
    return x


def _pad_rows(x, rows):
    if x.shape[0] == rows:
        return x
    return jnp.concatenate([x, jnp.zeros((rows - x.shape[0], x.shape[1]), x.dtype)], axis=0)


def _softmax_parts(s_list, mask_list):
    masked = [jnp.where(mk, s, NEG_INF) for s, mk in zip(s_list, mask_list)]
    m = masked[0].max(axis=-1, keepdims=True)
    for s in masked[1:]:
        m = jnp.maximum(m, s.max(axis=-1, keepdims=True))
    m = jnp.where(m > NEG_INF, m, 0.0)
    es = [jnp.exp(s - m) for s in masked]
    tot = es[0].sum(axis=-1, keepdims=True)
    for e in es[1:]:
        tot = tot + e.sum(axis=-1, keepdims=True)
    return es, 1.0 / jnp.maximum(tot, 1e-30)


def _with_carried_outputs(kernel_fn, n_in, n_carried):
    if not n_carried:
        return kernel_fn

    def wrapped(*refs):
        return kernel_fn(*refs[:n_in], *refs[n_in + n_carried:])

    return wrapped


def _carry_args(prev, first_in_index, out_indices):
    if prev is None:
        return [], [], {}
    specs = [pl.BlockSpec(memory_space=pl.ANY) for _ in prev]
    aliases = {first_in_index + k: oi for k, oi in enumerate(out_indices)}
    return specs, list(prev), aliases


def _segments():
    offs = {}
    off = 0
    for name, width in _REF_LAYOUT:
        offs[name] = (off, width)
        off += width
    segs = []
    dst = 0
    for name in _MY_LAYOUT:
        src, width = offs[name]
        segs.append((src, dst, width))
        dst += width
    return segs, dst


def _permute_kernel(wt_ref, o_ref, small_ref):
    segs, _ = _segments()
    for src, dst, width in segs:
        if width >= 128:
            w = wt_ref[src:src + width, :]
            if dst == COL_GATE:
                w = w * 0.5
            o_ref[dst:dst + width, :] = w.astype(BF16)
    small_ref[...] = jnp.zeros(small_ref.shape, F32)
    for src, dst, width in segs:
        if width < 128:
            small_ref[dst - COL_SMALL:dst - COL_SMALL + width, :] = wt_ref[src:src + width, :]
    o_ref[COL_SMALL:N_PROJ, :] = small_ref[...].astype(BF16)


def _permute_w_in(w_in):
    depth, d, n = w_in.shape
    tk = 128
    wt = jnp.transpose(w_in, (0, 2, 1))
    return pl.pallas_call(
        _permute_kernel,
        grid=(depth, d // tk),
        in_specs=[pl.BlockSpec((None, n, tk), lambda l, i: (l, 0, i))],
        out_specs=pl.BlockSpec((None, N_PROJ, tk), lambda l, i: (l, 0, i)),
        out_shape=jax.ShapeDtypeStruct((depth, N_PROJ, d), BF16),
        scratch_shapes=[pltpu.VMEM((N_PROJ - COL_SMALL, tk), F32)],
        compiler_params=_cparams(("parallel", "parallel"), 40),
        name="permute_w_in",
    )(wt)


def _norm_kernel(x_ref, g_ref, o_ref):
    o_ref[...] = _rms_lanes(x_ref[...], g_ref[...]).astype(o_ref.dtype)


def _rmsnorm(x2d, g):
    m, d = x2d.shape
    tm = min(512, m)
    return pl.pallas_call(
        _norm_kernel,
        grid=(m // tm,),
        in_specs=[pl.BlockSpec((tm, d), lambda i: (i, 0)), pl.BlockSpec((1, d), lambda i: (0, 0))],
        out_specs=pl.BlockSpec((tm, d), lambda i: (i, 0)),
        out_shape=jax.ShapeDtypeStruct((m, d), BF16),
        compiler_params=_cparams(("parallel",), 32),
        name="rmsnorm",
    )(x2d, g.reshape(1, d))


def _mm_kernel(a_ref, bt_ref, o_ref):
    o_ref[...] = _dot_nt(a_ref[...], bt_ref[...])


def _in_proj(a, wt_all, layer):
    m, k = a.shape
    n = wt_all.shape[1]
    tm, tn = min(1024, m), 1024
    return pl.pallas_call(
        _mm_kernel,
        grid=(m // tm, n // tn),
        in_specs=[pl.BlockSpec((tm, k), lambda i, j: (i, 0)),
                  pl.BlockSpec((None, tn, k), lambda i, j: (layer, j, 0))],
        out_specs=pl.BlockSpec((tm, tn), lambda i, j: (i, j)),
        out_shape=jax.ShapeDtypeStruct((m, n), F32),
        compiler_params=_cparams(("parallel", "arbitrary"), 48),
        name="in_proj",
    )(a, wt_all)


PREP_TM = 256
N_PREP_IN = 10


def _prep_kernel(aq_ref, ak_ref, av_ref, dq_ref, dkv_ref, sm_ref, gains_ref, bias_ref, wk_ref, wv_ref,
                 foxkv_ref, nsakv_ref, win_ref, so_ref, fq_ref, fk_ref, fv_ref, nq_ref, nkv_ref,
                 kcmp_ref, vcmp_ref):
    tm = PREP_TM
    g_fq, g_fk, g_nq = gains_ref[0:1, :], gains_ref[1:2, :], gains_ref[2:3, :]
    g_ks, g_kw, g_kc = gains_ref[3:4, :], gains_ref[4:5, :], gains_ref[5:6, :]
    for h in range(N_HEADS):
        sl = slice(h * HD, (h + 1) * HD)
        fq_ref[:, sl] = _rms_lanes(aq_ref[:, sl], g_fq).astype(BF16)
        kn = _rms_lanes(ak_ref[:, sl], g_fk)
        foxkv_ref[pl.ds(h, tm, stride=2 * N_HEADS), :] = kn
        fk_ref[:, sl] = kn.astype(BF16)
        v = av_ref[:, sl]
        foxkv_ref[pl.ds(N_HEADS + h, tm, stride=2 * N_HEADS), :] = v
        fv_ref[:, sl] = v.astype(BF16)
        nq_ref[:, sl] = _rms_lanes(dq_ref[:, sl], g_nq).astype(BF16)
    kc = dkv_ref[:, 0:128]
    vc = dkv_ref[:, 128:256]
    ks = _rms_lanes(dkv_ref[:, 256:384], g_ks)
    vs = dkv_ref[:, 384:512]
    kw = _rms_lanes(dkv_ref[:, 512:640], g_kw)
    vw = dkv_ref[:, 640:768]
    for j, a in enumerate((kc, vc, ks, vs)):
        nsakv_ref[pl.ds(j, tm, stride=4), :] = a
    win_ref[pl.ds(0, tm, stride=2), :] = kw
    win_ref[pl.ds(1, tm, stride=2), :] = vw
    for j, a in enumerate((kc, vc, ks, vs, kw, vw)):
        nkv_ref[:, j * HD:(j + 1) * HD] = a.astype(BF16)
    nb = tm // CMP_BLOCK
    kcs = jnp.sum(kc.reshape(nb, CMP_BLOCK, HD) * wk_ref[...][None], axis=1)
    kcmp_ref[...] = _rms_lanes(kcs, g_kc)
    vcmp_ref[...] = jnp.sum(vc.reshape(nb, CMP_BLOCK, HD) * wv_ref[...][None], axis=1)
    y = sm_ref[...] + bias_ref[...]
    lane = lax.broadcasted_iota(jnp.int32, y.shape, 1)
    is_ls = (lane < SM_BI) | ((lane >= SM_BF) & (lane < SM_CA))
    is_sg = (lane >= SM_DG) & (lane < SM_DG + 12)
    so_ref[...] = jnp.where(is_ls, _log_sigmoid(y), jnp.where(is_sg, _sigmoid(y), y))


def _prep(h2d, gains, bias, wk_b, wv_b, layer, depth, prev):
    m = h2d.shape[0]
    tm = PREP_TM

    def col(width, off):
        blk = off // width
        return pl.BlockSpec((tm, width), lambda i, blk=blk: (i, blk))

    def full(shape):
        return pl.BlockSpec(shape, lambda i: (0, 0))

    def rows(width):
        return pl.BlockSpec((tm, width), lambda i: (i, 0))

    def stacked(slots):
        return pl.BlockSpec((None, tm * slots, HD), lambda i: (layer, i, 0))

    out_shapes = (
        jax.ShapeDtypeStruct((depth, m * 8, HD), F32),
        jax.ShapeDtypeStruct((depth, m * 4, HD), F32),
        jax.ShapeDtypeStruct((m * 2, HD), F32),
        jax.ShapeDtypeStruct((m, 128), F32),
        jax.ShapeDtypeStruct((m, 512), BF16),
        jax.ShapeDtypeStruct((m, 512), BF16),
        jax.ShapeDtypeStruct((m, 512), BF16),
        jax.ShapeDtypeStruct((m, 512), BF16),
        jax.ShapeDtypeStruct((m, 768), BF16),
        jax.ShapeDtypeStruct((m // CMP_BLOCK, 128), F32),
        jax.ShapeDtypeStruct((m // CMP_BLOCK, 128), F32),
    )
    out_specs = (stacked(8), stacked(4), pl.BlockSpec((tm * 2, HD), lambda i: (i, 0)),
                 rows(128), rows(512), rows(512), rows(512), rows(512), rows(768),
                 pl.BlockSpec((tm // CMP_BLOCK, 128), lambda i: (i, 0)),
                 pl.BlockSpec((tm // CMP_BLOCK, 128), lambda i: (i, 0)))
    c_specs, c_args, aliases = _carry_args(prev, N_PREP_IN, (0, 1))
    return pl.pallas_call(
        _with_carried_outputs(_prep_kernel, N_PREP_IN, len(c_args)),
        grid=(m // tm,),
        in_specs=[col(512, COL_AQ), col(512, COL_AK), col(512, COL_AV), col(512, COL_DQ),
                  col(768, COL_DKV), col(128, COL_SMALL),
                  full((8, 128)), full((1, 128)), full((CMP_BLOCK, 128)), full((CMP_BLOCK, 128))] + c_specs,
        out_specs=out_specs,
        out_shape=out_shapes,
        input_output_aliases=aliases,
        compiler_params=_cparams(("parallel",), 32),
        name="prep",
    )(h2d, h2d, h2d, h2d, h2d, h2d, gains, bias, wk_b, wv_b, *c_args)


def _cumsum_rows_kernel(x_ref, o_ref):
    o_ref[...] = _cumsum_lanes(x_ref[...])


def _cumsum_rows(x):
    r, n = x.shape
    return pl.pallas_call(
        _cumsum_rows_kernel,
        grid=(1,),
        in_specs=[pl.BlockSpec((r, n), lambda i: (0, 0))],
        out_specs=pl.BlockSpec((r, n), lambda i: (0, 0)),
        out_shape=jax.ShapeDtypeStruct((r, n), F32),
        name="fox_cumsum",
    )(x)


FOX_TQ = 256


def _fox_prompt_kernel(q_ref, k_ref, v_ref, fc_ref, fr_ref, o_ref, *, t_len):
    tq = FOX_TQ
    i = pl.program_id(1)
    n_cls = -(-t_len // KEY_CLASS)
    cls = (i * tq + tq - 1) // KEY_CLASS

    def run(kl):
        qpos = i * tq + lax.broadcasted_iota(jnp.int32, (tq, kl), 0)
        kpos = lax.broadcasted_iota(jnp.int32, (tq, kl), 1)
        mask = kpos <= qpos
        for h in range(N_HEADS):
            sl = slice(h * HD, (h + 1) * HD)
            s = _dot_nt(q_ref[:, sl], k_ref[0:kl, sl]) * ATT_SCALE
            s = s + fc_ref[:, h:h + 1] - fr_ref[h:h + 1, 0:kl]
            (e,), inv = _softmax_parts([s], [mask])
            o_ref[:, sl] = _dot(e.astype(BF16), v_ref[0:kl, sl]) * inv

    for c in range(n_cls):
        pl.when(cls == c)(functools.partial(run, min((c + 1) * KEY_CLASS, t_len)))


def _fox_prompt(fq, fk, fv, f_col, f_row):
    b, t, _ = fq.shape
    tq = FOX_TQ
    return pl.pallas_call(
        functools.partial(_fox_prompt_kernel, t_len=t),
        grid=(b, t // tq),
        in_specs=[pl.BlockSpec((None, tq, MIX_W), lambda bi, i: (bi, i, 0)),
                  pl.BlockSpec((None, t, MIX_W), lambda bi, i: (bi, 0, 0)),
                  pl.BlockSpec((None, t, MIX_W), lambda bi, i: (bi, 0, 0)),
                  pl.BlockSpec((None, tq, N_HEADS), lambda bi, i: (bi, i, 0)),
                  pl.BlockSpec((None, N_HEADS, t), lambda bi, i: (bi, 0, 0))],
        out_specs=pl.BlockSpec((None, tq, MIX_W), lambda bi, i: (bi, i, 0)),
        out_shape=jax.ShapeDtypeStruct((b, t, MIX_W), F32),
        compiler_params=_cparams(("parallel", "arbitrary"), 48),
        name="fox_prompt",
    )(fq, fk, fv, f_col, f_row)


PAGE_GROUP = 2


def _fox_sample_stages(q_ref, kn_ref, vn_ref, so_ref, spread_ref, kv_pages, lf_pages, o_ref, lf_scr,
                       n_pages, t_new):
    past = n_pages * PAGE_SIZE
    w2 = 2 * PAGE_SIZE
    n_rows = n_pages * N_HEADS
    heads = range(N_HEADS)
    sls = [slice(h * HD, (h + 1) * HD) for h in heads]
    st = {"qk": {}}

    def kv_rows(p, h):
        return kv_pages[p][pl.ds(h, w2, stride=N_HEADS), :].astype(BF16)

    def prefix_sums():
        for p in range(n_pages):
            lf_scr[p * N_HEADS:(p + 1) * N_HEADS, :] = lf_pages[p][...]
        lf_rows = lf_scr[...]
        local = _dot_exact(lf_rows, spread_ref[...])
        tot = _dot_exact(lf_rows, jnp.ones((PAGE_SIZE, 128), F32))
        rr = lax.broadcasted_iota(jnp.int32, (n_rows, n_rows), 0)
        cc = lax.broadcasted_iota(jnp.int32, (n_rows, n_rows), 1)
        same_head = (rr & (N_HEADS - 1)) == (cc & (N_HEADS - 1))
        earlier = jnp.where(same_head & (cc < rr), 1.0, 0.0)
        up_to = jnp.where(same_head & (cc <= rr), 1.0, 0.0)
        st["f_int"] = local + _dot_exact(earlier, tot)[:, 0:1]
        st["f_total"] = _dot_exact(up_to, tot)[n_rows - N_HEADS:n_rows, 0:1]
        st["f_new"] = _dot_exact(_tril_ones(t_new), so_ref[...])
        st["qs"] = [q_ref[:, sl].astype(F32) for sl in sls]

    def pair_tile(p, pr):
        return jnp.concatenate([kv_rows(p, pr[0]), kv_rows(p, pr[1])], axis=1)

    def scores(pr):
        def run():
            zero = jnp.zeros((t_new, HD), F32)
            q_pair = jnp.concatenate(
                [jnp.concatenate([st["qs"][pr[0]], zero], axis=1),
                 jnp.concatenate([zero, st["qs"][pr[1]]], axis=1)], axis=0).astype(BF16)
            both = [_dot_nt(q_pair, pair_tile(p, pr)) for p in range(n_pages)]
            st["qk"][pr[0]] = [x[0:t_new] for x in both]
            st["qk"][pr[1]] = [x[t_new:2 * t_new] for x in both]
        return run

    def softmax():
        lane = lax.broadcasted_iota(jnp.int32, (t_new, w2), 1)
        is_key = (lane & 1) == 0
        kpos0 = lane >> 1
        tpos = lax.broadcasted_iota(jnp.int32, (t_new, w2), 0) + past
        rn = lax.broadcasted_iota(jnp.int32, (t_new, t_new), 0)
        cn = lax.broadcasted_iota(jnp.int32, (t_new, t_new), 1)
        m_list = [is_key & (kpos0 + p * PAGE_SIZE <= tpos) for p in range(n_pages)] + [cn <= rn]
        qk_new = [_dot_nt(st["qs"][h], kn_ref[:, sls[h]].astype(F32)) for h in heads]
        st["w"] = []
        for h in heads:
            fq = st["f_new"][:, SM_AF + h:SM_AF + h + 1] + st["f_total"][h:h + 1, :]
            s_list = [st["qk"][h][p] * ATT_SCALE + fq - st["f_int"][p * N_HEADS + h:p * N_HEADS + h + 1, :]
                      for p in range(n_pages)]
            s_list.append(qk_new[h] * ATT_SCALE + fq - _col_to_row(fq))
            st["w"].append(_softmax_parts(s_list, m_list))

    def values(pr):
        def run():
            (es0, inv0), (es1, inv1) = st["w"][pr[0]], st["w"][pr[1]]
            acc = None
            for p in range(n_pages):
                w_pair = jnp.concatenate([pltpu.roll(es0[p], 1, axis=1), pltpu.roll(es1[p], 1, axis=1)], axis=0)
                part = _dot(w_pair.astype(BF16), pair_tile(p, pr))
                acc = part if acc is None else acc + part
            o0 = acc[0:t_new, 0:HD] + _dot(es0[n_pages], vn_ref[:, sls[pr[0]]].astype(F32))
            o1 = acc[t_new:2 * t_new, HD:2 * HD] + _dot(es1[n_pages], vn_ref[:, sls[pr[1]]].astype(F32))
            o_ref[:, sls[pr[0]]] = o0 * inv0
            o_ref[:, sls[pr[1]]] = o1 * inv1
        return run

    pairs = [(h, h + 1) for h in range(0, N_HEADS, 2)]
    return [prefix_sums, scores(pairs[0]), scores(pairs[1]), softmax, values(pairs[0]), values(pairs[1])]


def _spread_matrix():
    r = jnp.arange(PAGE_SIZE)[:, None]
    c = jnp.arange(2 * PAGE_SIZE)[None, :]
    return (2 * r <= c).astype(F32)


N_MLSTM_IN = 9
BATCH_UNROLL = 4


def _mlstm_kernel(q_ref, k_ref, v_ref, og_ref, so_ref, hg_ref, c0_ref, n0_ref, m0_ref,
                  o_ref, c_ref, n_ref, m_ref, *, bb, chunk, wide_chains):
    L = chunk
    Lp = max(L, BF16_ROWS)

    @pl.when(pl.program_id(1) == 0)
    def _():
        c_ref[...] = c0_ref[...]
        n_ref[...] = n0_ref[...]
        m_ref[...] = m0_ref[...]

    tril = _tril_ones(Lp)
    causal = tril > 0.5
    hg = hg_ref[...]
    lane = lax.broadcasted_iota(jnp.int32, (Lp - L, 128), 1) if Lp > L else None

    def run_chains(idx):
        so_of, csum_of = [], []
        for bi in idx:
            so = so_ref[bi]
            if Lp > L:
                pad = jnp.where((lane >= SM_BI) & (lane < SM_BF), -1e30, 0.0)
                so = jnp.concatenate([so, pad], axis=0)
            so_of.append(so)
            csum_of.append(_dot_exact(tril, so))
        ch = [(bi, h) for bi in idx for h in range(N_HEADS)]
        seq = [u for u in range(len(idx)) for _ in range(N_HEADS)]
        n = range(len(ch))
        sls = [slice(h * HD, (h + 1) * HD) for _, h in ch]
        q = [_pad_rows(q_ref[bi, :, sls[c]], Lp) for c, (bi, _) in enumerate(ch)]
        k = [_pad_rows(k_ref[bi, :, sls[c]], Lp) * ATT_SCALE for c, (bi, _) in enumerate(ch)]
        vb = [_pad_rows(v_ref[bi, :, sls[c]], Lp).astype(BF16) for c, (bi, _) in enumerate(ch)]
        qb = [x.astype(BF16) for x in q]
        ig = [so_of[seq[c]][:, SM_BI + h:SM_BI + h + 1] for c, (_, h) in enumerate(ch)]
        bcol = [csum_of[seq[c]][:, SM_BF + h:SM_BF + h + 1] for c, (_, h) in enumerate(ch)]
        c_st = [c_ref[bi, h] for bi, h in ch]
        n_st = [n_ref[bi, h:h + 1, :] for bi, h in ch]
        m_st = [m_ref[bi, h:h + 1, 0:1] for bi, h in ch]
        qk_raw = [_dot_nt(qb[c], k[c].astype(BF16)) for c in n]
        q_c = [_dot(qb[c], c_st[c].astype(BF16)) for c in n]
        dmat = [jnp.where(causal, bcol[c] - _col_to_row(bcol[c]) + _col_to_row(ig[c]), NEG_INF) for c in n]
        inter = [bcol[c] + m_st[c] for c in n]
        m_t = [jnp.maximum(inter[c], dmat[c].max(axis=-1, keepdims=True)) for c in n]
        w_inter = [jnp.exp(inter[c] - m_t[c]) for c in n]
        qk = [qk_raw[c] * jnp.exp(dmat[c] - m_t[c]) for c in n]
        qk_v = [_dot(qk[c].astype(BF16), vb[c]) for c in n]
        b_last = [bcol[c][Lp - 1:Lp, :] for c in n]
        g_end = [b_last[c] - bcol[c] + ig[c] for c in n]
        m_new = [jnp.maximum(b_last[c] + m_st[c], g_end[c].max(axis=0, keepdims=True)) for c in n]
        a_prev = [jnp.exp(b_last[c] + m_st[c] - m_new[c]) for c in n]
        kw = [k[c] * jnp.exp(g_end[c] - m_new[c]) for c in n]
        k_v = [_dot_tn(kw[c].astype(BF16), vb[c]) for c in n]
        new_states = []
        for c, (bi, h) in enumerate(ch):
            num = w_inter[c] * q_c[c] + qk_v[c]
            den = (w_inter[c] * jnp.sum(q[c] * n_st[c], axis=-1, keepdims=True)
                   + jnp.sum(qk[c], axis=-1, keepdims=True))
            hout = num / jnp.maximum(jnp.abs(den), jnp.exp(-m_t[c]))
            new_states.append((a_prev[c] * c_st[c] + k_v[c],
                               a_prev[c] * n_st[c] + jnp.sum(kw[c], axis=0, keepdims=True),
                               jnp.broadcast_to(m_new[c], (1, HD))))
            hn = _rms_lanes(hout[0:L], hg)
            o_ref[bi, :, sls[c]] = hn * _sigmoid(og_ref[bi, :, sls[c]])
        for (bi, h), (c_new, n_new, m_nw) in zip(ch, new_states):
            c_ref[bi, h] = c_new
            n_ref[bi, h:h + 1, :] = n_new
            m_ref[bi, h:h + 1, :] = m_nw

    unroll = min(BATCH_UNROLL, bb)
    wide = min(wide_chains, unroll)

    def group(g, carry):
        for u0 in range(0, unroll, wide):
            run_chains([g * unroll + u0 + u for u in range(wide)])
        return carry

    if bb == unroll:
        group(0, 0)
    else:
        lax.fori_loop(0, bb // unroll, group, 0)


def _mlstm(h3d, so3d, hg, c0, n0, m0, state_layer, layer, depth, prev, bb, chunk):
    b, t, _ = h3d.shape
    nc = t // chunk
    wide_chains = 4 if chunk >= 64 else 2

    def col(off):
        blk = off // MIX_W
        return pl.BlockSpec((bb, chunk, MIX_W), lambda bi, c, blk=blk: (bi, c, blk))

    def st_c(li):
        return pl.BlockSpec((None, bb, N_HEADS, HD, HD), lambda bi, c: (li, bi, 0, 0, 0))

    def st_n(li):
        return pl.BlockSpec((None, bb, N_HEADS, HD), lambda bi, c: (li, bi, 0, 0))

    c_specs, c_args, aliases = _carry_args(prev, N_MLSTM_IN, (1, 2, 3))
    return pl.pallas_call(
        _with_carried_outputs(functools.partial(_mlstm_kernel, bb=bb, chunk=chunk, wide_chains=wide_chains),
                              N_MLSTM_IN, len(c_args)),
        grid=(b // bb, nc),
        in_specs=[col(COL_BQ), col(COL_BK), col(COL_BV), col(COL_BO),
                  pl.BlockSpec((bb, chunk, 128), lambda bi, c: (bi, c, 0)),
                  pl.BlockSpec((1, HD), lambda bi, c: (0, 0)),
                  st_c(state_layer), st_n(state_layer), st_n(state_layer)] + c_specs,
        out_specs=(pl.BlockSpec((bb, chunk, MIX_W), lambda bi, c: (bi, c, 0)),
                   st_c(layer), st_n(layer), st_n(layer)),
        out_shape=(jax.ShapeDtypeStruct((b, t, MIX_W), F32),
                   jax.ShapeDtypeStruct((depth, b, N_HEADS, HD, HD), F32),
                   jax.ShapeDtypeStruct((depth, b, N_HEADS, HD), F32),
                   jax.ShapeDtypeStruct((depth, b, N_HEADS, HD), F32)),
        input_output_aliases=aliases,
        compiler_params=_cparams(("parallel", "arbitrary"), 40),
        name="mlstm",
    )(h3d, h3d, h3d, h3d, so3d, hg.reshape(1, HD), c0, n0, m0, *c_args)


N_GLA_IN = 8
GLA_SUB = 16


def _gla_kernel(q_ref, k_ref, v_ref, so_ref, w2_ref, b2_ref, hg_ref, s0_ref, o_ref, s_ref, *, bb, chunk,
                wide_seqs):
    L = chunk
    Lp = max(L, BF16_ROWS)
    sub = GLA_SUB

    @pl.when(pl.program_id(1) == 0)
    def _():
        s_ref[...] = s0_ref[...]

    tril = _tril_ones(Lp)
    hg = hg_ref[...]
    w2 = w2_ref[...].astype(BF16)
    b2 = b2_ref[...]
    q_scale = GLA_DK ** -0.5
    real_row = lax.broadcasted_iota(jnp.int32, (Lp, 1), 0) < L

    heads = range(N_HEADS)
    dkw = N_HEADS * GLA_DK
    head_of_lane = lax.broadcasted_iota(jnp.int32, (1, dkw), 1) >> GLA_DK_SHIFT
    zero_blk = jnp.zeros((GLA_DK, GLA_DV), F32)

    def run_seqs(idx):
        n = range(len(idx))
        s_old = [[s_ref[bi, h] for h in heads] for bi in idx]
        pre = [_dot(_pad_rows(so_ref[bi], Lp).astype(BF16), w2) + b2 for bi in idx]
        la = [jnp.where(real_row, _log_sigmoid(pre[u]) / GLA_TAU, 0.0) for u in n]
        bcs = [_dot_exact(tril, la[u]) for u in n]
        q = [_pad_rows(q_ref[bi], Lp) * q_scale for bi in idx]
        k = [_pad_rows(k_ref[bi], Lp) for bi in idx]
        vb = [_pad_rows(v_ref[bi], Lp).astype(BF16) for bi in idx]
        s_bd = [jnp.concatenate(
            [jnp.concatenate([s_old[u][h] if g == h else zero_blk for g in heads], axis=1) for h in heads],
            axis=0) for u in n]
        inter = [_dot((q[u] * jnp.exp(bcs[u])).astype(BF16), s_bd[u].astype(BF16)) for u in n]
        for i in range(Lp // sub):
            r0 = i * sub
            hi = r0 + sub
            rows = min(hi, L) - r0
            rr = (lax.broadcasted_iota(jnp.int32, (N_HEADS * sub, hi), 0) & (sub - 1)) + r0
            cc = lax.broadcasted_iota(jnp.int32, (N_HEADS * sub, hi), 1)
            base = [bcs[u][r0 - 1:r0, :] if i > 0 else jnp.zeros((1, dkw), F32) for u in n]
            qi = [q[u][r0:hi, :] * jnp.exp(bcs[u][r0:hi, :] - base[u]) for u in n]
            ke = [(k[u][0:hi, :] * jnp.exp(base[u] - bcs[u][0:hi, :])).astype(BF16) for u in n]
            q_heads = [jnp.concatenate([jnp.where(head_of_lane == h, qi[u], 0.0) for h in heads], axis=0)
                       for u in n]
            a = [jnp.where(cc <= rr, _dot_nt(q_heads[u].astype(BF16), ke[u]), 0.0).astype(BF16) for u in n]
            for u, bi in enumerate(idx):
                for h in heads:
                    vsl = slice(h * GLA_DV, (h + 1) * GLA_DV)
                    oi = inter[u][r0:hi, vsl] + _dot(a[u][h * sub:(h + 1) * sub, :], vb[u][0:hi, vsl])
                    o_ref[bi, r0:r0 + rows, vsl] = _rms_lanes(oi[0:rows], hg)
        b_end = [bcs[u][Lp - 1:Lp, :] for u in n]
        kd = [(k[u] * jnp.exp(b_end[u] - bcs[u])).astype(BF16) for u in n]
        upd = [_dot_tn(kd[u], vb[u]) for u in n]
        dcol = [_row_to_col(jnp.exp(b_end[u])) for u in n]
        for u, bi in enumerate(idx):
            for h in heads:
                s_ref[bi, h] = (dcol[u][h * GLA_DK:(h + 1) * GLA_DK, :] * s_old[u][h]
                                + upd[u][h * GLA_DK:(h + 1) * GLA_DK, h * GLA_DV:(h + 1) * GLA_DV])

    unroll = min(BATCH_UNROLL, bb)
    wide = min(wide_seqs, unroll)

    def group(g, carry):
        for u0 in range(0, unroll, wide):
            run_seqs([g * unroll + u0 + u for u in range(wide)])
        return carry

    if bb == unroll:
        group(0, 0)
    else:
        lax.fori_loop(0, bb // unroll, group, 0)


def _gla(h3d, so3d, w2pad, b2, hg, s0, state_layer, layer, depth, prev, bb, chunk):
    b, t, _ = h3d.shape
    nc = t // chunk

    def st(li):
        return pl.BlockSpec((None, bb, N_HEADS, GLA_DK, GLA_DV), lambda bi, c: (li, bi, 0, 0, 0))

    c_specs, c_args, aliases = _carry_args(prev, N_GLA_IN, (1,))
    return pl.pallas_call(
        _with_carried_outputs(functools.partial(_gla_kernel, bb=bb, chunk=chunk, wide_seqs=4),
                              N_GLA_IN, len(c_args)),
        grid=(b // bb, nc),
        in_specs=[pl.BlockSpec((bb, chunk, 256), lambda bi, c: (bi, c, COL_CQ // 256)),
                  pl.BlockSpec((bb, chunk, 256), lambda bi, c: (bi, c, COL_CK // 256)),
                  pl.BlockSpec((bb, chunk, MIX_W), lambda bi, c: (bi, c, COL_CV // MIX_W)),
                  pl.BlockSpec((bb, chunk, 128), lambda bi, c: (bi, c, 0)),
                  pl.BlockSpec((128, 256), lambda bi, c: (0, 0)),
                  pl.BlockSpec((1, 256), lambda bi, c: (0, 0)),
                  pl.BlockSpec((1, GLA_DV), lambda bi, c: (0, 0)),
                  st(state_layer)] + c_specs,
        out_specs=(pl.BlockSpec((bb, chunk, MIX_W), lambda bi, c: (bi, c, 0)), st(layer)),
        out_shape=(jax.ShapeDtypeStruct((b, t, MIX_W), F32),
                   jax.ShapeDtypeStruct((depth, b, N_HEADS, GLA_DK, GLA_DV), F32)),
        input_output_aliases=aliases,
        compiler_params=_cparams(("parallel", "arbitrary"), 40),
        name="gla",
    )(h3d, h3d, h3d, so3d, w2pad, b2.reshape(1, 256), hg.reshape(1, GLA_DV), s0, *c_args)


def _select_blocks(imp, qpos_col, ns):
    r = imp.shape[0]
    j = lax.broadcasted_iota(jnp.int32, (r, 128), 1)
    cur = qpos_col >> SEL_SHIFT
    valid = j <= cur
    forced = (j == 0) | (valid & (j >= cur - 1))
    val = jnp.where(forced, 1e4, jnp.where(valid, imp, -1e4))
    val = jnp.where(j < ns, val, -3e38)
    rank = jnp.zeros((r, 128), F32)
    for i in range(ns):
        ci = val[:, i:i + 1]
        beats = (ci > val) | ((ci == val) & (j > i))
        rank = rank + jnp.where(beats, 1.0, 0.0)
    return (rank < float(min(N_SEL, ns))) & (j < ns)


def _pair_matrix(nc):
    c = jnp.arange(nc)[:, None]
    j = jnp.arange(128)[None, :]
    return (c // (SEL_BLOCK // CMP_BLOCK) == j).astype(F32)


def _expand_matrix(tk):
    j = jnp.arange(128)[:, None]
    s = jnp.arange(tk)[None, :]
    return (s // SEL_BLOCK == j).astype(BF16)


NSA_TQ = 128


def _nsa_prompt_kernel(q_ref, kcmp_ref, vcmp_ref, kv_ref, win_ref, so_ref, e_ref, pair_ref, o_ref, *, t_len):
    tq = NSA_TQ
    nc = t_len // CMP_BLOCK
    ns = -(-t_len // SEL_BLOCK)
    i = pl.program_id(1)
    start = pl.multiple_of(i * tq, tq)
    qpos_col = start + lax.broadcasted_iota(jnp.int32, (tq, 1), 0)
    so = so_ref[...]
    q_all = jnp.concatenate([q_ref[:, h * HD:(h + 1) * HD] for h in range(N_HEADS)], axis=0)
    sc = _dot_nt(q_all, kcmp_ref[...].astype(BF16)) * ATT_SCALE
    qpos4 = start + (lax.broadcasted_iota(jnp.int32, (N_HEADS * tq, 1), 0) & (tq - 1))
    cend = lax.broadcasted_iota(jnp.int32, (N_HEADS * tq, nc), 1) * CMP_BLOCK + (CMP_BLOCK - 1)
    (ec,), invc = _softmax_parts([sc], [cend <= qpos4])
    pc = ec * invc
    o_cmp = _dot(pc.astype(BF16), vcmp_ref[...].astype(BF16))
    imp_c = pc[0:tq]
    for h in range(1, N_HEADS):
        imp_c = imp_c + pc[h * tq:(h + 1) * tq]
    imp = _dot_exact(imp_c, pair_ref[...])
    sel_b = jnp.where(_select_blocks(imp, qpos_col, ns), 1.0, 0.0).astype(BF16)
    band = WINDOW + tq
    kwin = win_ref[pl.ds(start, band), 0:HD]
    vwin = win_ref[pl.ds(start, band), HD:2 * HD]
    wpos = start - WINDOW + lax.broadcasted_iota(jnp.int32, (tq, band), 1)
    wmask = (wpos <= qpos_col) & (wpos > qpos_col - WINDOW) & (wpos >= 0)
    for h in range(N_HEADS):
        sl = slice(h * HD, (h + 1) * HD)
        (ew,), invw = _softmax_parts([_dot_nt(q_ref[:, sl], kwin) * ATT_SCALE], [wmask])
        g0 = so[:, SM_DG + 3 * h:SM_DG + 3 * h + 1]
        g2 = so[:, SM_DG + 3 * h + 2:SM_DG + 3 * h + 3]
        o_ref[:, sl] = g0 * o_cmp[h * tq:(h + 1) * tq] + g2 * (_dot(ew.astype(BF16), vwin) * invw)

    n_cls = -(-t_len // KEY_CLASS)
    cls = (start + tq - 1) // KEY_CLASS

    def run_selected(kl):
        sel_keys = _dot(sel_b, e_ref[:, 0:kl]) > 0.5
        kpos = lax.broadcasted_iota(jnp.int32, (tq, kl), 1)
        smask = sel_keys & (kpos <= qpos_col)
        ks = kv_ref[0:kl, 2 * HD:3 * HD]
        vs = kv_ref[0:kl, 3 * HD:4 * HD]
        for h in range(N_HEADS):
            sl = slice(h * HD, (h + 1) * HD)
            (es,), invs = _softmax_parts([_dot_nt(q_ref[:, sl], ks) * ATT_SCALE], [smask])
            g1 = so[:, SM_DG + 3 * h + 1:SM_DG + 3 * h + 2]
            o_ref[:, sl] = o_ref[:, sl] + g1 * (_dot(es.astype(BF16), vs) * invs)

    for c in range(n_cls):
        pl.when(cls == c)(functools.partial(run_selected, min((c + 1) * KEY_CLASS, t_len)))


def _nsa_prompt(nq, kcmp, vcmp, nkv, win_pad, so):
    b, t, _ = nq.shape
    tq = NSA_TQ
    nc = t // CMP_BLOCK
    e_mat = _expand_matrix(t)
    pair = _pair_matrix(nc)
    return pl.pallas_call(
        functools.partial(_nsa_prompt_kernel, t_len=t),
        grid=(b, t // tq),
        in_specs=[pl.BlockSpec((None, tq, MIX_W), lambda bi, i: (bi, i, 0)),
                  pl.BlockSpec((None, nc, HD), lambda bi, i: (bi, 0, 0)),
                  pl.BlockSpec((None, nc, HD), lambda bi, i: (bi, 0, 0)),
                  pl.BlockSpec((None, t, 768), lambda bi, i: (bi, 0, 0)),
                  pl.BlockSpec((None, t + WINDOW, 256), lambda bi, i: (bi, 0, 0)),
                  pl.BlockSpec((None, tq, 128), lambda bi, i: (bi, i, 0)),
                  pl.BlockSpec((128, t), lambda bi, i: (0, 0)),
                  pl.BlockSpec((nc, 128), lambda bi, i: (0, 0))],
        out_specs=pl.BlockSpec((None, tq, MIX_W), lambda bi, i: (bi, i, 0)),
        out_shape=jax.ShapeDtypeStruct((b, t, MIX_W), F32),
        compiler_params=_cparams(("parallel", "arbitrary"), 48),
        name="nsa_prompt",
    )(nq, kcmp, vcmp, nkv, win_pad, so, e_mat, pair)


def _nsa_sample_stages(q_ref, kvn_ref, winn_ref, so_ref, win_ref, wk_ref, wv_ref, kcg_ref, e_ref, pair_ref,
                       pages, o_ref, wout_ref, cmp_k, cmp_v, n_pages, t_new):
    past = n_pages * PAGE_SIZE
    nc = past // CMP_BLOCK
    ns = -(-(past + t_new) // SEL_BLOCK)
    per_page = PAGE_SIZE // CMP_BLOCK
    n_grp = n_pages // PAGE_GROUP
    gw = PAGE_GROUP * PAGE_SIZE
    rows = N_HEADS * t_new
    st = {}

    def page_rows(p, j):
        return pages[p][pl.ds(j, PAGE_SIZE, stride=4), :]

    def group_rows(g, j):
        return jnp.concatenate([page_rows(g * PAGE_GROUP + u, j).astype(BF16) for u in range(PAGE_GROUP)], axis=0)

    def compress():
        for p in range(n_pages):
            kc = page_rows(p, 0).reshape(per_page, CMP_BLOCK, HD)
            vc = page_rows(p, 1).reshape(per_page, CMP_BLOCK, HD)
            cmp_k[p * per_page:(p + 1) * per_page, :] = jnp.sum(kc * wk_ref[...][None], axis=1)
            cmp_v[p * per_page:(p + 1) * per_page, :] = jnp.sum(vc * wv_ref[...][None], axis=1)
        kcmp = _rms_lanes(cmp_k[...], kcg_ref[...]).astype(BF16)
        st["vcmp"] = cmp_v[...].astype(BF16)
        q_all = jnp.concatenate([q_ref[:, h * HD:(h + 1) * HD].astype(F32) for h in range(N_HEADS)], axis=0)
        st["q_all"] = q_all
        st["q_b"] = q_all.astype(BF16)
        trow = lax.broadcasted_iota(jnp.int32, (rows, 1), 0) & (t_new - 1)
        st["qpos4"] = past + trow
        st["kvn"] = kvn_ref[...].astype(F32)
        st["npos"] = past + lax.broadcasted_iota(jnp.int32, (rows, t_new), 1)
        st["causal_new"] = st["npos"] <= st["qpos4"]
        st["sc"] = _dot_nt(st["q_b"], kcmp) * ATT_SCALE

    def selected_scores():
        s_list = [_dot_nt(st["q_b"], group_rows(g, 2)) * ATT_SCALE for g in range(n_grp)]
        s_list.append(_dot_nt(st["q_all"], st["kvn"][:, 2 * HD:3 * HD]) * ATT_SCALE)
        st["s_list"] = s_list

    def compressed():
        cend = lax.broadcasted_iota(jnp.int32, (rows, nc), 1) * CMP_BLOCK + (CMP_BLOCK - 1)
        (ec,), invc = _softmax_parts([st["sc"]], [cend <= st["qpos4"]])
        pc = ec * invc
        imp_c = pc[0:t_new]
        for h in range(1, N_HEADS):
            imp_c = imp_c + pc[h * t_new:(h + 1) * t_new]
        st["imp"] = _dot_exact(imp_c, pair_ref[...])
        st["o_cmp"] = _dot(pc.astype(BF16), st["vcmp"])

    def window():
        qpos4, npos, kvn = st["qpos4"], st["npos"], st["kvn"]
        w_buf = win_ref.shape[0] // 2
        kwb = win_ref[pl.ds(0, w_buf, stride=2), :].astype(BF16)
        vwb = win_ref[pl.ds(1, w_buf, stride=2), :].astype(BF16)
        wpos = past - w_buf + lax.broadcasted_iota(jnp.int32, (rows, w_buf), 1)
        wmask = (wpos <= qpos4) & (wpos > qpos4 - WINDOW) & (wpos >= 0)
        sw_list = [_dot_nt(st["q_b"], kwb) * ATT_SCALE, _dot_nt(st["q_all"], kvn[:, 4 * HD:5 * HD]) * ATT_SCALE]
        ew, invw = _softmax_parts(sw_list, [wmask, st["causal_new"] & (npos > qpos4 - WINDOW)])
        st["o_win"] = (_dot(ew[0].astype(BF16), vwb) + _dot(ew[1], kvn[:, 5 * HD:6 * HD])) * invw
        keep = 2 * (w_buf - t_new)
        wout_ref[0:keep, :] = win_ref[2 * t_new:2 * w_buf, :]
        wout_ref[keep:2 * w_buf, :] = winn_ref[...]

    def select():
        qpos_col = past + lax.broadcasted_iota(jnp.int32, (t_new, 1), 0)
        sel = _select_blocks(st["imp"], qpos_col, ns)
        self32 = jnp.where(sel, 1.0, 0.0)
        sel_past = _dot(self32.astype(BF16), e_ref[...])
        sel_past4 = jnp.concatenate([sel_past] * N_HEADS, axis=0) > 0.5
        new_blk = past // SEL_BLOCK
        sel_new = jnp.concatenate([self32[:, new_blk:new_blk + 1]] * N_HEADS, axis=0) > 0.5
        m_list = []
        for g in range(n_grp):
            kpos = g * gw + lax.broadcasted_iota(jnp.int32, (rows, gw), 1)
            m_list.append(sel_past4[:, g * gw:(g + 1) * gw] & (kpos <= st["qpos4"]))
        m_list.append(sel_new & st["causal_new"])
        st["m_list"] = m_list

    def selected():
        es, invs = _softmax_parts(st["s_list"], st["m_list"])
        o_sel = _dot(es[n_grp], st["kvn"][:, 3 * HD:4 * HD])
        for g in range(n_grp):
            o_sel = o_sel + _dot(es[g].astype(BF16), group_rows(g, 3))
        o_sel = o_sel * invs
        so = so_ref[...]
        for h in range(N_HEADS):
            r = slice(h * t_new, (h + 1) * t_new)
            g0 = so[:, SM_DG + 3 * h:SM_DG + 3 * h + 1]
            g1 = so[:, SM_DG + 3 * h + 1:SM_DG + 3 * h + 2]
            g2 = so[:, SM_DG + 3 * h + 2:SM_DG + 3 * h + 3]
            o_ref[:, h * HD:(h + 1) * HD] = g0 * st["o_cmp"][r] + g1 * o_sel[r] + g2 * st["o_win"][r]

    return [compress, selected_scores, compressed, window, select, selected]


N_SAMPLE_IN = 15


def _sample_attn_kernel(pt_ref, fq_ref, fk_ref, fv_ref, so_ref, spread_ref, nq_ref, nkv_ref, winn_ref,
                        win_ref, wk_ref, wv_ref, kcg_ref, e_ref, pair_ref, *rest, n_pages, t_new):
    del pt_ref
    kv_pages = rest[:n_pages]
    lf_pages = rest[n_pages:2 * n_pages]
    nsa_pages = rest[2 * n_pages:3 * n_pages]
    o_fox, o_nsa, wout_ref, lf_scr, cmp_k, cmp_v = rest[3 * n_pages:3 * n_pages + 6]
    fox = _fox_sample_stages(fq_ref, fk_ref, fv_ref, so_ref, spread_ref, kv_pages, lf_pages, o_fox, lf_scr,
                             n_pages, t_new)
    nsa = _nsa_sample_stages(nq_ref, nkv_ref, winn_ref, so_ref, win_ref, wk_ref, wv_ref, kcg_ref, e_ref,
                             pair_ref, nsa_pages, o_nsa, wout_ref, cmp_k, cmp_v, n_pages, t_new)
    for n_stage, f_stage in zip(nsa, fox):
        n_stage()
        f_stage()


def _sample_attn(layer, depth, prev, page_table, fq, fk, fv, nq, nkv, win_new, so, fox_kv4, fox_lf4, nsa_kv4,
                 win_state4, wk_b, wv_b, kcg):
    b, t, _ = nq.shape
    n_pages = page_table.shape[1]
    past = n_pages * PAGE_SIZE
    nc = past // CMP_BLOCK
    w2 = win_state4.shape[2]
    e_mat = _expand_matrix(past)
    pair = _pair_matrix(nc)

    def tok(rows, width):
        return pl.BlockSpec((None, rows, width), lambda bi, pt: (bi, 0, 0))

    def const(shape):
        return pl.BlockSpec(shape, lambda bi, pt: (0, 0))

    def page_spec(rows, p):
        return pl.BlockSpec((None, None, rows, 128), lambda bi, pt, p=p: (layer, pt[bi, p], 0, 0))

    in_specs = [tok(t, MIX_W), tok(t, MIX_W), tok(t, MIX_W), tok(t, 128), const((PAGE_SIZE, 2 * PAGE_SIZE)),
                tok(t, MIX_W), tok(t, 768), tok(2 * t, HD),
                pl.BlockSpec((None, None, w2, 128), lambda bi, pt: (layer, bi, 0, 0)),
                const((CMP_BLOCK, 128)), const((CMP_BLOCK, 128)), const((1, 128)),
                const((128, past)), const((nc, 128))]
    in_specs += [page_spec(PAGE_SIZE * 2 * N_HEADS, p) for p in range(n_pages)]
    in_specs += [page_spec(N_HEADS, p) for p in range(n_pages)]
    in_specs += [page_spec(PAGE_SIZE * 4, p) for p in range(n_pages)]
    n_in = N_SAMPLE_IN + 3 * n_pages
    c_specs, c_args, aliases = _carry_args(prev, n_in, (2,))
    out_tok = pl.BlockSpec((None, t, MIX_W), lambda bi, pt: (bi, 0, 0))
    grid_spec = pltpu.PrefetchScalarGridSpec(
        num_scalar_prefetch=1, grid=(b,), in_specs=in_specs + c_specs,
        out_specs=(out_tok, out_tok, pl.BlockSpec((None, None, w2, 128), lambda bi, pt: (layer, bi, 0, 0))),
        scratch_shapes=[pltpu.VMEM((n_pages * N_HEADS, PAGE_SIZE), F32),
                        pltpu.VMEM((nc, HD), F32), pltpu.VMEM((nc, HD), F32)])
    return pl.pallas_call(
        _with_carried_outputs(functools.partial(_sample_attn_kernel, n_pages=n_pages, t_new=t), n_in, len(c_args)),
        grid_spec=grid_spec,
        out_shape=(jax.ShapeDtypeStruct((b, t, MIX_W), F32), jax.ShapeDtypeStruct((b, t, MIX_W), F32),
                   jax.ShapeDtypeStruct((depth, b, w2, 128), F32)),
        input_output_aliases=aliases,
        compiler_params=_cparams(("arbitrary",), 56),
        name="sample_attn",
    )(page_table, fq, fk, fv, so, _spread_matrix(), nq, nkv, win_new, win_state4, wk_b, wv_b, kcg, e_mat, pair,
      *([fox_kv4] * n_pages), *([fox_lf4] * n_pages), *([nsa_kv4] * n_pages), *c_args)


OUT_TM = 256


def _out_kernel(x_ref, oa_ref, ob_ref, oc_ref, od_ref, z_ref, g0_ref, g1_ref, g2_ref, g3_ref,
                wb_ref, wo_ref, y_ref):
    acc = None
    for g, (o_r, g_r) in enumerate(((oa_ref, g0_ref), (ob_ref, g1_ref), (oc_ref, g2_ref), (od_ref, g3_ref))):
        z = z_ref[:, g * MIX_W:(g + 1) * MIX_W]
        br = (o_r[...] * (z * _sigmoid(z))).astype(BF16)
        term = (jnp.tanh(g_r[...]) + 1.0) * _dot(br, wb_ref[g])
        acc = term if acc is None else acc + term
    y_ref[...] = x_ref[...] + _dot(acc.astype(BF16), wo_ref[...])


def _out_proj(x2d, oa, ob, oc, od, h2d, wb_all, wo_all, layer):
    m = x2d.shape[0]
    tm = OUT_TM

    def rows(width):
        return pl.BlockSpec((tm, width), lambda i: (i, 0))

    def gate(g):
        return pl.BlockSpec((tm, D_MODEL), lambda i, g=g: (i, COL_GATE // D_MODEL + g))

    single = pl.Buffered(1)
    return pl.pallas_call(
        _out_kernel,
        grid=(m // tm,),
        in_specs=[rows(D_MODEL), rows(MIX_W), rows(MIX_W), rows(MIX_W), rows(MIX_W),
                  rows(D_MODEL), gate(0), gate(1), gate(2), gate(3),
                  pl.BlockSpec((None, N_HEADS, MIX_W, D_MODEL), lambda i: (layer, 0, 0, 0), pipeline_mode=single),
                  pl.BlockSpec((None, D_MODEL, D_MODEL), lambda i: (layer, 0, 0), pipeline_mode=single)],
        out_specs=rows(D_MODEL),
        out_shape=jax.ShapeDtypeStruct((m, D_MODEL), F32),
        compiler_params=_cparams(("parallel",), 56),
        name="out_proj",
    )(x2d, oa, ob, oc, od, h2d, h2d, h2d, h2d, h2d, wb_all, wo_all)


def _layer_params(l, p):
    zeros = jnp.zeros((2, HD), F32)
    gains = jnp.concatenate([p["fox_qg"][l][None], p["fox_kg"][l][None], p["nsa_qg"][l][None],
                             p["nsa_ksg"][l][None], p["nsa_kwg"][l][None], p["nsa_kcg"][l][None], zeros], axis=0)
    bias = jnp.concatenate([p["fox_bf"][l], p["mlstm_bi"][l], p["mlstm_bf"][l], jnp.zeros((GLA_RANK,), F32),
                            p["nsa_bg"][l].reshape(-1), jnp.zeros((128 - SM_DG - 12,), F32)]).reshape(1, 128)
    w2pad = jnp.zeros((128, 256), F32).at[SM_CA:SM_CA + GLA_RANK].set(p["gla_w2"][l])
    return dict(
        norm_g=p["norm_g"][l], gains=gains, bias=bias,
        wk_b=jnp.broadcast_to(p["nsa_wk"][l][:, None], (CMP_BLOCK, 128)),
        wv_b=jnp.broadcast_to(p["nsa_wv"][l][:, None], (CMP_BLOCK, 128)),
        kcg=p["nsa_kcg"][l].reshape(1, HD),
        mlstm_hg=p["mlstm_hg"][l], w2pad=w2pad, gla_b2=p["gla_b2"][l], gla_hg=p["gla_hg"][l])


def _layer(x3d, lp, shared, st, prev, *, is_prompt, layer, depth):
    b, t, _ = x3d.shape
    m = b * t
    x2d = x3d.reshape(m, D_MODEL)
    xn = _rmsnorm(x2d, lp["norm_g"])
    h2d = _in_proj(xn, shared["w_in"], layer)
    h3d = h2d.reshape(b, t, N_PROJ)
    (foxkv, nsakv, win, so, fq, fk, fv, nq, nkv, kcmp, vcmp) = _prep(
        h2d, lp["gains"], lp["bias"], lp["wk_b"], lp["wv_b"], layer, depth,
        None if prev is None else (prev["fox_kv"], prev["nsa_kv"]))
    so3 = so.reshape(b, t, 128)
    fq3, fk3, fv3 = fq.reshape(b, t, MIX_W), fk.reshape(b, t, MIX_W), fv.reshape(b, t, MIX_W)
    nq3 = nq.reshape(b, t, MIX_W)
    nkv3 = nkv.reshape(b, t, 768)
    chunk = min(64, t)
    carried = dict(fox_kv=foxkv, nsa_kv=nsakv)
    if is_prompt:
        lf_t = so3[:, :, SM_AF:SM_AF + N_HEADS].transpose(0, 2, 1).reshape(b * N_HEADS, t)
        f_row = _cumsum_rows(lf_t).reshape(b, N_HEADS, t)
        f_col = f_row.transpose(0, 2, 1)
        o_a = _fox_prompt(fq3, fk3, fv3, f_col, f_row)
        win_pad = jnp.pad(nkv3[:, :, 4 * HD:], ((0, 0), (WINDOW, 0), (0, 0)))
        o_d = _nsa_prompt(nq3, kcmp.reshape(b, t // CMP_BLOCK, HD), vcmp.reshape(b, t // CMP_BLOCK, HD),
                          nkv3, win_pad, so3)
        win_out = win.reshape(b, t, 2, HD)[:, -min(WINDOW, t):]
        bb = min(BATCH_UNROLL, b)
    else:
        o_a, o_d, win_out = _sample_attn(
            layer, depth, None if prev is None else (prev["win"],), shared["page_table"],
            fq3, fk3, fv3, nq3, nkv3, win.reshape(b, 2 * t, HD), so3,
            shared["fox_kv4"], shared["fox_lf4"], shared["nsa_kv4"], shared["nsa_win4"],
            lp["wk_b"], lp["wv_b"], lp["kcg"])
        bb = 2 * BATCH_UNROLL
    o_b, c_new, n_new, m_new = _mlstm(
        h3d, so3, lp["mlstm_hg"], st["C"], st["n"], st["m"], st["layer"], layer, depth,
        None if prev is None else (prev["C"], prev["n"], prev["m"]), bb, chunk)
    o_c, s_new = _gla(h3d, so3, lp["w2pad"], lp["gla_b2"], lp["gla_hg"], st["S"], st["layer"], layer, depth,
                      None if prev is None else (prev["S"],), bb, chunk)
    y2d = _out_proj(x2d, o_a.reshape(m, MIX_W), o_b.reshape(m, MIX_W), o_c.reshape(m, MIX_W),
                    o_d.reshape(m, MIX_W), h2d, shared["w_branch"], shared["w_out"], layer)
    carried.update(win=win_out, C=c_new, n=n_new, m=m_new, S=s_new)
    return y2d.reshape(b, t, D_MODEL), carried, so3[:, :, SM_AF:SM_AF + N_HEADS]


def kernel(x_prompt, x_sample, cache_fox_kv, cache_fox_logf, cache_nsa_kv, page_table, state_nsa_win, state_mlstm_C, state_mlstm_n, state_mlstm_m, state_gla_S, norm_g, w_in, fox_qg, fox_kg, fox_bf, mlstm_bi, mlstm_bf, mlstm_hg, gla_w2, gla_b2, gla_hg, nsa_qg, nsa_kcg, nsa_ksg, nsa_kwg, nsa_wk, nsa_wv, nsa_bg, w_branch, w_out):
    params = dict(norm_g=norm_g, fox_qg=fox_qg, fox_kg=fox_kg, fox_bf=fox_bf, mlstm_bi=mlstm_bi,
                  mlstm_bf=mlstm_bf, mlstm_hg=mlstm_hg, gla_w2=gla_w2, gla_b2=gla_b2, gla_hg=gla_hg,
                  nsa_qg=nsa_qg, nsa_kcg=nsa_kcg, nsa_ksg=nsa_ksg, nsa_kwg=nsa_kwg, nsa_wk=nsa_wk,
                  nsa_wv=nsa_wv, nsa_bg=nsa_bg)
    depth = w_in.shape[0]
    bp, tp, _ = x_prompt.shape
    db, ts, _ = x_sample.shape
    n_pool = cache_fox_kv.shape[1]
    w_buf = state_nsa_win.shape[2]
    shared = dict(
        w_in=_permute_w_in(w_in), w_branch=(0.5 * w_branch).astype(BF16), w_out=w_out.astype(BF16),
        page_table=page_table,
        fox_kv4=cache_fox_kv.reshape(depth, n_pool, PAGE_SIZE * 2 * N_HEADS, HD),
        fox_lf4=cache_fox_logf.transpose(0, 1, 3, 2),
        nsa_kv4=cache_nsa_kv.reshape(depth, n_pool, PAGE_SIZE * 4, HD),
        nsa_win4=state_nsa_win.reshape(depth, db, w_buf * 2, HD))
    st_p = dict(C=jnp.zeros((1, bp, N_HEADS, HD, HD), F32), n=jnp.zeros((1, bp, N_HEADS, HD), F32),
                m=jnp.zeros((1, bp, N_HEADS, HD), F32), S=jnp.zeros((1, bp, N_HEADS, GLA_DK, GLA_DV), F32),
                layer=0)
    m_s = jnp.broadcast_to(state_mlstm_m[..., None], state_mlstm_m.shape + (HD,))
    y_p, y_s = x_prompt, x_sample
    prev_p = prev_s = None
    lf_p, lf_s, win_p = [], [], []
    for l in range(depth):
        lp = _layer_params(l, params)
        st_s = dict(C=state_mlstm_C, n=state_mlstm_n, m=m_s, S=state_gla_S, layer=l)
        y_p, prev_p, lf = _layer(y_p, lp, shared, st_p, prev_p, is_prompt=True, layer=l, depth=depth)
        lf_p.append(lf)
        win_p.append(prev_p["win"])
        y_s, prev_s, lf = _layer(y_s, lp, shared, st_s, prev_s, is_prompt=False, layer=l, depth=depth)
        lf_s.append(lf)

    def finish(c, b, t, win, lf):
        return (c["fox_kv"].reshape(depth, b, t, 2, N_HEADS, HD), jnp.stack(lf, axis=0),
                c["nsa_kv"].reshape(depth, b, t, 4, HD), win,
                c["C"], c["n"], c["m"][..., 0], c["S"])

    out_p = finish(prev_p, bp, tp, jnp.stack(win_p, axis=0), lf_p)
    out_s = finish(prev_s, db, ts, prev_s["win"].reshape(depth, db, w_buf, 2, HD), lf_s)
    return (y_p, y_s) + out_p + out_s
```

```python
import functools

import jax
import jax.numpy as jnp
from jax import lax
from jax.experimental import pallas as pl
from jax.experimental.pallas import tpu as pltpu

F32 = jnp.float32
BF16 = jnp.bfloat16
HIGHEST = lax.Precision.HIGHEST

D_MODEL = 2048
MIX_W = 512
HD = 128
N_HEADS = 4
GLA_DK = 64
GLA_DK_SHIFT = 6
GLA_DV = 128
GLA_RANK = 16
GLA_TAU = 16.0
CMP_BLOCK = 32
SEL_BLOCK = 64
SEL_SHIFT = 6
N_SEL = 16
WINDOW = 512
PAGE_SIZE = 128
EPS = 1e-6
ATT_SCALE = HD ** -0.5
LOG2E = 1.4426950408889634
Q_PRESCALE = ATT_SCALE * LOG2E
NEG_INF = float("-inf")
KEY_CLASS = 512
BF16_ROWS = 16

_REF_LAYOUT = (
    ("a_q", 512), ("a_k", 512), ("a_v", 512), ("a_f", 4), ("a_z", 512),
    ("b_q", 512), ("b_k", 512), ("b_v", 512), ("b_i", 4), ("b_f", 4), ("b_o", 512), ("b_z", 512),
    ("c_q", 256), ("c_k", 256), ("c_v", 512), ("c_a", 16), ("c_z", 512),
    ("d_q", 512), ("d_kv", 768), ("d_g", 12), ("d_z", 512),
    ("gate", 8192),
)
N_REF = sum(w for _, w in _REF_LAYOUT)
_MY_LAYOUT = (
    "a_z", "b_z", "c_z", "d_z", "gate",
    "a_q", "a_k", "a_v", "d_q", "b_q", "b_k", "b_v", "b_o", "c_v", "c_q", "c_k", "d_kv",
    "a_f", "b_i", "b_f", "c_a", "d_g",
)
N_PROJ = 16384
COL_Z = 0
COL_GATE = 2048
COL_AQ, COL_AK, COL_AV, COL_DQ = 10240, 10752, 11264, 11776
COL_BQ, COL_BK, COL_BV, COL_BO = 12288, 12800, 13312, 13824
COL_CV, COL_CQ, COL_CK, COL_DKV, COL_SMALL = 14336, 14848, 15104, 15360, 16128
SM_AF, SM_BI, SM_BF, SM_CA, SM_DG = 0, 4, 8, 12, 28


def _cparams(sem, vmem_mb):
    return pltpu.CompilerParams(dimension_semantics=sem, vmem_limit_bytes=vmem_mb * 1024 * 1024)


def _log_sigmoid(x):
    return jnp.minimum(x, 0.0) - jnp.log(1.0 + jnp.exp(-jnp.abs(x)))


def _sigmoid(x):
    return 0.5 * jnp.tanh(0.5 * x) + 0.5


def _rms_lanes(x, g):
    return (x * lax.rsqrt(jnp.mean(x * x, axis=-1, keepdims=True) + EPS)) * g


def _dot(a, b):
    return jnp.dot(a, b, preferred_element_type=F32)


def _dot_nt(a, b):
    return lax.dot_general(a, b, (((1,), (1,)), ((), ())), preferred_element_type=F32)


def _dot_tn(a, b):
    return lax.dot_general(a, b, (((0,), (0,)), ((), ())), preferred_element_type=F32)


def _dot_exact(a, b):
    return jnp.dot(a, b, precision=HIGHEST, preferred_element_type=F32)


def _eye_mask(n):
    return lax.broadcasted_iota(jnp.int32, (n, n), 0) == lax.broadcasted_iota(jnp.int32, (n, n), 1)


def _col_to_row(c):
    n = c.shape[0]
    return jnp.sum(jnp.where(_eye_mask(n), c, 0.0), axis=0, keepdims=True)


def _row_to_col(r):
    n = r.shape[1]
    return jnp.sum(jnp.where(_eye_mask(n), r, 0.0), axis=1, keepdims=True)


def _tril_ones(n):
    r = lax.broadcasted_iota(jnp.int32, (n, n), 0)
    c = lax.broadcasted_iota(jnp.int32, (n, n), 1)
    return jnp.where(c <= r, 1.0, 0.0).astype(F32)


def _cumsum_lanes(x):
    n = x.shape[-1]
    lane = lax.broadcasted_iota(jnp.int32, x.shape, x.ndim - 1)
    s = 1
    while s < n:
        x = x + jnp.where(lane >= s, pltpu.roll(x, s, axis=x.ndim - 1), 0.0)
        s *= 2
    return x


def _pad_rows(x, rows):
    if x.shape[0] == rows:
        return x
    return jnp.concatenate([x, jnp.zeros((rows - x.shape[0], x.shape[1]), x.dtype)], axis=0)


def _softmax_parts(s_list, mask_list):
    masked = [jnp.where(mk, s, NEG_INF) for s, mk in zip(s_list, mask_list)]
    m = masked[0].max(axis=-1, keepdims=True)
    for s in masked[1:]:
        m = jnp.maximum(m, s.max(axis=-1, keepdims=True))
    m = jnp.where(m > NEG_INF, m, 0.0)
    es = [jnp.exp2(s - m) for s in masked]
    tot = es[0].sum(axis=-1, keepdims=True)
    for e in es[1:]:
        tot = tot + e.sum(axis=-1, keepdims=True)
    return es, 1.0 / jnp.maximum(tot, 1e-30)


def _with_carried_outputs(kernel_fn, n_in, n_carried):
    if not n_carried:
        return kernel_fn

    def wrapped(*refs):
        return kernel_fn(*refs[:n_in], *refs[n_in + n_carried:])

    return wrapped


def _carry_args(prev, first_in_index, out_indices):
    if prev is None:
        return [], [], {}
    specs = [pl.BlockSpec(memory_space=pl.ANY) for _ in prev]
    aliases = {first_in_index + k: oi for k, oi in enumerate(out_indices)}
    return specs, list(prev), aliases


def _segments():
    offs = {}
    off = 0
    for name, width in _REF_LAYOUT:
        offs[name] = (off, width)
        off += width
    segs = []
    dst = 0
    for name in _MY_LAYOUT:
        src, width = offs[name]
        segs.append((src, dst, width))
        dst += width
    return segs, dst


def _permute_kernel(wt_ref, o_ref, small_ref):
    segs, _ = _segments()
    for src, dst, width in segs:
        if width >= 128:
            w = wt_ref[src:src + width, :]
            if dst == COL_GATE:
                w = w * 0.5
            o_ref[dst:dst + width, :] = w.astype(BF16)
    small_ref[...] = jnp.zeros(small_ref.shape, F32)
    for src, dst, width in segs:
        if width < 128:
            small_ref[dst - COL_SMALL:dst - COL_SMALL + width, :] = wt_ref[src:src + width, :]
    o_ref[COL_SMALL:N_PROJ, :] = small_ref[...].astype(BF16)


def _permute_w_in(w_in):
    depth, d, n = w_in.shape
    tk = 128
    wt = jnp.transpose(w_in, (0, 2, 1))
    return pl.pallas_call(
        _permute_kernel,
        grid=(depth, d // tk),
        in_specs=[pl.BlockSpec((None, n, tk), lambda l, i: (l, 0, i))],
        out_specs=pl.BlockSpec((None, N_PROJ, tk), lambda l, i: (l, 0, i)),
        out_shape=jax.ShapeDtypeStruct((depth, N_PROJ, d), BF16),
        scratch_shapes=[pltpu.VMEM((N_PROJ - COL_SMALL, tk), F32)],
        compiler_params=_cparams(("parallel", "parallel"), 40),
        name="permute_w_in",
    )(wt)


def _norm_kernel(x_ref, g_ref, o_ref):
    o_ref[...] = _rms_lanes(x_ref[...], g_ref[...]).astype(o_ref.dtype)


def _rmsnorm(x2d, g):
    m, d = x2d.shape
    tm = min(512, m)
    return pl.pallas_call(
        _norm_kernel,
        grid=(m // tm,),
        in_specs=[pl.BlockSpec((tm, d), lambda i: (i, 0)), pl.BlockSpec((1, d), lambda i: (0, 0))],
        out_specs=pl.BlockSpec((tm, d), lambda i: (i, 0)),
        out_shape=jax.ShapeDtypeStruct((m, d), BF16),
        compiler_params=_cparams(("parallel",), 32),
        name="rmsnorm",
    )(x2d, g.reshape(1, d))


def _mm_kernel(a_ref, bt_ref, o_ref):
    o_ref[...] = _dot_nt(a_ref[...], bt_ref[...])


def _in_proj(a, wt_all, layer):
    m, k = a.shape
    n = wt_all.shape[1]
    tm, tn = min(1024, m), 1024
    return pl.pallas_call(
        _mm_kernel,
        grid=(m // tm, n // tn),
        in_specs=[pl.BlockSpec((tm, k), lambda i, j: (i, 0)),
                  pl.BlockSpec((None, tn, k), lambda i, j: (layer, j, 0))],
        out_specs=pl.BlockSpec((tm, tn), lambda i, j: (i, j)),
        out_shape=jax.ShapeDtypeStruct((m, n), F32),
        compiler_params=_cparams(("parallel", "arbitrary"), 48),
        name="in_proj",
    )(a, wt_all)


PREP_TM = 256
N_PREP_IN = 10


def _prep_kernel(aq_ref, ak_ref, av_ref, dq_ref, dkv_ref, sm_ref, gains_ref, bias_ref, wk_ref, wv_ref,
                 foxkv_ref, nsakv_ref, win_ref, so_ref, fq_ref, fk_ref, fv_ref, nq_ref, nkv_ref,
                 kcmp_ref, vcmp_ref):
    tm = PREP_TM
    g_fq, g_fk, g_nq = gains_ref[0:1, :], gains_ref[1:2, :], gains_ref[2:3, :]
    g_ks, g_kw, g_kc = gains_ref[3:4, :], gains_ref[4:5, :], gains_ref[5:6, :]
    for h in range(N_HEADS):
        sl = slice(h * HD, (h + 1) * HD)
        fq_ref[:, sl] = (_rms_lanes(aq_ref[:, sl], g_fq) * Q_PRESCALE).astype(BF16)
        kn = _rms_lanes(ak_ref[:, sl], g_fk)
        foxkv_ref[pl.ds(h, tm, stride=2 * N_HEADS), :] = kn
        fk_ref[:, sl] = kn.astype(BF16)
        v = av_ref[:, sl]
        foxkv_ref[pl.ds(N_HEADS + h, tm, stride=2 * N_HEADS), :] = v
        fv_ref[:, sl] = v.astype(BF16)
        nq_ref[:, sl] = (_rms_lanes(dq_ref[:, sl], g_nq) * Q_PRESCALE).astype(BF16)
    kc = dkv_ref[:, 0:128]
    vc = dkv_ref[:, 128:256]
    ks = _rms_lanes(dkv_ref[:, 256:384], g_ks)
    vs = dkv_ref[:, 384:512]
    kw = _rms_lanes(dkv_ref[:, 512:640], g_kw)
    vw = dkv_ref[:, 640:768]
    for j, a in enumerate((kc, vc, ks, vs)):
        nsakv_ref[pl.ds(j, tm, stride=4), :] = a
    win_ref[pl.ds(0, tm, stride=2), :] = kw
    win_ref[pl.ds(1, tm, stride=2), :] = vw
    for j, a in enumerate((kc, vc, ks, vs, kw, vw)):
        nkv_ref[:, j * HD:(j + 1) * HD] = a.astype(BF16)
    nb = tm // CMP_BLOCK
    kcs = jnp.sum(kc.reshape(nb, CMP_BLOCK, HD) * wk_ref[...][None], axis=1)
    kcmp_ref[...] = _rms_lanes(kcs, g_kc)
    vcmp_ref[...] = jnp.sum(vc.reshape(nb, CMP_BLOCK, HD) * wv_ref[...][None], axis=1)
    y = sm_ref[...] + bias_ref[...]
    lane = lax.broadcasted_iota(jnp.int32, y.shape, 1)
    is_ls = (lane < SM_BI) | ((lane >= SM_BF) & (lane < SM_CA))
    is_sg = (lane >= SM_DG) & (lane < SM_DG + 12)
    so_ref[...] = jnp.where(is_ls, _log_sigmoid(y), jnp.where(is_sg, _sigmoid(y), y))


def _prep(h2d, gains, bias, wk_b, wv_b, layer, depth, prev):
    m = h2d.shape[0]
    tm = PREP_TM

    def col(width, off):
        blk = off // width
        return pl.BlockSpec((tm, width), lambda i, blk=blk: (i, blk))

    def full(shape):
        return pl.BlockSpec(shape, lambda i: (0, 0))

    def rows(width):
        return pl.BlockSpec((tm, width), lambda i: (i, 0))

    def stacked(slots):
        return pl.BlockSpec((None, tm * slots, HD), lambda i: (layer, i, 0))

    out_shapes = (
        jax.ShapeDtypeStruct((depth, m * 8, HD), F32),
        jax.ShapeDtypeStruct((depth, m * 4, HD), F32),
        jax.ShapeDtypeStruct((m * 2, HD), F32),
        jax.ShapeDtypeStruct((m, 128), F32),
        jax.ShapeDtypeStruct((m, 512), BF16),
        jax.ShapeDtypeStruct((m, 512), BF16),
        jax.ShapeDtypeStruct((m, 512), BF16),
        jax.ShapeDtypeStruct((m, 512), BF16),
        jax.ShapeDtypeStruct((m, 768), BF16),
        jax.ShapeDtypeStruct((m // CMP_BLOCK, 128), F32),
        jax.ShapeDtypeStruct((m // CMP_BLOCK, 128), F32),
    )
    out_specs = (stacked(8), stacked(4), pl.BlockSpec((tm * 2, HD), lambda i: (i, 0)),
                 rows(128), rows(512), rows(512), rows(512), rows(512), rows(768),
                 pl.BlockSpec((tm // CMP_BLOCK, 128), lambda i: (i, 0)),
                 pl.BlockSpec((tm // CMP_BLOCK, 128), lambda i: (i, 0)))
    c_specs, c_args, aliases = _carry_args(prev, N_PREP_IN, (0, 1))
    return pl.pallas_call(
        _with_carried_outputs(_prep_kernel, N_PREP_IN, len(c_args)),
        grid=(m // tm,),
        in_specs=[col(512, COL_AQ), col(512, COL_AK), col(512, COL_AV), col(512, COL_DQ),
                  col(768, COL_DKV), col(128, COL_SMALL),
                  full((8, 128)), full((1, 128)), full((CMP_BLOCK, 128)), full((CMP_BLOCK, 128))] + c_specs,
        out_specs=out_specs,
        out_shape=out_shapes,
        input_output_aliases=aliases,
        compiler_params=_cparams(("parallel",), 32),
        name="prep",
    )(h2d, h2d, h2d, h2d, h2d, h2d, gains, bias, wk_b, wv_b, *c_args)


def _cumsum_rows_kernel(x_ref, o_ref):
    o_ref[...] = _cumsum_lanes(x_ref[...])


def _cumsum_rows(x):
    r, n = x.shape
    return pl.pallas_call(
        _cumsum_rows_kernel,
        grid=(1,),
        in_specs=[pl.BlockSpec((r, n), lambda i: (0, 0))],
        out_specs=pl.BlockSpec((r, n), lambda i: (0, 0)),
        out_shape=jax.ShapeDtypeStruct((r, n), F32),
        name="fox_cumsum",
    )(x)


FOX_TQ = 256


def _fox_prompt_kernel(q_ref, k_ref, v_ref, fc_ref, fr_ref, o_ref, *, t_len):
    tq = FOX_TQ
    i = pl.program_id(1)
    n_cls = -(-t_len // KEY_CLASS)
    cls = (i * tq + tq - 1) // KEY_CLASS

    def run(kl):
        qpos = i * tq + lax.broadcasted_iota(jnp.int32, (tq, kl), 0)
        kpos = lax.broadcasted_iota(jnp.int32, (tq, kl), 1)
        mask = kpos <= qpos
        for h in range(N_HEADS):
            sl = slice(h * HD, (h + 1) * HD)
            s = _dot_nt(q_ref[:, sl], k_ref[0:kl, sl])
            s = s + fc_ref[:, h:h + 1] * LOG2E - fr_ref[h:h + 1, 0:kl] * LOG2E
            (e,), inv = _softmax_parts([s], [mask])
            o_ref[:, sl] = _dot(e.astype(BF16), v_ref[0:kl, sl]) * inv

    for c in range(n_cls):
        pl.when(cls == c)(functools.partial(run, min((c + 1) * KEY_CLASS, t_len)))


def _fox_prompt(fq, fk, fv, f_col, f_row):
    b, t, _ = fq.shape
    tq = FOX_TQ
    return pl.pallas_call(
        functools.partial(_fox_prompt_kernel, t_len=t),
        grid=(b, t // tq),
        in_specs=[pl.BlockSpec((None, tq, MIX_W), lambda bi, i: (bi, i, 0)),
                  pl.BlockSpec((None, t, MIX_W), lambda bi, i: (bi, 0, 0)),
                  pl.BlockSpec((None, t, MIX_W), lambda bi, i: (bi, 0, 0)),
                  pl.BlockSpec((None, tq, N_HEADS), lambda bi, i: (bi, i, 0)),
                  pl.BlockSpec((None, N_HEADS, t), lambda bi, i: (bi, 0, 0))],
        out_specs=pl.BlockSpec((None, tq, MIX_W), lambda bi, i: (bi, i, 0)),
        out_shape=jax.ShapeDtypeStruct((b, t, MIX_W), F32),
        compiler_params=_cparams(("parallel", "arbitrary"), 48),
        name="fox_prompt",
    )(fq, fk, fv, f_col, f_row)


PAGE_GROUP = 2


def _fox_sample_stages(q_ref, kn_ref, vn_ref, so_ref, spread_ref, kv_pages, lf_pages, o_ref, lf_scr,
                       n_pages, t_new):
    past = n_pages * PAGE_SIZE
    w2 = 2 * PAGE_SIZE
    n_rows = n_pages * N_HEADS
    heads = range(N_HEADS)
    sls = [slice(h * HD, (h + 1) * HD) for h in heads]
    st = {"qk": {}}

    def kv_rows(p, h):
        return kv_pages[p][pl.ds(h, w2, stride=N_HEADS), :].astype(BF16)

    def prefix_sums():
        for p in range(n_pages):
            lf_scr[p * N_HEADS:(p + 1) * N_HEADS, :] = lf_pages[p][...]
        lf_rows = lf_scr[...]
        local = _dot_exact(lf_rows, spread_ref[...])
        tot = _dot_exact(lf_rows, jnp.ones((PAGE_SIZE, 128), F32))
        rr = lax.broadcasted_iota(jnp.int32, (n_rows, n_rows), 0)
        cc = lax.broadcasted_iota(jnp.int32, (n_rows, n_rows), 1)
        same_head = (rr & (N_HEADS - 1)) == (cc & (N_HEADS - 1))
        earlier = jnp.where(same_head & (cc < rr), 1.0, 0.0)
        up_to = jnp.where(same_head & (cc <= rr), 1.0, 0.0)
        st["f_int"] = (local + _dot_exact(earlier, tot)[:, 0:1]) * LOG2E
        st["f_total"] = _dot_exact(up_to, tot)[n_rows - N_HEADS:n_rows, 0:1]
        st["f_new"] = _dot_exact(_tril_ones(t_new), so_ref[...])
        st["qs"] = [q_ref[:, sl].astype(F32) for sl in sls]

    def pair_tile(p, pr):
        return jnp.concatenate([kv_rows(p, pr[0]), kv_rows(p, pr[1])], axis=1)

    def scores(pr):
        def run():
            zero = jnp.zeros((t_new, HD), F32)
            q_pair = jnp.concatenate(
                [jnp.concatenate([st["qs"][pr[0]], zero], axis=1),
                 jnp.concatenate([zero, st["qs"][pr[1]]], axis=1)], axis=0).astype(BF16)
            both = [_dot_nt(q_pair, pair_tile(p, pr)) for p in range(n_pages)]
            st["qk"][pr[0]] = [x[0:t_new] for x in both]
            st["qk"][pr[1]] = [x[t_new:2 * t_new] for x in both]
        return run

    def softmax():
        lane = lax.broadcasted_iota(jnp.int32, (t_new, w2), 1)
        is_key = (lane & 1) == 0
        kpos0 = lane >> 1
        tpos = lax.broadcasted_iota(jnp.int32, (t_new, w2), 0) + past
        rn = lax.broadcasted_iota(jnp.int32, (t_new, t_new), 0)
        cn = lax.broadcasted_iota(jnp.int32, (t_new, t_new), 1)
        m_list = [is_key & (kpos0 + p * PAGE_SIZE <= tpos) for p in range(n_pages)] + [cn <= rn]
        qk_new = [_dot_nt(st["qs"][h], kn_ref[:, sls[h]].astype(F32)) for h in heads]
        st["w"] = []
        for h in heads:
            fq = (st["f_new"][:, SM_AF + h:SM_AF + h + 1] + st["f_total"][h:h + 1, :]) * LOG2E
            s_list = [st["qk"][h][p] + fq - st["f_int"][p * N_HEADS + h:p * N_HEADS + h + 1, :]
                      for p in range(n_pages)]
            s_list.append(qk_new[h] + fq - _col_to_row(fq))
            st["w"].append(_softmax_parts(s_list, m_list))

    def values(pr):
        def run():
            (es0, inv0), (es1, inv1) = st["w"][pr[0]], st["w"][pr[1]]
            acc = None
            for p in range(n_pages):
                w_pair = jnp.concatenate([pltpu.roll(es0[p], 1, axis=1), pltpu.roll(es1[p], 1, axis=1)], axis=0)
                part = _dot(w_pair.astype(BF16), pair_tile(p, pr))
                acc = part if acc is None else acc + part
            o0 = acc[0:t_new, 0:HD] + _dot(es0[n_pages], vn_ref[:, sls[pr[0]]].astype(F32))
            o1 = acc[t_new:2 * t_new, HD:2 * HD] + _dot(es1[n_pages], vn_ref[:, sls[pr[1]]].astype(F32))
            o_ref[:, sls[pr[0]]] = o0 * inv0
            o_ref[:, sls[pr[1]]] = o1 * inv1
        return run

    pairs = [(h, h + 1) for h in range(0, N_HEADS, 2)]
    return [prefix_sums, scores(pairs[0]), scores(pairs[1]), softmax, values(pairs[0]), values(pairs[1])]


def _spread_matrix():
    r = jnp.arange(PAGE_SIZE)[:, None]
    c = jnp.arange(2 * PAGE_SIZE)[None, :]
    return (2 * r <= c).astype(F32)


N_MLSTM_IN = 9
BATCH_UNROLL = 4


def _mlstm_kernel(q_ref, k_ref, v_ref, og_ref, so_ref, hg_ref, c0_ref, n0_ref, m0_ref,
                  o_ref, c_ref, n_ref, m_ref, *, bb, chunk, wide_chains):
    L = chunk
    Lp = max(L, BF16_ROWS)

    @pl.when(pl.program_id(1) == 0)
    def _():
        c_ref[...] = c0_ref[...]
        n_ref[...] = n0_ref[...]
        m_ref[...] = m0_ref[...]

    tril = _tril_ones(Lp)
    causal = tril > 0.5
    hg = hg_ref[...]
    lane = lax.broadcasted_iota(jnp.int32, (Lp - L, 128), 1) if Lp > L else None

    def run_chains(idx):
        so_of, csum_of = [], []
        for bi in idx:
            so = so_ref[bi]
            if Lp > L:
                pad = jnp.where((lane >= SM_BI) & (lane < SM_BF), -1e30, 0.0)
                so = jnp.concatenate([so, pad], axis=0)
            so_of.append(so)
            csum_of.append(_dot_exact(tril, so))
        ch = [(bi, h) for bi in idx for h in range(N_HEADS)]
        seq = [u for u in range(len(idx)) for _ in range(N_HEADS)]
        n = range(len(ch))
        sls = [slice(h * HD, (h + 1) * HD) for _, h in ch]
        q = [_pad_rows(q_ref[bi, :, sls[c]], Lp) for c, (bi, _) in enumerate(ch)]
        k = [_pad_rows(k_ref[bi, :, sls[c]], Lp) * ATT_SCALE for c, (bi, _) in enumerate(ch)]
        vb = [_pad_rows(v_ref[bi, :, sls[c]], Lp).astype(BF16) for c, (bi, _) in enumerate(ch)]
        qb = [x.astype(BF16) for x in q]
        ig = [so_of[seq[c]][:, SM_BI + h:SM_BI + h + 1] for c, (_, h) in enumerate(ch)]
        bcol = [csum_of[seq[c]][:, SM_BF + h:SM_BF + h + 1] for c, (_, h) in enumerate(ch)]
        c_st = [c_ref[bi, h] for bi, h in ch]
        n_st = [n_ref[bi, h:h + 1, :] for bi, h in ch]
        m_st = [m_ref[bi, h:h + 1, 0:1] for bi, h in ch]
        qk_raw = [_dot_nt(qb[c], k[c].astype(BF16)) for c in n]
        q_c = [_dot(qb[c], c_st[c].astype(BF16)) for c in n]
        dmat = [jnp.where(causal, bcol[c] - _col_to_row(bcol[c]) + _col_to_row(ig[c]), NEG_INF) for c in n]
        inter = [bcol[c] + m_st[c] for c in n]
        m_t = [jnp.maximum(inter[c], dmat[c].max(axis=-1, keepdims=True)) for c in n]
        w_inter = [jnp.exp(inter[c] - m_t[c]) for c in n]
        qk = [qk_raw[c] * jnp.exp(dmat[c] - m_t[c]) for c in n]
        qk_v = [_dot(qk[c].astype(BF16), vb[c]) for c in n]
        b_last = [bcol[c][Lp - 1:Lp, :] for c in n]
        g_end = [b_last[c] - bcol[c] + ig[c] for c in n]
        m_new = [jnp.maximum(b_last[c] + m_st[c], g_end[c].max(axis=0, keepdims=True)) for c in n]
        a_prev = [jnp.exp(b_last[c] + m_st[c] - m_new[c]) for c in n]
        kw = [k[c] * jnp.exp(g_end[c] - m_new[c]) for c in n]
        k_v = [_dot_tn(kw[c].astype(BF16), vb[c]) for c in n]
        new_states = []
        for c, (bi, h) in enumerate(ch):
            num = w_inter[c] * q_c[c] + qk_v[c]
            den = (w_inter[c] * jnp.sum(q[c] * n_st[c], axis=-1, keepdims=True)
                   + jnp.sum(qk[c], axis=-1, keepdims=True))
            hout = num / jnp.maximum(jnp.abs(den), jnp.exp(-m_t[c]))
            new_states.append((a_prev[c] * c_st[c] + k_v[c],
                               a_prev[c] * n_st[c] + jnp.sum(kw[c], axis=0, keepdims=True),
                               jnp.broadcast_to(m_new[c], (1, HD))))
            hn = _rms_lanes(hout[0:L], hg)
            o_ref[bi, :, sls[c]] = hn * _sigmoid(og_ref[bi, :, sls[c]])
        for (bi, h), (c_new, n_new, m_nw) in zip(ch, new_states):
            c_ref[bi, h] = c_new
            n_ref[bi, h:h + 1, :] = n_new
            m_ref[bi, h:h + 1, :] = m_nw

    unroll = min(BATCH_UNROLL, bb)
    wide = min(wide_chains, unroll)

    def group(g, carry):
        for u0 in range(0, unroll, wide):
            run_chains([g * unroll + u0 + u for u in range(wide)])
        return carry

    if bb == unroll:
        group(0, 0)
    else:
        lax.fori_loop(0, bb // unroll, group, 0)


def _mlstm(h3d, so3d, hg, c0, n0, m0, state_layer, layer, depth, prev, bb, chunk):
    b, t, _ = h3d.shape
    nc = t // chunk
    wide_chains = 4 if chunk >= 64 else 2

    def col(off):
        blk = off // MIX_W
        return pl.BlockSpec((bb, chunk, MIX_W), lambda bi, c, blk=blk: (bi, c, blk))

    def st_c(li):
        return pl.BlockSpec((None, bb, N_HEADS, HD, HD), lambda bi, c: (li, bi, 0, 0, 0))

    def st_n(li):
        return pl.BlockSpec((None, bb, N_HEADS, HD), lambda bi, c: (li, bi, 0, 0))

    c_specs, c_args, aliases = _carry_args(prev, N_MLSTM_IN, (1, 2, 3))
    return pl.pallas_call(
        _with_carried_outputs(functools.partial(_mlstm_kernel, bb=bb, chunk=chunk, wide_chains=wide_chains),
                              N_MLSTM_IN, len(c_args)),
        grid=(b // bb, nc),
        in_specs=[col(COL_BQ), col(COL_BK), col(COL_BV), col(COL_BO),
                  pl.BlockSpec((bb, chunk, 128), lambda bi, c: (bi, c, 0)),
                  pl.BlockSpec((1, HD), lambda bi, c: (0, 0)),
                  st_c(state_layer), st_n(state_layer), st_n(state_layer)] + c_specs,
        out_specs=(pl.BlockSpec((bb, chunk, MIX_W), lambda bi, c: (bi, c, 0)),
                   st_c(layer), st_n(layer), st_n(layer)),
        out_shape=(jax.ShapeDtypeStruct((b, t, MIX_W), F32),
                   jax.ShapeDtypeStruct((depth, b, N_HEADS, HD, HD), F32),
                   jax.ShapeDtypeStruct((depth, b, N_HEADS, HD), F32),
                   jax.ShapeDtypeStruct((depth, b, N_HEADS, HD), F32)),
        input_output_aliases=aliases,
        compiler_params=_cparams(("parallel", "arbitrary"), 40),
        name="mlstm",
    )(h3d, h3d, h3d, h3d, so3d, hg.reshape(1, HD), c0, n0, m0, *c_args)


N_GLA_IN = 8
GLA_SUB = 16


def _gla_kernel(q_ref, k_ref, v_ref, so_ref, w2_ref, b2_ref, hg_ref, s0_ref, o_ref, s_ref, *, bb, chunk,
                wide_seqs):
    L = chunk
    Lp = max(L, BF16_ROWS)
    sub = GLA_SUB

    @pl.when(pl.program_id(1) == 0)
    def _():
        s_ref[...] = s0_ref[...]

    tril = _tril_ones(Lp)
    hg = hg_ref[...]
    w2 = w2_ref[...].astype(BF16)
    b2 = b2_ref[...]
    q_scale = GLA_DK ** -0.5
    real_row = lax.broadcasted_iota(jnp.int32, (Lp, 1), 0) < L

    heads = range(N_HEADS)
    dkw = N_HEADS * GLA_DK
    head_of_lane = lax.broadcasted_iota(jnp.int32, (1, dkw), 1) >> GLA_DK_SHIFT
    zero_blk = jnp.zeros((GLA_DK, GLA_DV), F32)

    def run_seqs(idx):
        n = range(len(idx))
        s_old = [[s_ref[bi, h] for h in heads] for bi in idx]
        pre = [_dot(_pad_rows(so_ref[bi], Lp).astype(BF16), w2) + b2 for bi in idx]
        la = [jnp.where(real_row, _log_sigmoid(pre[u]) / GLA_TAU, 0.0) for u in n]
        bcs = [_dot_exact(tril, la[u]) for u in n]
        q = [_pad_rows(q_ref[bi], Lp) * q_scale for bi in idx]
        k = [_pad_rows(k_ref[bi], Lp) for bi in idx]
        vb = [_pad_rows(v_ref[bi], Lp).astype(BF16) for bi in idx]
        s_bd = [jnp.concatenate(
            [jnp.concatenate([s_old[u][h] if g == h else zero_blk for g in heads], axis=1) for h in heads],
            axis=0) for u in n]
        inter = [_dot((q[u] * jnp.exp(bcs[u])).astype(BF16), s_bd[u].astype(BF16)) for u in n]
        for i in range(Lp // sub):
            r0 = i * sub
            hi = r0 + sub
            rows = min(hi, L) - r0
            rr = (lax.broadcasted_iota(jnp.int32, (N_HEADS * sub, hi), 0) & (sub - 1)) + r0
            cc = lax.broadcasted_iota(jnp.int32, (N_HEADS * sub, hi), 1)
            base = [bcs[u][r0 - 1:r0, :] if i > 0 else jnp.zeros((1, dkw), F32) for u in n]
            qi = [q[u][r0:hi, :] * jnp.exp(bcs[u][r0:hi, :] - base[u]) for u in n]
            ke = [(k[u][0:hi, :] * jnp.exp(base[u] - bcs[u][0:hi, :])).astype(BF16) for u in n]
            q_heads = [jnp.concatenate([jnp.where(head_of_lane == h, qi[u], 0.0) for h in heads], axis=0)
                       for u in n]
            a = [jnp.where(cc <= rr, _dot_nt(q_heads[u].astype(BF16), ke[u]), 0.0).astype(BF16) for u in n]
            for u, bi in enumerate(idx):
                for h in heads:
                    vsl = slice(h * GLA_DV, (h + 1) * GLA_DV)
                    oi = inter[u][r0:hi, vsl] + _dot(a[u][h * sub:(h + 1) * sub, :], vb[u][0:hi, vsl])
                    o_ref[bi, r0:r0 + rows, vsl] = _rms_lanes(oi[0:rows], hg)
        b_end = [bcs[u][Lp - 1:Lp, :] for u in n]
        kd = [(k[u] * jnp.exp(b_end[u] - bcs[u])).astype(BF16) for u in n]
        upd = [_dot_tn(kd[u], vb[u]) for u in n]
        dcol = [_row_to_col(jnp.exp(b_end[u])) for u in n]
        for u, bi in enumerate(idx):
            for h in heads:
                s_ref[bi, h] = (dcol[u][h * GLA_DK:(h + 1) * GLA_DK, :] * s_old[u][h]
                                + upd[u][h * GLA_DK:(h + 1) * GLA_DK, h * GLA_DV:(h + 1) * GLA_DV])

    unroll = min(BATCH_UNROLL, bb)
    wide = min(wide_seqs, unroll)

    def group(g, carry):
        for u0 in range(0, unroll, wide):
            run_seqs([g * unroll + u0 + u for u in range(wide)])
        return carry

    if bb == unroll:
        group(0, 0)
    else:
        lax.fori_loop(0, bb // unroll, group, 0)


def _gla(h3d, so3d, w2pad, b2, hg, s0, state_layer, layer, depth, prev, bb, chunk):
    b, t, _ = h3d.shape
    nc = t // chunk

    def st(li):
        return pl.BlockSpec((None, bb, N_HEADS, GLA_DK, GLA_DV), lambda bi, c: (li, bi, 0, 0, 0))

    c_specs, c_args, aliases = _carry_args(prev, N_GLA_IN, (1,))
    return pl.pallas_call(
        _with_carried_outputs(functools.partial(_gla_kernel, bb=bb, chunk=chunk, wide_seqs=4),
                              N_GLA_IN, len(c_args)),
        grid=(b // bb, nc),
        in_specs=[pl.BlockSpec((bb, chunk, 256), lambda bi, c: (bi, c, COL_CQ // 256)),
                  pl.BlockSpec((bb, chunk, 256), lambda bi, c: (bi, c, COL_CK // 256)),
                  pl.BlockSpec((bb, chunk, MIX_W), lambda bi, c: (bi, c, COL_CV // MIX_W)),
                  pl.BlockSpec((bb, chunk, 128), lambda bi, c: (bi, c, 0)),
                  pl.BlockSpec((128, 256), lambda bi, c: (0, 0)),
                  pl.BlockSpec((1, 256), lambda bi, c: (0, 0)),
                  pl.BlockSpec((1, GLA_DV), lambda bi, c: (0, 0)),
                  st(state_layer)] + c_specs,
        out_specs=(pl.BlockSpec((bb, chunk, MIX_W), lambda bi, c: (bi, c, 0)), st(layer)),
        out_shape=(jax.ShapeDtypeStruct((b, t, MIX_W), F32),
                   jax.ShapeDtypeStruct((depth, b, N_HEADS, GLA_DK, GLA_DV), F32)),
        input_output_aliases=aliases,
        compiler_params=_cparams(("parallel", "arbitrary"), 40),
        name="gla",
    )(h3d, h3d, h3d, so3d, w2pad, b2.reshape(1, 256), hg.reshape(1, GLA_DV), s0, *c_args)


def _select_blocks(imp, qpos_col, ns):
    r = imp.shape[0]
    j = lax.broadcasted_iota(jnp.int32, (r, 128), 1)
    cur = qpos_col >> SEL_SHIFT
    valid = j <= cur
    forced = (j == 0) | (valid & (j >= cur - 1))
    val = jnp.where(forced, 1e4, jnp.where(valid, imp, -1e4))
    val = jnp.where(j < ns, val, -3e38)
    rank = jnp.zeros((r, 128), F32)
    for i in range(ns):
        ci = val[:, i:i + 1]
        beats = (ci > val) | ((ci == val) & (j > i))
        rank = rank + jnp.where(beats, 1.0, 0.0)
    return (rank < float(min(N_SEL, ns))) & (j < ns)


def _pair_matrix(nc):
    c = jnp.arange(nc)[:, None]
    j = jnp.arange(128)[None, :]
    return (c // (SEL_BLOCK // CMP_BLOCK) == j).astype(F32)


def _expand_matrix(tk):
    j = jnp.arange(128)[:, None]
    s = jnp.arange(tk)[None, :]
    return (s // SEL_BLOCK == j).astype(BF16)


NSA_TQ = 128
STACK_HEADS_MAX_KEYS = 1024


def _nsa_prompt_kernel(q_ref, kcmp_ref, vcmp_ref, kv_ref, win_ref, so_ref, e_ref, pair_ref, o_ref, *, t_len):
    tq = NSA_TQ
    nc = t_len // CMP_BLOCK
    ns = -(-t_len // SEL_BLOCK)
    i = pl.program_id(1)
    start = pl.multiple_of(i * tq, tq)
    qpos_col = start + lax.broadcasted_iota(jnp.int32, (tq, 1), 0)
    so = so_ref[...]
    q_all = jnp.concatenate([q_ref[:, h * HD:(h + 1) * HD] for h in range(N_HEADS)], axis=0)
    sc = _dot_nt(q_all, kcmp_ref[...].astype(BF16))
    qpos4 = start + (lax.broadcasted_iota(jnp.int32, (N_HEADS * tq, 1), 0) & (tq - 1))
    cend = lax.broadcasted_iota(jnp.int32, (N_HEADS * tq, nc), 1) * CMP_BLOCK + (CMP_BLOCK - 1)
    (ec,), invc = _softmax_parts([sc], [cend <= qpos4])
    pc = ec * invc
    o_cmp = _dot(pc.astype(BF16), vcmp_ref[...].astype(BF16))
    imp_c = pc[0:tq]
    for h in range(1, N_HEADS):
        imp_c = imp_c + pc[h * tq:(h + 1) * tq]
    imp = _dot_exact(imp_c, pair_ref[...])
    sel_b = jnp.where(_select_blocks(imp, qpos_col, ns), 1.0, 0.0).astype(BF16)
    band = WINDOW + tq
    kwin = win_ref[pl.ds(start, band), 0:HD]
    vwin = win_ref[pl.ds(start, band), HD:2 * HD]
    wpos = start - WINDOW + lax.broadcasted_iota(jnp.int32, (tq, band), 1)
    wmask = (wpos <= qpos_col) & (wpos > qpos_col - WINDOW) & (wpos >= 0)
    def shared_kv_attention(keys, values, mask):
        s_all = _dot_nt(q_all, keys)
        parts = [_softmax_parts([s_all[h * tq:(h + 1) * tq]], [mask]) for h in range(N_HEADS)]
        e_all = jnp.concatenate([es.astype(BF16) for (es,), _ in parts], axis=0)
        o_all = _dot(e_all, values)
        return [o_all[h * tq:(h + 1) * tq] * parts[h][1] for h in range(N_HEADS)]

    o_win = shared_kv_attention(kwin, vwin, wmask)
    for h in range(N_HEADS):
        sl = slice(h * HD, (h + 1) * HD)
        g0 = so[:, SM_DG + 3 * h:SM_DG + 3 * h + 1]
        g2 = so[:, SM_DG + 3 * h + 2:SM_DG + 3 * h + 3]
        o_ref[:, sl] = g0 * o_cmp[h * tq:(h + 1) * tq] + g2 * o_win[h]

    n_cls = -(-t_len // KEY_CLASS)
    cls = (start + tq - 1) // KEY_CLASS

    def run_selected(kl):
        sel_keys = _dot(sel_b, e_ref[:, 0:kl]) > 0.5
        kpos = lax.broadcasted_iota(jnp.int32, (tq, kl), 1)
        smask = sel_keys & (kpos <= qpos_col)
        ks = kv_ref[0:kl, 2 * HD:3 * HD]
        vs = kv_ref[0:kl, 3 * HD:4 * HD]
        o_sel = shared_kv_attention(ks, vs, smask) if kl <= STACK_HEADS_MAX_KEYS else None
        for h in range(N_HEADS):
            sl = slice(h * HD, (h + 1) * HD)
            if o_sel is None:
                (es,), invs = _softmax_parts([_dot_nt(q_ref[:, sl], ks)], [smask])
                o_h = _dot(es.astype(BF16), vs) * invs
            else:
                o_h = o_sel[h]
            g1 = so[:, SM_DG + 3 * h + 1:SM_DG + 3 * h + 2]
            o_ref[:, sl] = o_ref[:, sl] + g1 * o_h

    for c in range(n_cls):
        pl.when(cls == c)(functools.partial(run_selected, min((c + 1) * KEY_CLASS, t_len)))


def _nsa_prompt(nq, kcmp, vcmp, nkv, win_pad, so):
    b, t, _ = nq.shape
    tq = NSA_TQ
    nc = t // CMP_BLOCK
    e_mat = _expand_matrix(t)
    pair = _pair_matrix(nc)
    return pl.pallas_call(
        functools.partial(_nsa_prompt_kernel, t_len=t),
        grid=(b, t // tq),
        in_specs=[pl.BlockSpec((None, tq, MIX_W), lambda bi, i: (bi, i, 0)),
                  pl.BlockSpec((None, nc, HD), lambda bi, i: (bi, 0, 0)),
                  pl.BlockSpec((None, nc, HD), lambda bi, i: (bi, 0, 0)),
                  pl.BlockSpec((None, t, 768), lambda bi, i: (bi, 0, 0)),
                  pl.BlockSpec((None, t + WINDOW, 256), lambda bi, i: (bi, 0, 0)),
                  pl.BlockSpec((None, tq, 128), lambda bi, i: (bi, i, 0)),
                  pl.BlockSpec((128, t), lambda bi, i: (0, 0)),
                  pl.BlockSpec((nc, 128), lambda bi, i: (0, 0))],
        out_specs=pl.BlockSpec((None, tq, MIX_W), lambda bi, i: (bi, i, 0)),
        out_shape=jax.ShapeDtypeStruct((b, t, MIX_W), F32),
        compiler_params=_cparams(("parallel", "arbitrary"), 48),
        name="nsa_prompt",
    )(nq, kcmp, vcmp, nkv, win_pad, so, e_mat, pair)


def _nsa_sample_stages(q_ref, kvn_ref, winn_ref, so_ref, win_ref, wk_ref, wv_ref, kcg_ref, e_ref, pair_ref,
                       pages, o_ref, wout_ref, cmp_k, cmp_v, n_pages, t_new):
    past = n_pages * PAGE_SIZE
    nc = past // CMP_BLOCK
    ns = -(-(past + t_new) // SEL_BLOCK)
    per_page = PAGE_SIZE // CMP_BLOCK
    n_grp = n_pages // PAGE_GROUP
    gw = PAGE_GROUP * PAGE_SIZE
    rows = N_HEADS * t_new
    st = {}

    def page_rows(p, j):
        return pages[p][pl.ds(j, PAGE_SIZE, stride=4), :]

    def group_rows(g, j):
        return jnp.concatenate([page_rows(g * PAGE_GROUP + u, j).astype(BF16) for u in range(PAGE_GROUP)], axis=0)

    def compress():
        for p in range(n_pages):
            kc = page_rows(p, 0).reshape(per_page, CMP_BLOCK, HD)
            vc = page_rows(p, 1).reshape(per_page, CMP_BLOCK, HD)
            cmp_k[p * per_page:(p + 1) * per_page, :] = jnp.sum(kc * wk_ref[...][None], axis=1)
            cmp_v[p * per_page:(p + 1) * per_page, :] = jnp.sum(vc * wv_ref[...][None], axis=1)
        kcmp = _rms_lanes(cmp_k[...], kcg_ref[...]).astype(BF16)
        st["vcmp"] = cmp_v[...].astype(BF16)
        q_all = jnp.concatenate([q_ref[:, h * HD:(h + 1) * HD].astype(F32) for h in range(N_HEADS)], axis=0)
        st["q_all"] = q_all
        st["q_b"] = q_all.astype(BF16)
        trow = lax.broadcasted_iota(jnp.int32, (rows, 1), 0) & (t_new - 1)
        st["qpos4"] = past + trow
        st["kvn"] = kvn_ref[...].astype(F32)
        st["npos"] = past + lax.broadcasted_iota(jnp.int32, (rows, t_new), 1)
        st["causal_new"] = st["npos"] <= st["qpos4"]
        st["sc"] = _dot_nt(st["q_b"], kcmp)

    def selected_scores():
        s_list = [_dot_nt(st["q_b"], group_rows(g, 2)) for g in range(n_grp)]
        s_list.append(_dot_nt(st["q_all"], st["kvn"][:, 2 * HD:3 * HD]))
        st["s_list"] = s_list

    def compressed():
        cend = lax.broadcasted_iota(jnp.int32, (rows, nc), 1) * CMP_BLOCK + (CMP_BLOCK - 1)
        (ec,), invc = _softmax_parts([st["sc"]], [cend <= st["qpos4"]])
        pc = ec * invc
        imp_c = pc[0:t_new]
        for h in range(1, N_HEADS):
            imp_c = imp_c + pc[h * t_new:(h + 1) * t_new]
        st["imp"] = _dot_exact(imp_c, pair_ref[...])
        st["o_cmp"] = _dot(pc.astype(BF16), st["vcmp"])

    def window():
        qpos4, npos, kvn = st["qpos4"], st["npos"], st["kvn"]
        w_buf = win_ref.shape[0] // 2
        kwb = win_ref[pl.ds(0, w_buf, stride=2), :].astype(BF16)
        vwb = win_ref[pl.ds(1, w_buf, stride=2), :].astype(BF16)
        wpos = past - w_buf + lax.broadcasted_iota(jnp.int32, (rows, w_buf), 1)
        wmask = (wpos <= qpos4) & (wpos > qpos4 - WINDOW) & (wpos >= 0)
        sw_list = [_dot_nt(st["q_b"], kwb), _dot_nt(st["q_all"], kvn[:, 4 * HD:5 * HD])]
        ew, invw = _softmax_parts(sw_list, [wmask, st["causal_new"] & (npos > qpos4 - WINDOW)])
        st["o_win"] = (_dot(ew[0].astype(BF16), vwb) + _dot(ew[1], kvn[:, 5 * HD:6 * HD])) * invw
        keep = 2 * (w_buf - t_new)
        wout_ref[0:keep, :] = win_ref[2 * t_new:2 * w_buf, :]
        wout_ref[keep:2 * w_buf, :] = winn_ref[...]

    def select():
        qpos_col = past + lax.broadcasted_iota(jnp.int32, (t_new, 1), 0)
        sel = _select_blocks(st["imp"], qpos_col, ns)
        self32 = jnp.where(sel, 1.0, 0.0)
        sel_past = _dot(self32.astype(BF16), e_ref[...])
        sel_past4 = jnp.concatenate([sel_past] * N_HEADS, axis=0) > 0.5
        new_blk = past // SEL_BLOCK
        sel_new = jnp.concatenate([self32[:, new_blk:new_blk + 1]] * N_HEADS, axis=0) > 0.5
        m_list = []
        for g in range(n_grp):
            kpos = g * gw + lax.broadcasted_iota(jnp.int32, (rows, gw), 1)
            m_list.append(sel_past4[:, g * gw:(g + 1) * gw] & (kpos <= st["qpos4"]))
        m_list.append(sel_new & st["causal_new"])
        st["m_list"] = m_list

    def selected():
        es, invs = _softmax_parts(st["s_list"], st["m_list"])
        o_sel = _dot(es[n_grp], st["kvn"][:, 3 * HD:4 * HD])
        for g in range(n_grp):
            o_sel = o_sel + _dot(es[g].astype(BF16), group_rows(g, 3))
        o_sel = o_sel * invs
        so = so_ref[...]
        for h in range(N_HEADS):
            r = slice(h * t_new, (h + 1) * t_new)
            g0 = so[:, SM_DG + 3 * h:SM_DG + 3 * h + 1]
            g1 = so[:, SM_DG + 3 * h + 1:SM_DG + 3 * h + 2]
            g2 = so[:, SM_DG + 3 * h + 2:SM_DG + 3 * h + 3]
            o_ref[:, h * HD:(h + 1) * HD] = g0 * st["o_cmp"][r] + g1 * o_sel[r] + g2 * st["o_win"][r]

    return [compress, selected_scores, compressed, window, select, selected]


N_SAMPLE_IN = 15


def _sample_attn_kernel(pt_ref, fq_ref, fk_ref, fv_ref, so_ref, spread_ref, nq_ref, nkv_ref, winn_ref,
                        win_ref, wk_ref, wv_ref, kcg_ref, e_ref, pair_ref, *rest, n_pages, t_new):
    del pt_ref
    kv_pages = rest[:n_pages]
    lf_pages = rest[n_pages:2 * n_pages]
    nsa_pages = rest[2 * n_pages:3 * n_pages]
    o_fox, o_nsa, wout_ref, lf_scr, cmp_k, cmp_v = rest[3 * n_pages:3 * n_pages + 6]
    fox = _fox_sample_stages(fq_ref, fk_ref, fv_ref, so_ref, spread_ref, kv_pages, lf_pages, o_fox, lf_scr,
                             n_pages, t_new)
    nsa = _nsa_sample_stages(nq_ref, nkv_ref, winn_ref, so_ref, win_ref, wk_ref, wv_ref, kcg_ref, e_ref,
                             pair_ref, nsa_pages, o_nsa, wout_ref, cmp_k, cmp_v, n_pages, t_new)
    for n_stage, f_stage in zip(nsa, fox):
        n_stage()
        f_stage()


def _sample_attn(layer, depth, prev, page_table, fq, fk, fv, nq, nkv, win_new, so, fox_kv4, fox_lf4, nsa_kv4,
                 win_state4, wk_b, wv_b, kcg):
    b, t, _ = nq.shape
    n_pages = page_table.shape[1]
    past = n_pages * PAGE_SIZE
    nc = past // CMP_BLOCK
    w2 = win_state4.shape[2]
    e_mat = _expand_matrix(past)
    pair = _pair_matrix(nc)

    def tok(rows, width):
        return pl.BlockSpec((None, rows, width), lambda bi, pt: (bi, 0, 0))

    def const(shape):
        return pl.BlockSpec(shape, lambda bi, pt: (0, 0))

    def page_spec(rows, p):
        return pl.BlockSpec((None, None, rows, 128), lambda bi, pt, p=p: (layer, pt[bi, p], 0, 0))

    in_specs = [tok(t, MIX_W), tok(t, MIX_W), tok(t, MIX_W), tok(t, 128), const((PAGE_SIZE, 2 * PAGE_SIZE)),
                tok(t, MIX_W), tok(t, 768), tok(2 * t, HD),
                pl.BlockSpec((None, None, w2, 128), lambda bi, pt: (layer, bi, 0, 0)),
                const((CMP_BLOCK, 128)), const((CMP_BLOCK, 128)), const((1, 128)),
                const((128, past)), const((nc, 128))]
    in_specs += [page_spec(PAGE_SIZE * 2 * N_HEADS, p) for p in range(n_pages)]
    in_specs += [page_spec(N_HEADS, p) for p in range(n_pages)]
    in_specs += [page_spec(PAGE_SIZE * 4, p) for p in range(n_pages)]
    n_in = N_SAMPLE_IN + 3 * n_pages
    c_specs, c_args, aliases = _carry_args(prev, n_in, (2,))
    out_tok = pl.BlockSpec((None, t, MIX_W), lambda bi, pt: (bi, 0, 0))
    grid_spec = pltpu.PrefetchScalarGridSpec(
        num_scalar_prefetch=1, grid=(b,), in_specs=in_specs + c_specs,
        out_specs=(out_tok, out_tok, pl.BlockSpec((None, None, w2, 128), lambda bi, pt: (layer, bi, 0, 0))),
        scratch_shapes=[pltpu.VMEM((n_pages * N_HEADS, PAGE_SIZE), F32),
                        pltpu.VMEM((nc, HD), F32), pltpu.VMEM((nc, HD), F32)])
    return pl.pallas_call(
        _with_carried_outputs(functools.partial(_sample_attn_kernel, n_pages=n_pages, t_new=t), n_in, len(c_args)),
        grid_spec=grid_spec,
        out_shape=(jax.ShapeDtypeStruct((b, t, MIX_W), F32), jax.ShapeDtypeStruct((b, t, MIX_W), F32),
                   jax.ShapeDtypeStruct((depth, b, w2, 128), F32)),
        input_output_aliases=aliases,
        compiler_params=_cparams(("arbitrary",), 56),
        name="sample_attn",
    )(page_table, fq, fk, fv, so, _spread_matrix(), nq, nkv, win_new, win_state4, wk_b, wv_b, kcg, e_mat, pair,
      *([fox_kv4] * n_pages), *([fox_lf4] * n_pages), *([nsa_kv4] * n_pages), *c_args)


OUT_TM = 256


def _out_kernel(x_ref, oa_ref, ob_ref, oc_ref, od_ref, z_ref, g0_ref, g1_ref, g2_ref, g3_ref,
                wb_ref, wo_ref, gn_ref, y_ref, xn_ref):
    acc = None
    for g, (o_r, g_r) in enumerate(((oa_ref, g0_ref), (ob_ref, g1_ref), (oc_ref, g2_ref), (od_ref, g3_ref))):
        z = z_ref[:, g * MIX_W:(g + 1) * MIX_W]
        br = (o_r[...] * (z * _sigmoid(z))).astype(BF16)
        term = (jnp.tanh(g_r[...]) + 1.0) * _dot(br, wb_ref[g])
        acc = term if acc is None else acc + term
    y = x_ref[...] + _dot(acc.astype(BF16), wo_ref[...])
    y_ref[...] = y
    xn_ref[...] = _rms_lanes(y, gn_ref[...]).astype(BF16)


def _out_proj(x2d, oa, ob, oc, od, h2d, wb_all, wo_all, layer, next_norm_g):
    m = x2d.shape[0]
    tm = OUT_TM

    def rows(width):
        return pl.BlockSpec((tm, width), lambda i: (i, 0))

    def gate(g):
        return pl.BlockSpec((tm, D_MODEL), lambda i, g=g: (i, COL_GATE // D_MODEL + g))

    single = pl.Buffered(1)
    return pl.pallas_call(
        _out_kernel,
        grid=(m // tm,),
        in_specs=[rows(D_MODEL), rows(MIX_W), rows(MIX_W), rows(MIX_W), rows(MIX_W),
                  rows(D_MODEL), gate(0), gate(1), gate(2), gate(3),
                  pl.BlockSpec((None, N_HEADS, MIX_W, D_MODEL), lambda i: (layer, 0, 0, 0), pipeline_mode=single),
                  pl.BlockSpec((None, D_MODEL, D_MODEL), lambda i: (layer, 0, 0), pipeline_mode=single),
                  pl.BlockSpec((1, D_MODEL), lambda i: (0, 0))],
        out_specs=(rows(D_MODEL), rows(D_MODEL)),
        out_shape=(jax.ShapeDtypeStruct((m, D_MODEL), F32), jax.ShapeDtypeStruct((m, D_MODEL), BF16)),
        compiler_params=_cparams(("parallel",), 56),
        name="out_proj",
    )(x2d, oa, ob, oc, od, h2d, h2d, h2d, h2d, h2d, wb_all, wo_all, next_norm_g.reshape(1, D_MODEL))


def _layer_params(l, p):
    zeros = jnp.zeros((2, HD), F32)
    gains = jnp.concatenate([p["fox_qg"][l][None], p["fox_kg"][l][None], p["nsa_qg"][l][None],
                             p["nsa_ksg"][l][None], p["nsa_kwg"][l][None], p["nsa_kcg"][l][None], zeros], axis=0)
    bias = jnp.concatenate([p["fox_bf"][l], p["mlstm_bi"][l], p["mlstm_bf"][l], jnp.zeros((GLA_RANK,), F32),
                            p["nsa_bg"][l].reshape(-1), jnp.zeros((128 - SM_DG - 12,), F32)]).reshape(1, 128)
    w2pad = jnp.zeros((128, 256), F32).at[SM_CA:SM_CA + GLA_RANK].set(p["gla_w2"][l])
    return dict(
        norm_g=p["norm_g"][l], gains=gains, bias=bias,
        wk_b=jnp.broadcast_to(p["nsa_wk"][l][:, None], (CMP_BLOCK, 128)),
        wv_b=jnp.broadcast_to(p["nsa_wv"][l][:, None], (CMP_BLOCK, 128)),
        kcg=p["nsa_kcg"][l].reshape(1, HD),
        mlstm_hg=p["mlstm_hg"][l], w2pad=w2pad, gla_b2=p["gla_b2"][l], gla_hg=p["gla_hg"][l])


def _layer(x3d, xn, lp, shared, st, prev, *, is_prompt, layer, depth):
    b, t, _ = x3d.shape
    m = b * t
    x2d = x3d.reshape(m, D_MODEL)
    if xn is None:
        xn = _rmsnorm(x2d, lp["norm_g"])
    h2d = _in_proj(xn, shared["w_in"], layer)
    h3d = h2d.reshape(b, t, N_PROJ)
    (foxkv, nsakv, win, so, fq, fk, fv, nq, nkv, kcmp, vcmp) = _prep(
        h2d, lp["gains"], lp["bias"], lp["wk_b"], lp["wv_b"], layer, depth,
        None if prev is None else (prev["fox_kv"], prev["nsa_kv"]))
    so3 = so.reshape(b, t, 128)
    fq3, fk3, fv3 = fq.reshape(b, t, MIX_W), fk.reshape(b, t, MIX_W), fv.reshape(b, t, MIX_W)
    nq3 = nq.reshape(b, t, MIX_W)
    nkv3 = nkv.reshape(b, t, 768)
    chunk = min(64, t)
    carried = dict(fox_kv=foxkv, nsa_kv=nsakv)
    if is_prompt:
        lf_t = so3[:, :, SM_AF:SM_AF + N_HEADS].transpose(0, 2, 1).reshape(b * N_HEADS, t)
        f_row = _cumsum_rows(lf_t).reshape(b, N_HEADS, t)
        f_col = f_row.transpose(0, 2, 1)
        o_a = _fox_prompt(fq3, fk3, fv3, f_col, f_row)
        win_pad = jnp.pad(nkv3[:, :, 4 * HD:], ((0, 0), (WINDOW, 0), (0, 0)))
        o_d = _nsa_prompt(nq3, kcmp.reshape(b, t // CMP_BLOCK, HD), vcmp.reshape(b, t // CMP_BLOCK, HD),
                          nkv3, win_pad, so3)
        win_out = win.reshape(b, t, 2, HD)[:, -min(WINDOW, t):]
        bb = min(BATCH_UNROLL, b)
    else:
        o_a, o_d, win_out = _sample_attn(
            layer, depth, None if prev is None else (prev["win"],), shared["page_table"],
            fq3, fk3, fv3, nq3, nkv3, win.reshape(b, 2 * t, HD), so3,
            shared["fox_kv4"], shared["fox_lf4"], shared["nsa_kv4"], shared["nsa_win4"],
            lp["wk_b"], lp["wv_b"], lp["kcg"])
        bb = 2 * BATCH_UNROLL
    o_b, c_new, n_new, m_new = _mlstm(
        h3d, so3, lp["mlstm_hg"], st["C"], st["n"], st["m"], st["layer"], layer, depth,
        None if prev is None else (prev["C"], prev["n"], prev["m"]), bb, chunk)
    o_c, s_new = _gla(h3d, so3, lp["w2pad"], lp["gla_b2"], lp["gla_hg"], st["S"], st["layer"], layer, depth,
                      None if prev is None else (prev["S"],), bb, chunk)
    y2d, xn_next = _out_proj(x2d, o_a.reshape(m, MIX_W), o_b.reshape(m, MIX_W), o_c.reshape(m, MIX_W),
                             o_d.reshape(m, MIX_W), h2d, shared["w_branch"], shared["w_out"], layer,
                             lp["next_norm_g"])
    carried.update(win=win_out, C=c_new, n=n_new, m=m_new, S=s_new)
    return y2d.reshape(b, t, D_MODEL), xn_next, carried, so3[:, :, SM_AF:SM_AF + N_HEADS]


def kernel(x_prompt, x_sample, cache_fox_kv, cache_fox_logf, cache_nsa_kv, page_table, state_nsa_win, state_mlstm_C, state_mlstm_n, state_mlstm_m, state_gla_S, norm_g, w_in, fox_qg, fox_kg, fox_bf, mlstm_bi, mlstm_bf, mlstm_hg, gla_w2, gla_b2, gla_hg, nsa_qg, nsa_kcg, nsa_ksg, nsa_kwg, nsa_wk, nsa_wv, nsa_bg, w_branch, w_out):
    params = dict(norm_g=norm_g, fox_qg=fox_qg, fox_kg=fox_kg, fox_bf=fox_bf, mlstm_bi=mlstm_bi,
                  mlstm_bf=mlstm_bf, mlstm_hg=mlstm_hg, gla_w2=gla_w2, gla_b2=gla_b2, gla_hg=gla_hg,
                  nsa_qg=nsa_qg, nsa_kcg=nsa_kcg, nsa_ksg=nsa_ksg, nsa_kwg=nsa_kwg, nsa_wk=nsa_wk,
                  nsa_wv=nsa_wv, nsa_bg=nsa_bg)
    depth = w_in.shape[0]
    bp, tp, _ = x_prompt.shape
    db, ts, _ = x_sample.shape
    n_pool = cache_fox_kv.shape[1]
    w_buf = state_nsa_win.shape[2]
    shared = dict(
        w_in=_permute_w_in(w_in), w_branch=(0.5 * w_branch).astype(BF16), w_out=w_out.astype(BF16),
        page_table=page_table,
        fox_kv4=cache_fox_kv.reshape(depth, n_pool, PAGE_SIZE * 2 * N_HEADS, HD),
        fox_lf4=cache_fox_logf.transpose(0, 1, 3, 2),
        nsa_kv4=cache_nsa_kv.reshape(depth, n_pool, PAGE_SIZE * 4, HD),
        nsa_win4=state_nsa_win.reshape(depth, db, w_buf * 2, HD))
    st_p = dict(C=jnp.zeros((1, bp, N_HEADS, HD, HD), F32), n=jnp.zeros((1, bp, N_HEADS, HD), F32),
                m=jnp.zeros((1, bp, N_HEADS, HD), F32), S=jnp.zeros((1, bp, N_HEADS, GLA_DK, GLA_DV), F32),
                layer=0)
    m_s = jnp.broadcast_to(state_mlstm_m[..., None], state_mlstm_m.shape + (HD,))
    y_p, y_s = x_prompt, x_sample
    xn_p = xn_s = None
    prev_p = prev_s = None
    lf_p, lf_s, win_p = [], [], []
    for l in range(depth):
        lp = _layer_params(l, params)
        lp["next_norm_g"] = norm_g[min(l + 1, depth - 1)]
        st_s = dict(C=state_mlstm_C, n=state_mlstm_n, m=m_s, S=state_gla_S, layer=l)
        y_p, xn_p, prev_p, lf = _layer(y_p, xn_p, lp, shared, st_p, prev_p, is_prompt=True, layer=l, depth=depth)
        lf_p.append(lf)
        win_p.append(prev_p["win"])
        y_s, xn_s, prev_s, lf = _layer(y_s, xn_s, lp, shared, st_s, prev_s, is_prompt=False, layer=l, depth=depth)
        lf_s.append(lf)

    def finish(c, b, t, win, lf):
        return (c["fox_kv"].reshape(depth, b, t, 2, N_HEADS, HD), jnp.stack(lf, axis=0),
                c["nsa_kv"].reshape(depth, b, t, 4, HD), win,
                c["C"], c["n"], c["m"][..., 0], c["S"])

    out_p = finish(prev_p, bp, tp, jnp.stack(win_p, axis=0), lf_p)
    out_s = finish(prev_s, db, ts, prev_s["win"].reshape(depth, db, w_buf, 2, HD), lf_s)
    return (y_p, y_s) + out_p + out_s
```

```python
import functools

import jax
import jax.numpy as jnp
from jax import lax
from jax.experimental import pallas as pl
from jax.experimental.pallas import tpu as pltpu

F32 = jnp.float32
BF16 = jnp.bfloat16
HIGHEST = lax.Precision.HIGHEST

D_MODEL = 2048
MIX_W = 512
HD = 128
N_HEADS = 4
GLA_DK = 64
GLA_DK_SHIFT = 6
GLA_DV = 128
GLA_RANK = 16
GLA_TAU = 16.0
CMP_BLOCK = 32
SEL_BLOCK = 64
SEL_SHIFT = 6
N_SEL = 16
WINDOW = 512
PAGE_SIZE = 128
EPS = 1e-6
ATT_SCALE = HD ** -0.5
LOG2E = 1.4426950408889634
Q_PRESCALE = ATT_SCALE * LOG2E
NEG_INF = float("-inf")
KEY_CLASS = 256
BF16_ROWS = 16

_REF_LAYOUT = (
    ("a_q", 512), ("a_k", 512), ("a_v", 512), ("a_f", 4), ("a_z", 512),
    ("b_q", 512), ("b_k", 512), ("b_v", 512), ("b_i", 4), ("b_f", 4), ("b_o", 512), ("b_z", 512),
    ("c_q", 256), ("c_k", 256), ("c_v", 512), ("c_a", 16), ("c_z", 512),
    ("d_q", 512), ("d_kv", 768), ("d_g", 12), ("d_z", 512),
    ("gate", 8192),
)
N_REF = sum(w for _, w in _REF_LAYOUT)
_MY_LAYOUT = (
    "a_z", "b_z", "c_z", "d_z", "gate",
    "a_q", "a_k", "a_v", "d_q", "b_q", "b_k", "b_v", "b_o", "c_v", "c_q", "c_k", "d_kv",
    "a_f", "b_i", "b_f", "c_a", "d_g",
)
N_PROJ = 16384
COL_Z = 0
COL_GATE = 2048
COL_AQ, COL_AK, COL_AV, COL_DQ = 10240, 10752, 11264, 11776
COL_BQ, COL_BK, COL_BV, COL_BO = 12288, 12800, 13312, 13824
COL_CV, COL_CQ, COL_CK, COL_DKV, COL_SMALL = 14336, 14848, 15104, 15360, 16128
SM_AF, SM_BI, SM_BF, SM_CA, SM_DG = 0, 4, 8, 12, 28


def _cparams(sem, vmem_mb):
    return pltpu.CompilerParams(dimension_semantics=sem, vmem_limit_bytes=vmem_mb * 1024 * 1024)


def _log_sigmoid(x):
    return jnp.minimum(x, 0.0) - jnp.log(1.0 + jnp.exp(-jnp.abs(x)))


def _sigmoid(x):
    return 0.5 * jnp.tanh(0.5 * x) + 0.5


def _rms_lanes(x, g):
    return (x * lax.rsqrt(jnp.mean(x * x, axis=-1, keepdims=True) + EPS)) * g


def _dot(a, b):
    return jnp.dot(a, b, preferred_element_type=F32)


def _dot_nt(a, b):
    return lax.dot_general(a, b, (((1,), (1,)), ((), ())), preferred_element_type=F32)


def _dot_tn(a, b):
    return lax.dot_general(a, b, (((0,), (0,)), ((), ())), preferred_element_type=F32)


def _dot_exact(a, b):
    return jnp.dot(a, b, precision=HIGHEST, preferred_element_type=F32)


def _eye_mask(n):
    return lax.broadcasted_iota(jnp.int32, (n, n), 0) == lax.broadcasted_iota(jnp.int32, (n, n), 1)


def _col_to_row(c):
    n = c.shape[0]
    return jnp.sum(jnp.where(_eye_mask(n), c, 0.0), axis=0, keepdims=True)


def _row_to_col(r):
    n = r.shape[1]
    return jnp.sum(jnp.where(_eye_mask(n), r, 0.0), axis=1, keepdims=True)


def _tril_ones(n):
    r = lax.broadcasted_iota(jnp.int32, (n, n), 0)
    c = lax.broadcasted_iota(jnp.int32, (n, n), 1)
    return jnp.where(c <= r, 1.0, 0.0).astype(F32)


def _cumsum_lanes(x):
    n = x.shape[-1]
    lane = lax.broadcasted_iota(jnp.int32, x.shape, x.ndim - 1)
    s = 1
    while s < n:
        x = x + jnp.where(lane >= s, pltpu.roll(x, s, axis=x.ndim - 1), 0.0)
        s *= 2
    return x


def _pad_rows(x, rows):
    if x.shape[0] == rows:
        return x
    return jnp.concatenate([x, jnp.zeros((rows - x.shape[0], x.shape[1]), x.dtype)], axis=0)


def _softmax_parts(s_list, mask_list):
    masked = [jnp.where(mk, s, NEG_INF) for s, mk in zip(s_list, mask_list)]
    m = masked[0].max(axis=-1, keepdims=True)
    for s in masked[1:]:
        m = jnp.maximum(m, s.max(axis=-1, keepdims=True))
    m = jnp.where(m > NEG_INF, m, 0.0)
    es = [jnp.exp2(s - m) for s in masked]
    tot = es[0].sum(axis=-1, keepdims=True)
    for e in es[1:]:
        tot = tot + e.sum(axis=-1, keepdims=True)
    return es, 1.0 / jnp.maximum(tot, 1e-30)


def _with_carried_outputs(kernel_fn, n_in, n_carried):
    if not n_carried:
        return kernel_fn

    def wrapped(*refs):
        return kernel_fn(*refs[:n_in], *refs[n_in + n_carried:])

    return wrapped


def _carry_args(prev, first_in_index, out_indices):
    if prev is None:
        return [], [], {}
    specs = [pl.BlockSpec(memory_space=pl.ANY) for _ in prev]
    aliases = {first_in_index + k: oi for k, oi in enumerate(out_indices)}
    return specs, list(prev), aliases


def _segments():
    offs = {}
    off = 0
    for name, width in _REF_LAYOUT:
        offs[name] = (off, width)
        off += width
    segs = []
    dst = 0
    for name in _MY_LAYOUT:
        src, width = offs[name]
        segs.append((src, dst, width))
        dst += width
    return segs, dst


def _permute_kernel(wt_ref, o_ref, small_ref):
    segs, _ = _segments()
    for src, dst, width in segs:
        if width >= 128:
            w = wt_ref[src:src + width, :]
            if dst == COL_GATE:
                w = w * 0.5
            o_ref[dst:dst + width, :] = w.astype(BF16)
    small_ref[...] = jnp.zeros(small_ref.shape, F32)
    for src, dst, width in segs:
        if width < 128:
            small_ref[dst - COL_SMALL:dst - COL_SMALL + width, :] = wt_ref[src:src + width, :]
    o_ref[COL_SMALL:N_PROJ, :] = small_ref[...].astype(BF16)


def _permute_w_in(w_in):
    depth, d, n = w_in.shape
    tk = 128
    wt = jnp.transpose(w_in, (0, 2, 1))
    return pl.pallas_call(
        _permute_kernel,
        grid=(depth, d // tk),
        in_specs=[pl.BlockSpec((None, n, tk), lambda l, i: (l, 0, i))],
        out_specs=pl.BlockSpec((None, N_PROJ, tk), lambda l, i: (l, 0, i)),
        out_shape=jax.ShapeDtypeStruct((depth, N_PROJ, d), BF16),
        scratch_shapes=[pltpu.VMEM((N_PROJ - COL_SMALL, tk), F32)],
        compiler_params=_cparams(("parallel", "parallel"), 40),
        name="permute_w_in",
    )(wt)


def _norm_kernel(x_ref, g_ref, o_ref):
    o_ref[...] = _rms_lanes(x_ref[...], g_ref[...]).astype(o_ref.dtype)


def _rmsnorm(x2d, g):
    m, d = x2d.shape
    tm = min(512, m)
    return pl.pallas_call(
        _norm_kernel,
        grid=(m // tm,),
        in_specs=[pl.BlockSpec((tm, d), lambda i: (i, 0)), pl.BlockSpec((1, d), lambda i: (0, 0))],
        out_specs=pl.BlockSpec((tm, d), lambda i: (i, 0)),
        out_shape=jax.ShapeDtypeStruct((m, d), BF16),
        compiler_params=_cparams(("parallel",), 32),
        name="rmsnorm",
    )(x2d, g.reshape(1, d))


def _mm_kernel(a_ref, bt_ref, o_ref):
    o_ref[...] = _dot_nt(a_ref[...], bt_ref[...])


def _in_proj(a, wt_all, layer):
    m, k = a.shape
    n = wt_all.shape[1]
    tm, tn = min(1024, m), 1024
    return pl.pallas_call(
        _mm_kernel,
        grid=(m // tm, n // tn),
        in_specs=[pl.BlockSpec((tm, k), lambda i, j: (i, 0)),
                  pl.BlockSpec((None, tn, k), lambda i, j: (layer, j, 0))],
        out_specs=pl.BlockSpec((tm, tn), lambda i, j: (i, j)),
        out_shape=jax.ShapeDtypeStruct((m, n), F32),
        compiler_params=_cparams(("parallel", "arbitrary"), 48),
        name="in_proj",
    )(a, wt_all)


PREP_TM = 256
N_PREP_IN = 10


def _prep_kernel(aq_ref, ak_ref, av_ref, dq_ref, dkv_ref, sm_ref, gains_ref, bias_ref, wk_ref, wv_ref,
                 foxkv_ref, nsakv_ref, win_ref, so_ref, fq_ref, fk_ref, fv_ref, nq_ref, nkv_ref,
                 kcmp_ref, vcmp_ref):
    tm = PREP_TM
    g_fq, g_fk, g_nq = gains_ref[0:1, :], gains_ref[1:2, :], gains_ref[2:3, :]
    g_ks, g_kw, g_kc = gains_ref[3:4, :], gains_ref[4:5, :], gains_ref[5:6, :]
    for h in range(N_HEADS):
        sl = slice(h * HD, (h + 1) * HD)
        fq_ref[:, sl] = (_rms_lanes(aq_ref[:, sl], g_fq) * Q_PRESCALE).astype(BF16)
        kn = _rms_lanes(ak_ref[:, sl], g_fk)
        foxkv_ref[pl.ds(h, tm, stride=2 * N_HEADS), :] = kn
        fk_ref[:, sl] = kn.astype(BF16)
        v = av_ref[:, sl]
        foxkv_ref[pl.ds(N_HEADS + h, tm, stride=2 * N_HEADS), :] = v
        fv_ref[:, sl] = v.astype(BF16)
        nq_ref[:, sl] = (_rms_lanes(dq_ref[:, sl], g_nq) * Q_PRESCALE).astype(BF16)
    kc = dkv_ref[:, 0:128]
    vc = dkv_ref[:, 128:256]
    ks = _rms_lanes(dkv_ref[:, 256:384], g_ks)
    vs = dkv_ref[:, 384:512]
    kw = _rms_lanes(dkv_ref[:, 512:640], g_kw)
    vw = dkv_ref[:, 640:768]
    for j, a in enumerate((kc, vc, ks, vs)):
        nsakv_ref[pl.ds(j, tm, stride=4), :] = a
    win_ref[pl.ds(0, tm, stride=2), :] = kw
    win_ref[pl.ds(1, tm, stride=2), :] = vw
    for j, a in enumerate((kc, vc, ks, vs, kw, vw)):
        nkv_ref[:, j * HD:(j + 1) * HD] = a.astype(BF16)
    nb = tm // CMP_BLOCK
    kcs = jnp.sum(kc.reshape(nb, CMP_BLOCK, HD) * wk_ref[...][None], axis=1)
    kcmp_ref[...] = _rms_lanes(kcs, g_kc)
    vcmp_ref[...] = jnp.sum(vc.reshape(nb, CMP_BLOCK, HD) * wv_ref[...][None], axis=1)
    y = sm_ref[...] + bias_ref[...]
    lane = lax.broadcasted_iota(jnp.int32, y.shape, 1)
    is_ls = (lane < SM_BI) | ((lane >= SM_BF) & (lane < SM_CA))
    is_sg = (lane >= SM_DG) & (lane < SM_DG + 12)
    so_ref[...] = jnp.where(is_ls, _log_sigmoid(y), jnp.where(is_sg, _sigmoid(y), y))


def _prep(h2d, gains, bias, wk_b, wv_b, layer, depth, prev):
    m = h2d.shape[0]
    tm = PREP_TM

    def col(width, off):
        blk = off // width
        return pl.BlockSpec((tm, width), lambda i, blk=blk: (i, blk))

    def full(shape):
        return pl.BlockSpec(shape, lambda i: (0, 0))

    def rows(width):
        return pl.BlockSpec((tm, width), lambda i: (i, 0))

    def stacked(slots):
        return pl.BlockSpec((None, tm * slots, HD), lambda i: (layer, i, 0))

    out_shapes = (
        jax.ShapeDtypeStruct((depth, m * 8, HD), F32),
        jax.ShapeDtypeStruct((depth, m * 4, HD), F32),
        jax.ShapeDtypeStruct((m * 2, HD), F32),
        jax.ShapeDtypeStruct((m, 128), F32),
        jax.ShapeDtypeStruct((m, 512), BF16),
        jax.ShapeDtypeStruct((m, 512), BF16),
        jax.ShapeDtypeStruct((m, 512), BF16),
        jax.ShapeDtypeStruct((m, 512), BF16),
        jax.ShapeDtypeStruct((m, 768), BF16),
        jax.ShapeDtypeStruct((m // CMP_BLOCK, 128), F32),
        jax.ShapeDtypeStruct((m // CMP_BLOCK, 128), F32),
    )
    out_specs = (stacked(8), stacked(4), pl.BlockSpec((tm * 2, HD), lambda i: (i, 0)),
                 rows(128), rows(512), rows(512), rows(512), rows(512), rows(768),
                 pl.BlockSpec((tm // CMP_BLOCK, 128), lambda i: (i, 0)),
                 pl.BlockSpec((tm // CMP_BLOCK, 128), lambda i: (i, 0)))
    c_specs, c_args, aliases = _carry_args(prev, N_PREP_IN, (0, 1))
    return pl.pallas_call(
        _with_carried_outputs(_prep_kernel, N_PREP_IN, len(c_args)),
        grid=(m // tm,),
        in_specs=[col(512, COL_AQ), col(512, COL_AK), col(512, COL_AV), col(512, COL_DQ),
                  col(768, COL_DKV), col(128, COL_SMALL),
                  full((8, 128)), full((1, 128)), full((CMP_BLOCK, 128)), full((CMP_BLOCK, 128))] + c_specs,
        out_specs=out_specs,
        out_shape=out_shapes,
        input_output_aliases=aliases,
        compiler_params=_cparams(("parallel",), 32),
        name="prep",
    )(h2d, h2d, h2d, h2d, h2d, h2d, gains, bias, wk_b, wv_b, *c_args)


def _cumsum_rows_kernel(x_ref, o_ref):
    o_ref[...] = _cumsum_lanes(x_ref[...])


def _cumsum_rows(x):
    r, n = x.shape
    return pl.pallas_call(
        _cumsum_rows_kernel,
        grid=(1,),
        in_specs=[pl.BlockSpec((r, n), lambda i: (0, 0))],
        out_specs=pl.BlockSpec((r, n), lambda i: (0, 0)),
        out_shape=jax.ShapeDtypeStruct((r, n), F32),
        name="fox_cumsum",
    )(x)


FOX_TQ = 256


def _fox_prompt_kernel(q_ref, k_ref, v_ref, fc_ref, fr_ref, o_ref, *, t_len):
    tq = FOX_TQ
    i = pl.program_id(1)
    n_cls = -(-t_len // KEY_CLASS)
    cls = (i * tq + tq - 1) // KEY_CLASS

    def run(kl):
        qpos = i * tq + lax.broadcasted_iota(jnp.int32, (tq, kl), 0)
        kpos = lax.broadcasted_iota(jnp.int32, (tq, kl), 1)
        mask = kpos <= qpos
        for h in range(N_HEADS):
            sl = slice(h * HD, (h + 1) * HD)
            s = _dot_nt(q_ref[:, sl], k_ref[0:kl, sl])
            s = s + fc_ref[:, h:h + 1] * LOG2E - fr_ref[h:h + 1, 0:kl] * LOG2E
            (e,), inv = _softmax_parts([s], [mask])
            o_ref[:, sl] = _dot(e.astype(BF16), v_ref[0:kl, sl]) * inv

    for c in range(n_cls):
        pl.when(cls == c)(functools.partial(run, min((c + 1) * KEY_CLASS, t_len)))


def _fox_prompt(fq, fk, fv, f_col, f_row):
    b, t, _ = fq.shape
    tq = FOX_TQ
    return pl.pallas_call(
        functools.partial(_fox_prompt_kernel, t_len=t),
        grid=(b, t // tq),
        in_specs=[pl.BlockSpec((None, tq, MIX_W), lambda bi, i: (bi, i, 0)),
                  pl.BlockSpec((None, t, MIX_W), lambda bi, i: (bi, 0, 0)),
                  pl.BlockSpec((None, t, MIX_W), lambda bi, i: (bi, 0, 0)),
                  pl.BlockSpec((None, tq, N_HEADS), lambda bi, i: (bi, i, 0)),
                  pl.BlockSpec((None, N_HEADS, t), lambda bi, i: (bi, 0, 0))],
        out_specs=pl.BlockSpec((None, tq, MIX_W), lambda bi, i: (bi, i, 0)),
        out_shape=jax.ShapeDtypeStruct((b, t, MIX_W), F32),
        compiler_params=_cparams(("parallel", "arbitrary"), 48),
        name="fox_prompt",
    )(fq, fk, fv, f_col, f_row)


PAGE_GROUP = 2


def _fox_sample_stages(q_ref, kn_ref, vn_ref, so_ref, spread_ref, kv_pages, lf_pages, o_ref, lf_scr,
                       n_pages, t_new):
    past = n_pages * PAGE_SIZE
    w2 = 2 * PAGE_SIZE
    n_rows = n_pages * N_HEADS
    heads = range(N_HEADS)
    sls = [slice(h * HD, (h + 1) * HD) for h in heads]
    st = {"qk": {}}

    def kv_rows(p, h):
        return kv_pages[p][pl.ds(h, w2, stride=N_HEADS), :].astype(BF16)

    def prefix_sums():
        for p in range(n_pages):
            lf_scr[p * N_HEADS:(p + 1) * N_HEADS, :] = lf_pages[p][...]
        lf_rows = lf_scr[...]
        local = _dot_exact(lf_rows, spread_ref[...])
        tot = _dot_exact(lf_rows, jnp.ones((PAGE_SIZE, 128), F32))
        rr = lax.broadcasted_iota(jnp.int32, (n_rows, n_rows), 0)
        cc = lax.broadcasted_iota(jnp.int32, (n_rows, n_rows), 1)
        same_head = (rr & (N_HEADS - 1)) == (cc & (N_HEADS - 1))
        earlier = jnp.where(same_head & (cc < rr), 1.0, 0.0)
        up_to = jnp.where(same_head & (cc <= rr), 1.0, 0.0)
        st["f_int"] = (local + _dot_exact(earlier, tot)[:, 0:1]) * LOG2E
        st["f_total"] = _dot_exact(up_to, tot)[n_rows - N_HEADS:n_rows, 0:1]
        st["f_new"] = _dot_exact(_tril_ones(t_new), so_ref[...])
        st["qs"] = [q_ref[:, sl].astype(F32) for sl in sls]

    def pair_tile(p, pr):
        return jnp.concatenate([kv_rows(p, pr[0]), kv_rows(p, pr[1])], axis=1)

    def scores(pr):
        def run():
            zero = jnp.zeros((t_new, HD), F32)
            q_pair = jnp.concatenate(
                [jnp.concatenate([st["qs"][pr[0]], zero], axis=1),
                 jnp.concatenate([zero, st["qs"][pr[1]]], axis=1)], axis=0).astype(BF16)
            both = [_dot_nt(q_pair, pair_tile(p, pr)) for p in range(n_pages)]
            st["qk"][pr[0]] = [x[0:t_new] for x in both]
            st["qk"][pr[1]] = [x[t_new:2 * t_new] for x in both]
        return run

    def softmax():
        lane = lax.broadcasted_iota(jnp.int32, (t_new, w2), 1)
        is_key = (lane & 1) == 0
        kpos0 = lane >> 1
        tpos = lax.broadcasted_iota(jnp.int32, (t_new, w2), 0) + past
        rn = lax.broadcasted_iota(jnp.int32, (t_new, t_new), 0)
        cn = lax.broadcasted_iota(jnp.int32, (t_new, t_new), 1)
        m_list = [is_key & (kpos0 + p * PAGE_SIZE <= tpos) for p in range(n_pages)] + [cn <= rn]
        qk_new = [_dot_nt(st["qs"][h], kn_ref[:, sls[h]].astype(F32)) for h in heads]
        st["w"] = []
        for h in heads:
            fq = (st["f_new"][:, SM_AF + h:SM_AF + h + 1] + st["f_total"][h:h + 1, :]) * LOG2E
            s_list = [st["qk"][h][p] + fq - st["f_int"][p * N_HEADS + h:p * N_HEADS + h + 1, :]
                      for p in range(n_pages)]
            s_list.append(qk_new[h] + fq - _col_to_row(fq))
            st["w"].append(_softmax_parts(s_list, m_list))

    def values(pr):
        def run():
            (es0, inv0), (es1, inv1) = st["w"][pr[0]], st["w"][pr[1]]
            acc = None
            for p in range(n_pages):
                w_pair = jnp.concatenate([pltpu.roll(es0[p], 1, axis=1), pltpu.roll(es1[p], 1, axis=1)], axis=0)
                part = _dot(w_pair.astype(BF16), pair_tile(p, pr))
                acc = part if acc is None else acc + part
            o0 = acc[0:t_new, 0:HD] + _dot(es0[n_pages], vn_ref[:, sls[pr[0]]].astype(F32))
            o1 = acc[t_new:2 * t_new, HD:2 * HD] + _dot(es1[n_pages], vn_ref[:, sls[pr[1]]].astype(F32))
            o_ref[:, sls[pr[0]]] = o0 * inv0
            o_ref[:, sls[pr[1]]] = o1 * inv1
        return run

    pairs = [(h, h + 1) for h in range(0, N_HEADS, 2)]
    return [prefix_sums, scores(pairs[0]), scores(pairs[1]), softmax, values(pairs[0]), values(pairs[1])]


def _spread_matrix():
    r = jnp.arange(PAGE_SIZE)[:, None]
    c = jnp.arange(2 * PAGE_SIZE)[None, :]
    return (2 * r <= c).astype(F32)


N_MLSTM_IN = 9
BATCH_UNROLL = 4
OUTER_SUM_MIN_CHUNK = 64


def _mlstm_kernel(q_ref, k_ref, v_ref, og_ref, so_ref, hg_ref, c0_ref, n0_ref, m0_ref,
                  o_ref, c_ref, n_ref, m_ref, *, bb, chunk, wide_chains):
    L = chunk
    Lp = max(L, BF16_ROWS)

    @pl.when(pl.program_id(1) == 0)
    def _():
        c_ref[...] = c0_ref[...]
        n_ref[...] = n0_ref[...]
        m_ref[...] = m0_ref[...]

    tril = _tril_ones(Lp)
    causal = tril > 0.5
    hg = hg_ref[...]
    lane = lax.broadcasted_iota(jnp.int32, (Lp - L, 128), 1) if Lp > L else None

    def run_chains(idx):
        so_of, csum_of = [], []
        for bi in idx:
            so = so_ref[bi]
            if Lp > L:
                pad = jnp.where((lane >= SM_BI) & (lane < SM_BF), -1e30, 0.0)
                so = jnp.concatenate([so, pad], axis=0)
            so_of.append(so)
            csum_of.append(_dot_exact(tril, so))
        ch = [(bi, h) for bi in idx for h in range(N_HEADS)]
        seq = [u for u in range(len(idx)) for _ in range(N_HEADS)]
        n = range(len(ch))
        sls = [slice(h * HD, (h + 1) * HD) for _, h in ch]
        q = [_pad_rows(q_ref[bi, :, sls[c]], Lp) for c, (bi, _) in enumerate(ch)]
        k = [_pad_rows(k_ref[bi, :, sls[c]], Lp) * ATT_SCALE for c, (bi, _) in enumerate(ch)]
        vb = [_pad_rows(v_ref[bi, :, sls[c]], Lp).astype(BF16) for c, (bi, _) in enumerate(ch)]
        qb = [x.astype(BF16) for x in q]
        ig = [so_of[seq[c]][:, SM_BI + h:SM_BI + h + 1] for c, (_, h) in enumerate(ch)]
        bcol = [csum_of[seq[c]][:, SM_BF + h:SM_BF + h + 1] for c, (_, h) in enumerate(ch)]
        c_st = [c_ref[bi, h] for bi, h in ch]
        n_st = [n_ref[bi, h:h + 1, :] for bi, h in ch]
        m_st = [m_ref[bi, h:h + 1, 0:1] for bi, h in ch]
        qk_raw = [_dot_nt(qb[c], k[c].astype(BF16)) for c in n]
        q_c = [_dot(qb[c], c_st[c].astype(BF16)) for c in n]
        lane_g = lax.broadcasted_iota(jnp.int32, (Lp, 128), 1)
        dmat = []
        for c, (_, h) in enumerate(ch):
            if L < OUTER_SUM_MIN_CHUNK:
                dmat.append(jnp.where(causal, bcol[c] - _col_to_row(bcol[c]) + _col_to_row(ig[c]), NEG_INF))
                continue
            at_b, at_i = lane_g == SM_BF + h, lane_g == SM_BI + h
            cs, so_c = csum_of[seq[c]], so_of[seq[c]]
            left = jnp.concatenate([jnp.where(at_b, cs, 0.0), jnp.where(at_b | at_i, 1.0, 0.0)], axis=1)
            right = jnp.concatenate([jnp.where(at_b, 1.0, 0.0),
                                     jnp.where(at_i, so_c, 0.0) - jnp.where(at_b, cs, 0.0)], axis=1)
            outer = lax.dot_general(left, right, (((1,), (1,)), ((), ())), precision=HIGHEST,
                                    preferred_element_type=F32)
            dmat.append(jnp.where(causal, outer, NEG_INF))
        inter = [bcol[c] + m_st[c] for c in n]
        m_t = [jnp.maximum(inter[c], dmat[c].max(axis=-1, keepdims=True)) for c in n]
        w_inter = [jnp.exp(inter[c] - m_t[c]) for c in n]
        qk = [qk_raw[c] * jnp.exp(dmat[c] - m_t[c]) for c in n]
        qk_v = [_dot(qk[c].astype(BF16), vb[c]) for c in n]
        b_last = [bcol[c][Lp - 1:Lp, :] for c in n]
        g_end = [b_last[c] - bcol[c] + ig[c] for c in n]
        m_new = [jnp.maximum(b_last[c] + m_st[c], g_end[c].max(axis=0, keepdims=True)) for c in n]
        a_prev = [jnp.exp(b_last[c] + m_st[c] - m_new[c]) for c in n]
        kw = [k[c] * jnp.exp(g_end[c] - m_new[c]) for c in n]
        k_v = [_dot_tn(kw[c].astype(BF16), vb[c]) for c in n]
        new_states = []
        for c, (bi, h) in enumerate(ch):
            num = w_inter[c] * q_c[c] + qk_v[c]
            den = (w_inter[c] * jnp.sum(q[c] * n_st[c], axis=-1, keepdims=True)
                   + jnp.sum(qk[c], axis=-1, keepdims=True))
            hout = num / jnp.maximum(jnp.abs(den), jnp.exp(-m_t[c]))
            new_states.append((a_prev[c] * c_st[c] + k_v[c],
                               a_prev[c] * n_st[c] + jnp.sum(kw[c], axis=0, keepdims=True),
                               jnp.broadcast_to(m_new[c], (1, HD))))
            hn = _rms_lanes(hout[0:L], hg)
            o_ref[bi, :, sls[c]] = hn * _sigmoid(og_ref[bi, :, sls[c]])
        for (bi, h), (c_new, n_new, m_nw) in zip(ch, new_states):
            c_ref[bi, h] = c_new
            n_ref[bi, h:h + 1, :] = n_new
            m_ref[bi, h:h + 1, :] = m_nw

    unroll = min(BATCH_UNROLL, bb)
    wide = min(wide_chains, unroll)

    def group(g, carry):
        for u0 in range(0, unroll, wide):
            run_chains([g * unroll + u0 + u for u in range(wide)])
        return carry

    if bb == unroll:
        group(0, 0)
    else:
        lax.fori_loop(0, bb // unroll, group, 0)


def _mlstm(h3d, so3d, hg, c0, n0, m0, state_layer, layer, depth, prev, bb, chunk):
    b, t, _ = h3d.shape
    nc = t // chunk
    wide_chains = 4 if chunk >= 64 else 2

    def col(off):
        blk = off // MIX_W
        return pl.BlockSpec((bb, chunk, MIX_W), lambda bi, c, blk=blk: (bi, c, blk))

    def st_c(li):
        return pl.BlockSpec((None, bb, N_HEADS, HD, HD), lambda bi, c: (li, bi, 0, 0, 0))

    def st_n(li):
        return pl.BlockSpec((None, bb, N_HEADS, HD), lambda bi, c: (li, bi, 0, 0))

    c_specs, c_args, aliases = _carry_args(prev, N_MLSTM_IN, (1, 2, 3))
    return pl.pallas_call(
        _with_carried_outputs(functools.partial(_mlstm_kernel, bb=bb, chunk=chunk, wide_chains=wide_chains),
                              N_MLSTM_IN, len(c_args)),
        grid=(b // bb, nc),
        in_specs=[col(COL_BQ), col(COL_BK), col(COL_BV), col(COL_BO),
                  pl.BlockSpec((bb, chunk, 128), lambda bi, c: (bi, c, 0)),
                  pl.BlockSpec((1, HD), lambda bi, c: (0, 0)),
                  st_c(state_layer), st_n(state_layer), st_n(state_layer)] + c_specs,
        out_specs=(pl.BlockSpec((bb, chunk, MIX_W), lambda bi, c: (bi, c, 0)),
                   st_c(layer), st_n(layer), st_n(layer)),
        out_shape=(jax.ShapeDtypeStruct((b, t, MIX_W), F32),
                   jax.ShapeDtypeStruct((depth, b, N_HEADS, HD, HD), F32),
                   jax.ShapeDtypeStruct((depth, b, N_HEADS, HD), F32),
                   jax.ShapeDtypeStruct((depth, b, N_HEADS, HD), F32)),
        input_output_aliases=aliases,
        compiler_params=_cparams(("parallel", "arbitrary"), 40),
        name="mlstm",
    )(h3d, h3d, h3d, h3d, so3d, hg.reshape(1, HD), c0, n0, m0, *c_args)


N_GLA_IN = 8
GLA_SUB = 16


def _gla_kernel(q_ref, k_ref, v_ref, so_ref, w2_ref, b2_ref, hg_ref, s0_ref, o_ref, s_ref, *, bb, chunk,
                wide_seqs):
    L = chunk
    Lp = max(L, BF16_ROWS)
    sub = GLA_SUB

    @pl.when(pl.program_id(1) == 0)
    def _():
        s_ref[...] = s0_ref[...]

    tril = _tril_ones(Lp)
    hg = hg_ref[...]
    w2 = w2_ref[...].astype(BF16)
    b2 = b2_ref[...]
    q_scale = GLA_DK ** -0.5
    real_row = lax.broadcasted_iota(jnp.int32, (Lp, 1), 0) < L

    heads = range(N_HEADS)
    dkw = N_HEADS * GLA_DK
    head_of_lane = lax.broadcasted_iota(jnp.int32, (1, dkw), 1) >> GLA_DK_SHIFT
    zero_blk = jnp.zeros((GLA_DK, GLA_DV), F32)

    def run_seqs(idx):
        n = range(len(idx))
        s_old = [[s_ref[bi, h] for h in heads] for bi in idx]
        pre = [_dot(_pad_rows(so_ref[bi], Lp).astype(BF16), w2) + b2 for bi in idx]
        la = [jnp.where(real_row, _log_sigmoid(pre[u]) / GLA_TAU, 0.0) for u in n]
        bcs = [_dot_exact(tril, la[u]) for u in n]
        q = [_pad_rows(q_ref[bi], Lp) * q_scale for bi in idx]
        k = [_pad_rows(k_ref[bi], Lp) for bi in idx]
        vb = [_pad_rows(v_ref[bi], Lp).astype(BF16) for bi in idx]
        s_bd = [jnp.concatenate(
            [jnp.concatenate([s_old[u][h] if g == h else zero_blk for g in heads], axis=1) for h in heads],
            axis=0) for u in n]
        inter = [_dot((q[u] * jnp.exp(bcs[u])).astype(BF16), s_bd[u].astype(BF16)) for u in n]
        for i in range(Lp // sub):
            r0 = i * sub
            hi = r0 + sub
            rows = min(hi, L) - r0
            rr = (lax.broadcasted_iota(jnp.int32, (N_HEADS * sub, hi), 0) & (sub - 1)) + r0
            cc = lax.broadcasted_iota(jnp.int32, (N_HEADS * sub, hi), 1)
            base = [bcs[u][r0 - 1:r0, :] if i > 0 else jnp.zeros((1, dkw), F32) for u in n]
            qi = [q[u][r0:hi, :] * jnp.exp(bcs[u][r0:hi, :] - base[u]) for u in n]
            ke = [(k[u][0:hi, :] * jnp.exp(base[u] - bcs[u][0:hi, :])).astype(BF16) for u in n]
            q_heads = [jnp.concatenate([jnp.where(head_of_lane == h, qi[u], 0.0) for h in heads], axis=0)
                       for u in n]
            a = [jnp.where(cc <= rr, _dot_nt(q_heads[u].astype(BF16), ke[u]), 0.0).astype(BF16) for u in n]
            for u, bi in enumerate(idx):
                for h in heads:
                    vsl = slice(h * GLA_DV, (h + 1) * GLA_DV)
                    oi = inter[u][r0:hi, vsl] + _dot(a[u][h * sub:(h + 1) * sub, :], vb[u][0:hi, vsl])
                    o_ref[bi, r0:r0 + rows, vsl] = _rms_lanes(oi[0:rows], hg)
        b_end = [bcs[u][Lp - 1:Lp, :] for u in n]
        kd = [(k[u] * jnp.exp(b_end[u] - bcs[u])).astype(BF16) for u in n]
        upd = [_dot_tn(kd[u], vb[u]) for u in n]
        dcol = [_row_to_col(jnp.exp(b_end[u])) for u in n]
        for u, bi in enumerate(idx):
            for h in heads:
                s_ref[bi, h] = (dcol[u][h * GLA_DK:(h + 1) * GLA_DK, :] * s_old[u][h]
                                + upd[u][h * GLA_DK:(h + 1) * GLA_DK, h * GLA_DV:(h + 1) * GLA_DV])

    unroll = min(BATCH_UNROLL, bb)
    wide = min(wide_seqs, unroll)

    def group(g, carry):
        for u0 in range(0, unroll, wide):
            run_seqs([g * unroll + u0 + u for u in range(wide)])
        return carry

    if bb == unroll:
        group(0, 0)
    else:
        lax.fori_loop(0, bb // unroll, group, 0)


def _gla(h3d, so3d, w2pad, b2, hg, s0, state_layer, layer, depth, prev, bb, chunk):
    b, t, _ = h3d.shape
    nc = t // chunk

    def st(li):
        return pl.BlockSpec((None, bb, N_HEADS, GLA_DK, GLA_DV), lambda bi, c: (li, bi, 0, 0, 0))

    c_specs, c_args, aliases = _carry_args(prev, N_GLA_IN, (1,))
    return pl.pallas_call(
        _with_carried_outputs(functools.partial(_gla_kernel, bb=bb, chunk=chunk, wide_seqs=4),
                              N_GLA_IN, len(c_args)),
        grid=(b // bb, nc),
        in_specs=[pl.BlockSpec((bb, chunk, 256), lambda bi, c: (bi, c, COL_CQ // 256)),
                  pl.BlockSpec((bb, chunk, 256), lambda bi, c: (bi, c, COL_CK // 256)),
                  pl.BlockSpec((bb, chunk, MIX_W), lambda bi, c: (bi, c, COL_CV // MIX_W)),
                  pl.BlockSpec((bb, chunk, 128), lambda bi, c: (bi, c, 0)),
                  pl.BlockSpec((128, 256), lambda bi, c: (0, 0)),
                  pl.BlockSpec((1, 256), lambda bi, c: (0, 0)),
                  pl.BlockSpec((1, GLA_DV), lambda bi, c: (0, 0)),
                  st(state_layer)] + c_specs,
        out_specs=(pl.BlockSpec((bb, chunk, MIX_W), lambda bi, c: (bi, c, 0)), st(layer)),
        out_shape=(jax.ShapeDtypeStruct((b, t, MIX_W), F32),
                   jax.ShapeDtypeStruct((depth, b, N_HEADS, GLA_DK, GLA_DV), F32)),
        input_output_aliases=aliases,
        compiler_params=_cparams(("parallel", "arbitrary"), 40),
        name="gla",
    )(h3d, h3d, h3d, so3d, w2pad, b2.reshape(1, 256), hg.reshape(1, GLA_DV), s0, *c_args)


def _select_blocks(imp, qpos_col, ns):
    r = imp.shape[0]
    j = lax.broadcasted_iota(jnp.int32, (r, 128), 1)
    cur = qpos_col >> SEL_SHIFT
    valid = j <= cur
    forced = (j == 0) | (valid & (j >= cur - 1))
    val = jnp.where(forced, 1e4, jnp.where(valid, imp, -1e4))
    val = jnp.where(j < ns, val, -3e38)
    rank = jnp.zeros((r, 128), F32)
    for i in range(ns):
        ci = val[:, i:i + 1]
        beats = (ci > val) | ((ci == val) & (j > i))
        rank = rank + jnp.where(beats, 1.0, 0.0)
    return (rank < float(min(N_SEL, ns))) & (j < ns)


def _pair_matrix(nc):
    c = jnp.arange(nc)[:, None]
    j = jnp.arange(128)[None, :]
    return (c // (SEL_BLOCK // CMP_BLOCK) == j).astype(F32)


def _expand_matrix(tk):
    j = jnp.arange(128)[:, None]
    s = jnp.arange(tk)[None, :]
    return (s // SEL_BLOCK == j).astype(BF16)


NSA_TQ = 128
STACK_HEADS_MAX_KEYS = 1024


def _nsa_prompt_kernel(q_ref, kcmp_ref, vcmp_ref, kv_ref, win_ref, so_ref, e_ref, pair_ref, o_ref, *, t_len):
    tq = NSA_TQ
    nc = t_len // CMP_BLOCK
    ns = -(-t_len // SEL_BLOCK)
    i = pl.program_id(1)
    start = pl.multiple_of(i * tq, tq)
    qpos_col = start + lax.broadcasted_iota(jnp.int32, (tq, 1), 0)
    so = so_ref[...]
    q_all = jnp.concatenate([q_ref[:, h * HD:(h + 1) * HD] for h in range(N_HEADS)], axis=0)
    sc = _dot_nt(q_all, kcmp_ref[...].astype(BF16))
    qpos4 = start + (lax.broadcasted_iota(jnp.int32, (N_HEADS * tq, 1), 0) & (tq - 1))
    cend = lax.broadcasted_iota(jnp.int32, (N_HEADS * tq, nc), 1) * CMP_BLOCK + (CMP_BLOCK - 1)
    (ec,), invc = _softmax_parts([sc], [cend <= qpos4])
    pc = ec * invc
    o_cmp = _dot(pc.astype(BF16), vcmp_ref[...].astype(BF16))
    imp_c = pc[0:tq]
    for h in range(1, N_HEADS):
        imp_c = imp_c + pc[h * tq:(h + 1) * tq]
    imp = _dot_exact(imp_c, pair_ref[...])
    sel_b = jnp.where(_select_blocks(imp, qpos_col, ns), 1.0, 0.0).astype(BF16)
    band = WINDOW + tq
    kwin = win_ref[pl.ds(start, band), 0:HD]
    vwin = win_ref[pl.ds(start, band), HD:2 * HD]
    wpos = start - WINDOW + lax.broadcasted_iota(jnp.int32, (tq, band), 1)
    wmask = (wpos <= qpos_col) & (wpos > qpos_col - WINDOW) & (wpos >= 0)
    def shared_kv_attention(keys, values, mask):
        s_all = _dot_nt(q_all, keys)
        parts = [_softmax_parts([s_all[h * tq:(h + 1) * tq]], [mask]) for h in range(N_HEADS)]
        e_all = jnp.concatenate([es.astype(BF16) for (es,), _ in parts], axis=0)
        o_all = _dot(e_all, values)
        return [o_all[h * tq:(h + 1) * tq] * parts[h][1] for h in range(N_HEADS)]

    o_win = shared_kv_attention(kwin, vwin, wmask)
    for h in range(N_HEADS):
        sl = slice(h * HD, (h + 1) * HD)
        g0 = so[:, SM_DG + 3 * h:SM_DG + 3 * h + 1]
        g2 = so[:, SM_DG + 3 * h + 2:SM_DG + 3 * h + 3]
        o_ref[:, sl] = g0 * o_cmp[h * tq:(h + 1) * tq] + g2 * o_win[h]

    n_cls = -(-t_len // KEY_CLASS)
    cls = (start + tq - 1) // KEY_CLASS

    def run_selected(kl):
        sel_keys = _dot(sel_b, e_ref[:, 0:kl]) > 0.5
        kpos = lax.broadcasted_iota(jnp.int32, (tq, kl), 1)
        smask = sel_keys & (kpos <= qpos_col)
        ks = kv_ref[0:kl, 2 * HD:3 * HD]
        vs = kv_ref[0:kl, 3 * HD:4 * HD]
        o_sel = shared_kv_attention(ks, vs, smask) if kl <= STACK_HEADS_MAX_KEYS else None
        for h in range(N_HEADS):
            sl = slice(h * HD, (h + 1) * HD)
            if o_sel is None:
                (es,), invs = _softmax_parts([_dot_nt(q_ref[:, sl], ks)], [smask])
                o_h = _dot(es.astype(BF16), vs) * invs
            else:
                o_h = o_sel[h]
            g1 = so[:, SM_DG + 3 * h + 1:SM_DG + 3 * h + 2]
            o_ref[:, sl] = o_ref[:, sl] + g1 * o_h

    for c in range(n_cls):
        pl.when(cls == c)(functools.partial(run_selected, min((c + 1) * KEY_CLASS, t_len)))


def _nsa_prompt(nq, kcmp, vcmp, nkv, win_pad, so):
    b, t, _ = nq.shape
    tq = NSA_TQ
    nc = t // CMP_BLOCK
    e_mat = _expand_matrix(t)
    pair = _pair_matrix(nc)
    return pl.pallas_call(
        functools.partial(_nsa_prompt_kernel, t_len=t),
        grid=(b, t // tq),
        in_specs=[pl.BlockSpec((None, tq, MIX_W), lambda bi, i: (bi, i, 0)),
                  pl.BlockSpec((None, nc, HD), lambda bi, i: (bi, 0, 0)),
                  pl.BlockSpec((None, nc, HD), lambda bi, i: (bi, 0, 0)),
                  pl.BlockSpec((None, t, 768), lambda bi, i: (bi, 0, 0)),
                  pl.BlockSpec((None, t + WINDOW, 256), lambda bi, i: (bi, 0, 0)),
                  pl.BlockSpec((None, tq, 128), lambda bi, i: (bi, i, 0)),
                  pl.BlockSpec((128, t), lambda bi, i: (0, 0)),
                  pl.BlockSpec((nc, 128), lambda bi, i: (0, 0))],
        out_specs=pl.BlockSpec((None, tq, MIX_W), lambda bi, i: (bi, i, 0)),
        out_shape=jax.ShapeDtypeStruct((b, t, MIX_W), F32),
        compiler_params=_cparams(("parallel", "arbitrary"), 48),
        name="nsa_prompt",
    )(nq, kcmp, vcmp, nkv, win_pad, so, e_mat, pair)


def _nsa_sample_stages(q_ref, kvn_ref, winn_ref, so_ref, win_ref, wk_ref, wv_ref, kcg_ref, e_ref, pair_ref,
                       pages, o_ref, wout_ref, cmp_k, cmp_v, n_pages, t_new):
    past = n_pages * PAGE_SIZE
    nc = past // CMP_BLOCK
    ns = -(-(past + t_new) // SEL_BLOCK)
    per_page = PAGE_SIZE // CMP_BLOCK
    n_grp = n_pages // PAGE_GROUP
    gw = PAGE_GROUP * PAGE_SIZE
    rows = N_HEADS * t_new
    st = {}

    def page_rows(p, j):
        return pages[p][pl.ds(j, PAGE_SIZE, stride=4), :]

    def group_rows(g, j):
        return jnp.concatenate([page_rows(g * PAGE_GROUP + u, j).astype(BF16) for u in range(PAGE_GROUP)], axis=0)

    def compress():
        for p in range(n_pages):
            kc = page_rows(p, 0).reshape(per_page, CMP_BLOCK, HD)
            vc = page_rows(p, 1).reshape(per_page, CMP_BLOCK, HD)
            cmp_k[p * per_page:(p + 1) * per_page, :] = jnp.sum(kc * wk_ref[...][None], axis=1)
            cmp_v[p * per_page:(p + 1) * per_page, :] = jnp.sum(vc * wv_ref[...][None], axis=1)
        kcmp = _rms_lanes(cmp_k[...], kcg_ref[...]).astype(BF16)
        st["vcmp"] = cmp_v[...].astype(BF16)
        q_all = jnp.concatenate([q_ref[:, h * HD:(h + 1) * HD].astype(F32) for h in range(N_HEADS)], axis=0)
        st["q_all"] = q_all
        st["q_b"] = q_all.astype(BF16)
        trow = lax.broadcasted_iota(jnp.int32, (rows, 1), 0) & (t_new - 1)
        st["qpos4"] = past + trow
        st["kvn"] = kvn_ref[...].astype(F32)
        st["npos"] = past + lax.broadcasted_iota(jnp.int32, (rows, t_new), 1)
        st["causal_new"] = st["npos"] <= st["qpos4"]
        st["sc"] = _dot_nt(st["q_b"], kcmp)

    def selected_scores():
        s_list = [_dot_nt(st["q_b"], group_rows(g, 2)) for g in range(n_grp)]
        s_list.append(_dot_nt(st["q_all"], st["kvn"][:, 2 * HD:3 * HD]))
        st["s_list"] = s_list

    def compressed():
        cend = lax.broadcasted_iota(jnp.int32, (rows, nc), 1) * CMP_BLOCK + (CMP_BLOCK - 1)
        (ec,), invc = _softmax_parts([st["sc"]], [cend <= st["qpos4"]])
        pc = ec * invc
        imp_c = pc[0:t_new]
        for h in range(1, N_HEADS):
            imp_c = imp_c + pc[h * t_new:(h + 1) * t_new]
        st["imp"] = _dot_exact(imp_c, pair_ref[...])
        st["o_cmp"] = _dot(pc.astype(BF16), st["vcmp"])

    def window():
        qpos4, npos, kvn = st["qpos4"], st["npos"], st["kvn"]
        w_buf = win_ref.shape[0] // 2
        kwb = win_ref[pl.ds(0, w_buf, stride=2), :].astype(BF16)
        vwb = win_ref[pl.ds(1, w_buf, stride=2), :].astype(BF16)
        wpos = past - w_buf + lax.broadcasted_iota(jnp.int32, (rows, w_buf), 1)
        wmask = (wpos <= qpos4) & (wpos > qpos4 - WINDOW) & (wpos >= 0)
        sw_list = [_dot_nt(st["q_b"], kwb), _dot_nt(st["q_all"], kvn[:, 4 * HD:5 * HD])]
        ew, invw = _softmax_parts(sw_list, [wmask, st["causal_new"] & (npos > qpos4 - WINDOW)])
        st["o_win"] = (_dot(ew[0].astype(BF16), vwb) + _dot(ew[1], kvn[:, 5 * HD:6 * HD])) * invw
        keep = 2 * (w_buf - t_new)
        wout_ref[0:keep, :] = win_ref[2 * t_new:2 * w_buf, :]
        wout_ref[keep:2 * w_buf, :] = winn_ref[...]

    def select():
        qpos_col = past + lax.broadcasted_iota(jnp.int32, (t_new, 1), 0)
        sel = _select_blocks(st["imp"], qpos_col, ns)
        self32 = jnp.where(sel, 1.0, 0.0)
        sel_past = _dot(self32.astype(BF16), e_ref[...])
        sel_past4 = jnp.concatenate([sel_past] * N_HEADS, axis=0) > 0.5
        new_blk = past // SEL_BLOCK
        sel_new = jnp.concatenate([self32[:, new_blk:new_blk + 1]] * N_HEADS, axis=0) > 0.5
        m_list = []
        for g in range(n_grp):
            kpos = g * gw + lax.broadcasted_iota(jnp.int32, (rows, gw), 1)
            m_list.append(sel_past4[:, g * gw:(g + 1) * gw] & (kpos <= st["qpos4"]))
        m_list.append(sel_new & st["causal_new"])
        st["m_list"] = m_list

    def selected():
        es, invs = _softmax_parts(st["s_list"], st["m_list"])
        o_sel = _dot(es[n_grp], st["kvn"][:, 3 * HD:4 * HD])
        for g in range(n_grp):
            o_sel = o_sel + _dot(es[g].astype(BF16), group_rows(g, 3))
        o_sel = o_sel * invs
        so = so_ref[...]
        for h in range(N_HEADS):
            r = slice(h * t_new, (h + 1) * t_new)
            g0 = so[:, SM_DG + 3 * h:SM_DG + 3 * h + 1]
            g1 = so[:, SM_DG + 3 * h + 1:SM_DG + 3 * h + 2]
            g2 = so[:, SM_DG + 3 * h + 2:SM_DG + 3 * h + 3]
            o_ref[:, h * HD:(h + 1) * HD] = g0 * st["o_cmp"][r] + g1 * o_sel[r] + g2 * st["o_win"][r]

    return [compress, selected_scores, compressed, window, select, selected]


N_SAMPLE_IN = 15


def _sample_attn_kernel(pt_ref, fq_ref, fk_ref, fv_ref, so_ref, spread_ref, nq_ref, nkv_ref, winn_ref,
                        win_ref, wk_ref, wv_ref, kcg_ref, e_ref, pair_ref, *rest, n_pages, t_new):
    del pt_ref
    kv_pages = rest[:n_pages]
    lf_pages = rest[n_pages:2 * n_pages]
    nsa_pages = rest[2 * n_pages:3 * n_pages]
    o_fox, o_nsa, wout_ref, lf_scr, cmp_k, cmp_v = rest[3 * n_pages:3 * n_pages + 6]
    fox = _fox_sample_stages(fq_ref, fk_ref, fv_ref, so_ref, spread_ref, kv_pages, lf_pages, o_fox, lf_scr,
                             n_pages, t_new)
    nsa = _nsa_sample_stages(nq_ref, nkv_ref, winn_ref, so_ref, win_ref, wk_ref, wv_ref, kcg_ref, e_ref,
                             pair_ref, nsa_pages, o_nsa, wout_ref, cmp_k, cmp_v, n_pages, t_new)
    for n_stage, f_stage in zip(nsa, fox):
        n_stage()
        f_stage()


def _sample_attn(layer, depth, prev, page_table, fq, fk, fv, nq, nkv, win_new, so, fox_kv4, fox_lf4, nsa_kv4,
                 win_state4, wk_b, wv_b, kcg):
    b, t, _ = nq.shape
    n_pages = page_table.shape[1]
    past = n_pages * PAGE_SIZE
    nc = past // CMP_BLOCK
    w2 = win_state4.shape[2]
    e_mat = _expand_matrix(past)
    pair = _pair_matrix(nc)

    def tok(rows, width):
        return pl.BlockSpec((None, rows, width), lambda bi, pt: (bi, 0, 0))

    def const(shape):
        return pl.BlockSpec(shape, lambda bi, pt: (0, 0))

    def page_spec(rows, p):
        return pl.BlockSpec((None, None, rows, 128), lambda bi, pt, p=p: (layer, pt[bi, p], 0, 0))

    in_specs = [tok(t, MIX_W), tok(t, MIX_W), tok(t, MIX_W), tok(t, 128), const((PAGE_SIZE, 2 * PAGE_SIZE)),
                tok(t, MIX_W), tok(t, 768), tok(2 * t, HD),
                pl.BlockSpec((None, None, w2, 128), lambda bi, pt: (layer, bi, 0, 0)),
                const((CMP_BLOCK, 128)), const((CMP_BLOCK, 128)), const((1, 128)),
                const((128, past)), const((nc, 128))]
    in_specs += [page_spec(PAGE_SIZE * 2 * N_HEADS, p) for p in range(n_pages)]
    in_specs += [page_spec(N_HEADS, p) for p in range(n_pages)]
    in_specs += [page_spec(PAGE_SIZE * 4, p) for p in range(n_pages)]
    n_in = N_SAMPLE_IN + 3 * n_pages
    c_specs, c_args, aliases = _carry_args(prev, n_in, (2,))
    out_tok = pl.BlockSpec((None, t, MIX_W), lambda bi, pt: (bi, 0, 0))
    grid_spec = pltpu.PrefetchScalarGridSpec(
        num_scalar_prefetch=1, grid=(b,), in_specs=in_specs + c_specs,
        out_specs=(out_tok, out_tok, pl.BlockSpec((None, None, w2, 128), lambda bi, pt: (layer, bi, 0, 0))),
        scratch_shapes=[pltpu.VMEM((n_pages * N_HEADS, PAGE_SIZE), F32),
                        pltpu.VMEM((nc, HD), F32), pltpu.VMEM((nc, HD), F32)])
    return pl.pallas_call(
        _with_carried_outputs(functools.partial(_sample_attn_kernel, n_pages=n_pages, t_new=t), n_in, len(c_args)),
        grid_spec=grid_spec,
        out_shape=(jax.ShapeDtypeStruct((b, t, MIX_W), F32), jax.ShapeDtypeStruct((b, t, MIX_W), F32),
                   jax.ShapeDtypeStruct((depth, b, w2, 128), F32)),
        input_output_aliases=aliases,
        compiler_params=_cparams(("arbitrary",), 56),
        name="sample_attn",
    )(page_table, fq, fk, fv, so, _spread_matrix(), nq, nkv, win_new, win_state4, wk_b, wv_b, kcg, e_mat, pair,
      *([fox_kv4] * n_pages), *([fox_lf4] * n_pages), *([nsa_kv4] * n_pages), *c_args)


OUT_TM = 256


def _out_kernel(x_ref, oa_ref, ob_ref, oc_ref, od_ref, z_ref, g0_ref, g1_ref, g2_ref, g3_ref,
                wb_ref, wo_ref, gn_ref, y_ref, xn_ref):
    acc = None
    for g, (o_r, g_r) in enumerate(((oa_ref, g0_ref), (ob_ref, g1_ref), (oc_ref, g2_ref), (od_ref, g3_ref))):
        z = z_ref[:, g * MIX_W:(g + 1) * MIX_W]
        br = (o_r[...] * (z * _sigmoid(z))).astype(BF16)
        term = (jnp.tanh(g_r[...]) + 1.0) * _dot(br, wb_ref[g])
        acc = term if acc is None else acc + term
    y = x_ref[...] + _dot(acc.astype(BF16), wo_ref[...])
    y_ref[...] = y
    xn_ref[...] = _rms_lanes(y, gn_ref[...]).astype(BF16)


def _out_proj(x2d, oa, ob, oc, od, h2d, wb_all, wo_all, layer, next_norm_g):
    m = x2d.shape[0]
    tm = OUT_TM

    def rows(width):
        return pl.BlockSpec((tm, width), lambda i: (i, 0))

    def gate(g):
        return pl.BlockSpec((tm, D_MODEL), lambda i, g=g: (i, COL_GATE // D_MODEL + g))

    single = pl.Buffered(1)
    return pl.pallas_call(
        _out_kernel,
        grid=(m // tm,),
        in_specs=[rows(D_MODEL), rows(MIX_W), rows(MIX_W), rows(MIX_W), rows(MIX_W),
                  rows(D_MODEL), gate(0), gate(1), gate(2), gate(3),
                  pl.BlockSpec((None, N_HEADS, MIX_W, D_MODEL), lambda i: (layer, 0, 0, 0), pipeline_mode=single),
                  pl.BlockSpec((None, D_MODEL, D_MODEL), lambda i: (layer, 0, 0), pipeline_mode=single),
                  pl.BlockSpec((1, D_MODEL), lambda i: (0, 0))],
        out_specs=(rows(D_MODEL), rows(D_MODEL)),
        out_shape=(jax.ShapeDtypeStruct((m, D_MODEL), F32), jax.ShapeDtypeStruct((m, D_MODEL), BF16)),
        compiler_params=_cparams(("parallel",), 56),
        name="out_proj",
    )(x2d, oa, ob, oc, od, h2d, h2d, h2d, h2d, h2d, wb_all, wo_all, next_norm_g.reshape(1, D_MODEL))


def _layer_params(l, p):
    zeros = jnp.zeros((2, HD), F32)
    gains = jnp.concatenate([p["fox_qg"][l][None], p["fox_kg"][l][None], p["nsa_qg"][l][None],
                             p["nsa_ksg"][l][None], p["nsa_kwg"][l][None], p["nsa_kcg"][l][None], zeros], axis=0)
    bias = jnp.concatenate([p["fox_bf"][l], p["mlstm_bi"][l], p["mlstm_bf"][l], jnp.zeros((GLA_RANK,), F32),
                            p["nsa_bg"][l].reshape(-1), jnp.zeros((128 - SM_DG - 12,), F32)]).reshape(1, 128)
    w2pad = jnp.zeros((128, 256), F32).at[SM_CA:SM_CA + GLA_RANK].set(p["gla_w2"][l])
    return dict(
        norm_g=p["norm_g"][l], gains=gains, bias=bias,
        wk_b=jnp.broadcast_to(p["nsa_wk"][l][:, None], (CMP_BLOCK, 128)),
        wv_b=jnp.broadcast_to(p["nsa_wv"][l][:, None], (CMP_BLOCK, 128)),
        kcg=p["nsa_kcg"][l].reshape(1, HD),
        mlstm_hg=p["mlstm_hg"][l], w2pad=w2pad, gla_b2=p["gla_b2"][l], gla_hg=p["gla_hg"][l])


def _layer(x3d, xn, lp, shared, st, prev, *, is_prompt, layer, depth):
    b, t, _ = x3d.shape
    m = b * t
    x2d = x3d.reshape(m, D_MODEL)
    if xn is None:
        xn = _rmsnorm(x2d, lp["norm_g"])
    h2d = _in_proj(xn, shared["w_in"], layer)
    h3d = h2d.reshape(b, t, N_PROJ)
    (foxkv, nsakv, win, so, fq, fk, fv, nq, nkv, kcmp, vcmp) = _prep(
        h2d, lp["gains"], lp["bias"], lp["wk_b"], lp["wv_b"], layer, depth,
        None if prev is None else (prev["fox_kv"], prev["nsa_kv"]))
    so3 = so.reshape(b, t, 128)
    fq3, fk3, fv3 = fq.reshape(b, t, MIX_W), fk.reshape(b, t, MIX_W), fv.reshape(b, t, MIX_W)
    nq3 = nq.reshape(b, t, MIX_W)
    nkv3 = nkv.reshape(b, t, 768)
    chunk = min(64, t)
    carried = dict(fox_kv=foxkv, nsa_kv=nsakv)
    if is_prompt:
        lf_t = so3[:, :, SM_AF:SM_AF + N_HEADS].transpose(0, 2, 1).reshape(b * N_HEADS, t)
        f_row = _cumsum_rows(lf_t).reshape(b, N_HEADS, t)
        f_col = f_row.transpose(0, 2, 1)
        o_a = _fox_prompt(fq3, fk3, fv3, f_col, f_row)
        win_pad = jnp.pad(nkv3[:, :, 4 * HD:], ((0, 0), (WINDOW, 0), (0, 0)))
        o_d = _nsa_prompt(nq3, kcmp.reshape(b, t // CMP_BLOCK, HD), vcmp.reshape(b, t // CMP_BLOCK, HD),
                          nkv3, win_pad, so3)
        win_out = win.reshape(b, t, 2, HD)[:, -min(WINDOW, t):]
        bb = min(BATCH_UNROLL, b)
    else:
        o_a, o_d, win_out = _sample_attn(
            layer, depth, None if prev is None else (prev["win"],), shared["page_table"],
            fq3, fk3, fv3, nq3, nkv3, win.reshape(b, 2 * t, HD), so3,
            shared["fox_kv4"], shared["fox_lf4"], shared["nsa_kv4"], shared["nsa_win4"],
            lp["wk_b"], lp["wv_b"], lp["kcg"])
        bb = 2 * BATCH_UNROLL
    o_b, c_new, n_new, m_new = _mlstm(
        h3d, so3, lp["mlstm_hg"], st["C"], st["n"], st["m"], st["layer"], layer, depth,
        None if prev is None else (prev["C"], prev["n"], prev["m"]), bb, chunk)
    o_c, s_new = _gla(h3d, so3, lp["w2pad"], lp["gla_b2"], lp["gla_hg"], st["S"], st["layer"], layer, depth,
                      None if prev is None else (prev["S"],), bb, chunk)
    y2d, xn_next = _out_proj(x2d, o_a.reshape(m, MIX_W), o_b.reshape(m, MIX_W), o_c.reshape(m, MIX_W),
                             o_d.reshape(m, MIX_W), h2d, shared["w_branch"], shared["w_out"], layer,
                             lp["next_norm_g"])
    carried.update(win=win_out, C=c_new, n=n_new, m=m_new, S=s_new)
    return y2d.reshape(b, t, D_MODEL), xn_next, carried, so3[:, :, SM_AF:SM_AF + N_HEADS]


def kernel(x_prompt, x_sample, cache_fox_kv, cache_fox_logf, cache_nsa_kv, page_table, state_nsa_win, state_mlstm_C, state_mlstm_n, state_mlstm_m, state_gla_S, norm_g, w_in, fox_qg, fox_kg, fox_bf, mlstm_bi, mlstm_bf, mlstm_hg, gla_w2, gla_b2, gla_hg, nsa_qg, nsa_kcg, nsa_ksg, nsa_kwg, nsa_wk, nsa_wv, nsa_bg, w_branch, w_out):
    params = dict(norm_g=norm_g, fox_qg=fox_qg, fox_kg=fox_kg, fox_bf=fox_bf, mlstm_bi=mlstm_bi,
                  mlstm_bf=mlstm_bf, mlstm_hg=mlstm_hg, gla_w2=gla_w2, gla_b2=gla_b2, gla_hg=gla_hg,
                  nsa_qg=nsa_qg, nsa_kcg=nsa_kcg, nsa_ksg=nsa_ksg, nsa_kwg=nsa_kwg, nsa_wk=nsa_wk,
                  nsa_wv=nsa_wv, nsa_bg=nsa_bg)
    depth = w_in.shape[0]
    bp, tp, _ = x_prompt.shape
    db, ts, _ = x_sample.shape
    n_pool = cache_fox_kv.shape[1]
    w_buf = state_nsa_win.shape[2]
    shared = dict(
        w_in=_permute_w_in(w_in), w_branch=(0.5 * w_branch).astype(BF16), w_out=w_out.astype(BF16),
        page_table=page_table,
        fox_kv4=cache_fox_kv.reshape(depth, n_pool, PAGE_SIZE * 2 * N_HEADS, HD),
        fox_lf4=cache_fox_logf.transpose(0, 1, 3, 2),
        nsa_kv4=cache_nsa_kv.reshape(depth, n_pool, PAGE_SIZE * 4, HD),
        nsa_win4=state_nsa_win.reshape(depth, db, w_buf * 2, HD))
    st_p = dict(C=jnp.zeros((1, bp, N_HEADS, HD, HD), F32), n=jnp.zeros((1, bp, N_HEADS, HD), F32),
                m=jnp.zeros((1, bp, N_HEADS, HD), F32), S=jnp.zeros((1, bp, N_HEADS, GLA_DK, GLA_DV), F32),
                layer=0)
    m_s = jnp.broadcast_to(state_mlstm_m[..., None], state_mlstm_m.shape + (HD,))
    y_p, y_s = x_prompt, x_sample
    xn_p = xn_s = None
    prev_p = prev_s = None
    lf_p, lf_s, win_p = [], [], []
    for l in range(depth):
        lp = _layer_params(l, params)
        lp["next_norm_g"] = norm_g[min(l + 1, depth - 1)]
        st_s = dict(C=state_mlstm_C, n=state_mlstm_n, m=m_s, S=state_gla_S, layer=l)
        y_p, xn_p, prev_p, lf = _layer(y_p, xn_p, lp, shared, st_p, prev_p, is_prompt=True, layer=l, depth=depth)
        lf_p.append(lf)
        win_p.append(prev_p["win"])
        y_s, xn_s, prev_s, lf = _layer(y_s, xn_s, lp, shared, st_s, prev_s, is_prompt=False, layer=l, depth=depth)
        lf_s.append(lf)

    def finish(c, b, t, win, lf):
        return (c["fox_kv"].reshape(depth, b, t, 2, N_HEADS, HD), jnp.stack(lf, axis=0),
                c["nsa_kv"].reshape(depth, b, t, 4, HD), win,
                c["C"], c["n"], c["m"][..., 0], c["S"])

    out_p = finish(prev_p, bp, tp, jnp.stack(win_p, axis=0), lf_p)
    out_s = finish(prev_s, db, ts, prev_s["win"].reshape(depth, db, w_buf, 2, HD), lf_s)
    return (y_p, y_s) + out_p + out_s
```

```python
import functools

import jax
import jax.numpy as jnp
from jax import lax
from jax.experimental import pallas as pl
from jax.experimental.pallas import tpu as pltpu

F32 = jnp.float32
BF16 = jnp.bfloat16
HIGHEST = lax.Precision.HIGHEST

D_MODEL = 2048
MIX_W = 512
HD = 128
N_HEADS = 4
GLA_DK = 64
GLA_DK_SHIFT = 6
GLA_DV = 128
GLA_RANK = 16
GLA_TAU = 16.0
CMP_BLOCK = 32
SEL_BLOCK = 64
SEL_SHIFT = 6
N_SEL = 16
WINDOW = 512
PAGE_SIZE = 128
EPS = 1e-6
ATT_SCALE = HD ** -0.5
LOG2E = 1.4426950408889634
Q_PRESCALE = ATT_SCALE * LOG2E
NEG_INF = float("-inf")
KEY_CLASS = 256
BF16_ROWS = 16

_REF_LAYOUT = (
    ("a_q", 512), ("a_k", 512), ("a_v", 512), ("a_f", 4), ("a_z", 512),
    ("b_q", 512), ("b_k", 512), ("b_v", 512), ("b_i", 4), ("b_f", 4), ("b_o", 512), ("b_z", 512),
    ("c_q", 256), ("c_k", 256), ("c_v", 512), ("c_a", 16), ("c_z", 512),
    ("d_q", 512), ("d_kv", 768), ("d_g", 12), ("d_z", 512),
    ("gate", 8192),
)
N_REF = sum(w for _, w in _REF_LAYOUT)
_MY_LAYOUT = (
    "a_z", "b_z", "c_z", "d_z", "gate",
    "a_q", "a_k", "a_v", "d_q", "b_q", "b_k", "b_v", "b_o", "c_v", "c_q", "c_k", "d_kv",
    "a_f", "b_i", "b_f", "c_a", "d_g",
)
N_PROJ = 16384
COL_Z = 0
COL_GATE = 2048
COL_AQ, COL_AK, COL_AV, COL_DQ = 10240, 10752, 11264, 11776
COL_BQ, COL_BK, COL_BV, COL_BO = 12288, 12800, 13312, 13824
COL_CV, COL_CQ, COL_CK, COL_DKV, COL_SMALL = 14336, 14848, 15104, 15360, 16128
SM_AF, SM_BI, SM_BF, SM_CA, SM_DG = 0, 4, 8, 12, 28


def _cparams(sem, vmem_mb):
    return pltpu.CompilerParams(dimension_semantics=sem, vmem_limit_bytes=vmem_mb * 1024 * 1024)


def _log_sigmoid(x):
    return jnp.minimum(x, 0.0) - jnp.log(1.0 + jnp.exp(-jnp.abs(x)))


def _sigmoid(x):
    return 0.5 * jnp.tanh(0.5 * x) + 0.5


def _rms_lanes(x, g):
    return (x * lax.rsqrt(jnp.mean(x * x, axis=-1, keepdims=True) + EPS)) * g


def _dot(a, b):
    return jnp.dot(a, b, preferred_element_type=F32)


def _dot_nt(a, b):
    return lax.dot_general(a, b, (((1,), (1,)), ((), ())), preferred_element_type=F32)


def _dot_tn(a, b):
    return lax.dot_general(a, b, (((0,), (0,)), ((), ())), preferred_element_type=F32)


def _dot_exact(a, b):
    return jnp.dot(a, b, precision=HIGHEST, preferred_element_type=F32)


def _eye_mask(n):
    return lax.broadcasted_iota(jnp.int32, (n, n), 0) == lax.broadcasted_iota(jnp.int32, (n, n), 1)


def _col_to_row(c):
    n = c.shape[0]
    return jnp.sum(jnp.where(_eye_mask(n), c, 0.0), axis=0, keepdims=True)


def _row_to_col(r):
    n = r.shape[1]
    return jnp.sum(jnp.where(_eye_mask(n), r, 0.0), axis=1, keepdims=True)


def _tril_ones(n):
    r = lax.broadcasted_iota(jnp.int32, (n, n), 0)
    c = lax.broadcasted_iota(jnp.int32, (n, n), 1)
    return jnp.where(c <= r, 1.0, 0.0).astype(F32)


def _cumsum_lanes(x):
    n = x.shape[-1]
    lane = lax.broadcasted_iota(jnp.int32, x.shape, x.ndim - 1)
    s = 1
    while s < n:
        x = x + jnp.where(lane >= s, pltpu.roll(x, s, axis=x.ndim - 1), 0.0)
        s *= 2
    return x


def _pad_rows(x, rows):
    if x.shape[0] == rows:
        return x
    return jnp.concatenate([x, jnp.zeros((rows - x.shape[0], x.shape[1]), x.dtype)], axis=0)


def _softmax_parts(s_list, mask_list):
    masked = [jnp.where(mk, s, NEG_INF) for s, mk in zip(s_list, mask_list)]
    m = masked[0].max(axis=-1, keepdims=True)
    for s in masked[1:]:
        m = jnp.maximum(m, s.max(axis=-1, keepdims=True))
    m = jnp.where(m > NEG_INF, m, 0.0)
    es = [jnp.exp2(s - m) for s in masked]
    tot = es[0].sum(axis=-1, keepdims=True)
    for e in es[1:]:
        tot = tot + e.sum(axis=-1, keepdims=True)
    return es, 1.0 / jnp.maximum(tot, 1e-30)


def _with_carried_outputs(kernel_fn, n_in, n_carried):
    if not n_carried:
        return kernel_fn

    def wrapped(*refs):
        return kernel_fn(*refs[:n_in], *refs[n_in + n_carried:])

    return wrapped


def _carry_args(prev, first_in_index, out_indices):
    if prev is None:
        return [], [], {}
    specs = [pl.BlockSpec(memory_space=pl.ANY) for _ in prev]
    aliases = {first_in_index + k: oi for k, oi in enumerate(out_indices)}
    return specs, list(prev), aliases


def _segments():
    offs = {}
    off = 0
    for name, width in _REF_LAYOUT:
        offs[name] = (off, width)
        off += width
    segs = []
    dst = 0
    for name in _MY_LAYOUT:
        src, width = offs[name]
        segs.append((src, dst, width))
        dst += width
    return segs, dst


def _permute_kernel(wt_ref, o_ref, small_ref):
    segs, _ = _segments()
    for src, dst, width in segs:
        if width >= 128:
            w = wt_ref[src:src + width, :]
            if dst == COL_GATE:
                w = w * 0.5
            for c0 in range(0, width, PERMUTE_CHUNK):
                c1 = min(c0 + PERMUTE_CHUNK, width)
                o_ref[:, dst + c0:dst + c1] = jnp.transpose(w[c0:c1, :]).astype(BF16)
    small_ref[...] = jnp.zeros(small_ref.shape, F32)
    for src, dst, width in segs:
        if width < 128:
            small_ref[dst - COL_SMALL:dst - COL_SMALL + width, :] = wt_ref[src:src + width, :]
    o_ref[:, COL_SMALL:N_PROJ] = jnp.transpose(small_ref[...]).astype(BF16)


PERMUTE_CHUNK = 512


def _permute_w_in(w_in):
    depth, d, n = w_in.shape
    tk = 128
    wt = jnp.transpose(w_in, (0, 2, 1))
    return pl.pallas_call(
        _permute_kernel,
        grid=(depth, d // tk),
        in_specs=[pl.BlockSpec((None, n, tk), lambda l, i: (l, 0, i))],
        out_specs=pl.BlockSpec((None, tk, N_PROJ), lambda l, i: (l, i, 0)),
        out_shape=jax.ShapeDtypeStruct((depth, d, N_PROJ), BF16),
        scratch_shapes=[pltpu.VMEM((N_PROJ - COL_SMALL, tk), F32)],
        compiler_params=_cparams(("parallel", "parallel"), 40),
        name="permute_w_in",
    )(wt)


def _norm_kernel(x_ref, g_ref, o_ref):
    o_ref[...] = _rms_lanes(x_ref[...], g_ref[...]).astype(o_ref.dtype)


def _rmsnorm(x2d, g):
    m, d = x2d.shape
    tm = min(512, m)
    return pl.pallas_call(
        _norm_kernel,
        grid=(m // tm,),
        in_specs=[pl.BlockSpec((tm, d), lambda i: (i, 0)), pl.BlockSpec((1, d), lambda i: (0, 0))],
        out_specs=pl.BlockSpec((tm, d), lambda i: (i, 0)),
        out_shape=jax.ShapeDtypeStruct((m, d), BF16),
        compiler_params=_cparams(("parallel",), 32),
        name="rmsnorm",
    )(x2d, g.reshape(1, d))


def _mm_kernel(a_ref, b_ref, o_ref):
    o_ref[...] = _dot(a_ref[...], b_ref[...])


def _in_proj(a, w_all, layer):
    m, k = a.shape
    n = w_all.shape[2]
    tm, tn = min(1024, m), 1024
    return pl.pallas_call(
        _mm_kernel,
        grid=(m // tm, n // tn),
        in_specs=[pl.BlockSpec((tm, k), lambda i, j: (i, 0)),
                  pl.BlockSpec((None, k, tn), lambda i, j: (layer, 0, j))],
        out_specs=pl.BlockSpec((tm, tn), lambda i, j: (i, j)),
        out_shape=jax.ShapeDtypeStruct((m, n), F32),
        compiler_params=_cparams(("parallel", "arbitrary"), 48),
        name="in_proj",
    )(a, w_all)


PREP_TM = 256
N_PREP_IN = 10


def _prep_kernel(aq_ref, ak_ref, av_ref, dq_ref, dkv_ref, sm_ref, gains_ref, bias_ref, wk_ref, wv_ref,
                 foxkv_ref, nsakv_ref, win_ref, so_ref, fq_ref, fk_ref, fv_ref, nq_ref, nkv_ref,
                 kcmp_ref, vcmp_ref):
    tm = PREP_TM
    g_fq, g_fk, g_nq = gains_ref[0:1, :], gains_ref[1:2, :], gains_ref[2:3, :]
    g_ks, g_kw, g_kc = gains_ref[3:4, :], gains_ref[4:5, :], gains_ref[5:6, :]
    for h in range(N_HEADS):
        sl = slice(h * HD, (h + 1) * HD)
        fq_ref[:, sl] = (_rms_lanes(aq_ref[:, sl], g_fq) * Q_PRESCALE).astype(BF16)
        kn = _rms_lanes(ak_ref[:, sl], g_fk)
        foxkv_ref[pl.ds(h, tm, stride=2 * N_HEADS), :] = kn
        fk_ref[:, sl] = kn.astype(BF16)
        v = av_ref[:, sl]
        foxkv_ref[pl.ds(N_HEADS + h, tm, stride=2 * N_HEADS), :] = v
        fv_ref[:, sl] = v.astype(BF16)
        nq_ref[:, sl] = (_rms_lanes(dq_ref[:, sl], g_nq) * Q_PRESCALE).astype(BF16)
    kc = dkv_ref[:, 0:128]
    vc = dkv_ref[:, 128:256]
    ks = _rms_lanes(dkv_ref[:, 256:384], g_ks)
    vs = dkv_ref[:, 384:512]
    kw = _rms_lanes(dkv_ref[:, 512:640], g_kw)
    vw = dkv_ref[:, 640:768]
    for j, a in enumerate((kc, vc, ks, vs)):
        nsakv_ref[pl.ds(j, tm, stride=4), :] = a
    win_ref[pl.ds(0, tm, stride=2), :] = kw
    win_ref[pl.ds(1, tm, stride=2), :] = vw
    for j, a in enumerate((kc, vc, ks, vs, kw, vw)):
        nkv_ref[:, j * HD:(j + 1) * HD] = a.astype(BF16)
    nb = tm // CMP_BLOCK
    kcs = jnp.sum(kc.reshape(nb, CMP_BLOCK, HD) * wk_ref[...][None], axis=1)
    kcmp_ref[...] = _rms_lanes(kcs, g_kc)
    vcmp_ref[...] = jnp.sum(vc.reshape(nb, CMP_BLOCK, HD) * wv_ref[...][None], axis=1)
    y = sm_ref[...] + bias_ref[...]
    lane = lax.broadcasted_iota(jnp.int32, y.shape, 1)
    is_ls = (lane < SM_BI) | ((lane >= SM_BF) & (lane < SM_CA))
    is_sg = (lane >= SM_DG) & (lane < SM_DG + 12)
    so_ref[...] = jnp.where(is_ls, _log_sigmoid(y), jnp.where(is_sg, _sigmoid(y), y))


def _prep(h2d, gains, bias, wk_b, wv_b, layer, depth, prev):
    m = h2d.shape[0]
    tm = PREP_TM

    def col(width, off):
        blk = off // width
        return pl.BlockSpec((tm, width), lambda i, blk=blk: (i, blk))

    def full(shape):
        return pl.BlockSpec(shape, lambda i: (0, 0))

    def rows(width):
        return pl.BlockSpec((tm, width), lambda i: (i, 0))

    def stacked(slots):
        return pl.BlockSpec((None, tm * slots, HD), lambda i: (layer, i, 0))

    out_shapes = (
        jax.ShapeDtypeStruct((depth, m * 8, HD), F32),
        jax.ShapeDtypeStruct((depth, m * 4, HD), F32),
        jax.ShapeDtypeStruct((m * 2, HD), F32),
        jax.ShapeDtypeStruct((m, 128), F32),
        jax.ShapeDtypeStruct((m, 512), BF16),
        jax.ShapeDtypeStruct((m, 512), BF16),
        jax.ShapeDtypeStruct((m, 512), BF16),
        jax.ShapeDtypeStruct((m, 512), BF16),
        jax.ShapeDtypeStruct((m, 768), BF16),
        jax.ShapeDtypeStruct((m // CMP_BLOCK, 128), F32),
        jax.ShapeDtypeStruct((m // CMP_BLOCK, 128), F32),
    )
    out_specs = (stacked(8), stacked(4), pl.BlockSpec((tm * 2, HD), lambda i: (i, 0)),
                 rows(128), rows(512), rows(512), rows(512), rows(512), rows(768),
                 pl.BlockSpec((tm // CMP_BLOCK, 128), lambda i: (i, 0)),
                 pl.BlockSpec((tm // CMP_BLOCK, 128), lambda i: (i, 0)))
    c_specs, c_args, aliases = _carry_args(prev, N_PREP_IN, (0, 1))
    return pl.pallas_call(
        _with_carried_outputs(_prep_kernel, N_PREP_IN, len(c_args)),
        grid=(m // tm,),
        in_specs=[col(512, COL_AQ), col(512, COL_AK), col(512, COL_AV), col(512, COL_DQ),
                  col(768, COL_DKV), col(128, COL_SMALL),
                  full((8, 128)), full((1, 128)), full((CMP_BLOCK, 128)), full((CMP_BLOCK, 128))] + c_specs,
        out_specs=out_specs,
        out_shape=out_shapes,
        input_output_aliases=aliases,
        compiler_params=_cparams(("parallel",), 32),
        name="prep",
    )(h2d, h2d, h2d, h2d, h2d, h2d, gains, bias, wk_b, wv_b, *c_args)


def _cumsum_rows_kernel(x_ref, o_ref):
    o_ref[...] = _cumsum_lanes(x_ref[...])


def _cumsum_rows(x):
    r, n = x.shape
    return pl.pallas_call(
        _cumsum_rows_kernel,
        grid=(1,),
        in_specs=[pl.BlockSpec((r, n), lambda i: (0, 0))],
        out_specs=pl.BlockSpec((r, n), lambda i: (0, 0)),
        out_shape=jax.ShapeDtypeStruct((r, n), F32),
        name="fox_cumsum",
    )(x)


FOX_TQ = 256


def _fox_prompt_kernel(q_ref, k_ref, v_ref, fc_ref, fr_ref, o_ref, *, t_len):
    tq = FOX_TQ
    i = pl.program_id(1)
    n_cls = -(-t_len // KEY_CLASS)
    cls = (i * tq + tq - 1) // KEY_CLASS

    def run(kl):
        qpos = i * tq + lax.broadcasted_iota(jnp.int32, (tq, kl), 0)
        kpos = lax.broadcasted_iota(jnp.int32, (tq, kl), 1)
        mask = kpos <= qpos
        for h in range(N_HEADS):
            sl = slice(h * HD, (h + 1) * HD)
            s = _dot_nt(q_ref[:, sl], k_ref[0:kl, sl])
            s = s + fc_ref[:, h:h + 1] * LOG2E - fr_ref[h:h + 1, 0:kl] * LOG2E
            (e,), inv = _softmax_parts([s], [mask])
            o_ref[:, sl] = _dot(e.astype(BF16), v_ref[0:kl, sl]) * inv

    for c in range(n_cls):
        pl.when(cls == c)(functools.partial(run, min((c + 1) * KEY_CLASS, t_len)))


def _fox_prompt(fq, fk, fv, f_col, f_row):
    b, t, _ = fq.shape
    tq = FOX_TQ
    return pl.pallas_call(
        functools.partial(_fox_prompt_kernel, t_len=t),
        grid=(b, t // tq),
        in_specs=[pl.BlockSpec((None, tq, MIX_W), lambda bi, i: (bi, i, 0)),
                  pl.BlockSpec((None, t, MIX_W), lambda bi, i: (bi, 0, 0)),
                  pl.BlockSpec((None, t, MIX_W), lambda bi, i: (bi, 0, 0)),
                  pl.BlockSpec((None, tq, N_HEADS), lambda bi, i: (bi, i, 0)),
                  pl.BlockSpec((None, N_HEADS, t), lambda bi, i: (bi, 0, 0))],
        out_specs=pl.BlockSpec((None, tq, MIX_W), lambda bi, i: (bi, i, 0)),
        out_shape=jax.ShapeDtypeStruct((b, t, MIX_W), F32),
        compiler_params=_cparams(("parallel", "arbitrary"), 48),
        name="fox_prompt",
    )(fq, fk, fv, f_col, f_row)


PAGE_GROUP = 2


def _fox_sample_stages(q_ref, kn_ref, vn_ref, so_ref, spread_ref, kv_pages, lf_pages, o_ref, lf_scr,
                       n_pages, t_new):
    past = n_pages * PAGE_SIZE
    w2 = 2 * PAGE_SIZE
    n_rows = n_pages * N_HEADS
    heads = range(N_HEADS)
    sls = [slice(h * HD, (h + 1) * HD) for h in heads]
    st = {"qk": {}}

    def kv_rows(p, h):
        return kv_pages[p][pl.ds(h, w2, stride=N_HEADS), :].astype(BF16)

    def prefix_sums():
        for p in range(n_pages):
            lf_scr[p * N_HEADS:(p + 1) * N_HEADS, :] = lf_pages[p][...]
        lf_rows = lf_scr[...]
        local = _dot_exact(lf_rows, spread_ref[...])
        tot = _dot_exact(lf_rows, jnp.ones((PAGE_SIZE, 128), F32))
        rr = lax.broadcasted_iota(jnp.int32, (n_rows, n_rows), 0)
        cc = lax.broadcasted_iota(jnp.int32, (n_rows, n_rows), 1)
        same_head = (rr & (N_HEADS - 1)) == (cc & (N_HEADS - 1))
        earlier = jnp.where(same_head & (cc < rr), 1.0, 0.0)
        up_to = jnp.where(same_head & (cc <= rr), 1.0, 0.0)
        st["f_int"] = (local + _dot_exact(earlier, tot)[:, 0:1]) * LOG2E
        st["f_total"] = _dot_exact(up_to, tot)[n_rows - N_HEADS:n_rows, 0:1]
        st["f_new"] = _dot_exact(_tril_ones(t_new), so_ref[...])
        st["qs"] = [q_ref[:, sl].astype(F32) for sl in sls]

    def pair_tile(p, pr):
        return jnp.concatenate([kv_rows(p, pr[0]), kv_rows(p, pr[1])], axis=1)

    def scores(pr):
        def run():
            zero = jnp.zeros((t_new, HD), F32)
            q_pair = jnp.concatenate(
                [jnp.concatenate([st["qs"][pr[0]], zero], axis=1),
                 jnp.concatenate([zero, st["qs"][pr[1]]], axis=1)], axis=0).astype(BF16)
            both = [_dot_nt(q_pair, pair_tile(p, pr)) for p in range(n_pages)]
            st["qk"][pr[0]] = [x[0:t_new] for x in both]
            st["qk"][pr[1]] = [x[t_new:2 * t_new] for x in both]
        return run

    def softmax():
        lane = lax.broadcasted_iota(jnp.int32, (t_new, w2), 1)
        is_key = (lane & 1) == 0
        kpos0 = lane >> 1
        tpos = lax.broadcasted_iota(jnp.int32, (t_new, w2), 0) + past
        rn = lax.broadcasted_iota(jnp.int32, (t_new, t_new), 0)
        cn = lax.broadcasted_iota(jnp.int32, (t_new, t_new), 1)
        m_list = [is_key & (kpos0 + p * PAGE_SIZE <= tpos) for p in range(n_pages)] + [cn <= rn]
        qk_new = [_dot_nt(st["qs"][h], kn_ref[:, sls[h]].astype(F32)) for h in heads]
        st["w"] = []
        for h in heads:
            fq = (st["f_new"][:, SM_AF + h:SM_AF + h + 1] + st["f_total"][h:h + 1, :]) * LOG2E
            s_list = [st["qk"][h][p] + fq - st["f_int"][p * N_HEADS + h:p * N_HEADS + h + 1, :]
                      for p in range(n_pages)]
            s_list.append(qk_new[h] + fq - _col_to_row(fq))
            st["w"].append(_softmax_parts(s_list, m_list))

    def values(pr):
        def run():
            (es0, inv0), (es1, inv1) = st["w"][pr[0]], st["w"][pr[1]]
            acc = None
            for p in range(n_pages):
                w_pair = jnp.concatenate([pltpu.roll(es0[p], 1, axis=1), pltpu.roll(es1[p], 1, axis=1)], axis=0)
                part = _dot(w_pair.astype(BF16), pair_tile(p, pr))
                acc = part if acc is None else acc + part
            o0 = acc[0:t_new, 0:HD] + _dot(es0[n_pages], vn_ref[:, sls[pr[0]]].astype(F32))
            o1 = acc[t_new:2 * t_new, HD:2 * HD] + _dot(es1[n_pages], vn_ref[:, sls[pr[1]]].astype(F32))
            o_ref[:, sls[pr[0]]] = o0 * inv0
            o_ref[:, sls[pr[1]]] = o1 * inv1
        return run

    pairs = [(h, h + 1) for h in range(0, N_HEADS, 2)]
    return [prefix_sums, scores(pairs[0]), scores(pairs[1]), softmax, values(pairs[0]), values(pairs[1])]


def _spread_matrix():
    r = jnp.arange(PAGE_SIZE)[:, None]
    c = jnp.arange(2 * PAGE_SIZE)[None, :]
    return (2 * r <= c).astype(F32)


N_MLSTM_IN = 9
BATCH_UNROLL = 4
OUTER_SUM_MIN_CHUNK = 64


def _mlstm_kernel(q_ref, k_ref, v_ref, og_ref, so_ref, hg_ref, c0_ref, n0_ref, m0_ref,
                  o_ref, c_ref, n_ref, m_ref, *, bb, chunk, wide_chains):
    L = chunk
    Lp = max(L, BF16_ROWS)

    @pl.when(pl.program_id(1) == 0)
    def _():
        c_ref[...] = c0_ref[...]
        n_ref[...] = n0_ref[...]
        m_ref[...] = m0_ref[...]

    tril = _tril_ones(Lp)
    causal = tril > 0.5
    hg = hg_ref[...]
    lane = lax.broadcasted_iota(jnp.int32, (Lp - L, 128), 1) if Lp > L else None

    def run_chains(idx):
        so_of, csum_of = [], []
        for bi in idx:
            so = so_ref[bi]
            if Lp > L:
                pad = jnp.where((lane >= SM_BI) & (lane < SM_BF), -1e30, 0.0)
                so = jnp.concatenate([so, pad], axis=0)
            so_of.append(so)
            csum_of.append(_dot_exact(tril, so))
        ch = [(bi, h) for bi in idx for h in range(N_HEADS)]
        seq = [u for u in range(len(idx)) for _ in range(N_HEADS)]
        n = range(len(ch))
        sls = [slice(h * HD, (h + 1) * HD) for _, h in ch]
        q = [_pad_rows(q_ref[bi, :, sls[c]], Lp) for c, (bi, _) in enumerate(ch)]
        k = [_pad_rows(k_ref[bi, :, sls[c]], Lp) * ATT_SCALE for c, (bi, _) in enumerate(ch)]
        vb = [_pad_rows(v_ref[bi, :, sls[c]], Lp).astype(BF16) for c, (bi, _) in enumerate(ch)]
        qb = [x.astype(BF16) for x in q]
        ig = [so_of[seq[c]][:, SM_BI + h:SM_BI + h + 1] for c, (_, h) in enumerate(ch)]
        bcol = [csum_of[seq[c]][:, SM_BF + h:SM_BF + h + 1] for c, (_, h) in enumerate(ch)]
        c_st = [c_ref[bi, h] for bi, h in ch]
        n_st = [n_ref[bi, h:h + 1, :] for bi, h in ch]
        m_st = [m_ref[bi, h:h + 1, 0:1] for bi, h in ch]
        qk_raw = [_dot_nt(qb[c], k[c].astype(BF16)) for c in n]
        q_c = [_dot(qb[c], c_st[c].astype(BF16)) for c in n]
        lane_g = lax.broadcasted_iota(jnp.int32, (Lp, 128), 1)
        dmat = []
        for c, (_, h) in enumerate(ch):
            if L < OUTER_SUM_MIN_CHUNK:
                dmat.append(jnp.where(causal, bcol[c] - _col_to_row(bcol[c]) + _col_to_row(ig[c]), NEG_INF))
                continue
            at_b, at_i = lane_g == SM_BF + h, lane_g == SM_BI + h
            cs, so_c = csum_of[seq[c]], so_of[seq[c]]
            left = jnp.concatenate([jnp.where(at_b, cs, 0.0), jnp.where(at_b | at_i, 1.0, 0.0)], axis=1)
            right = jnp.concatenate([jnp.where(at_b, 1.0, 0.0),
                                     jnp.where(at_i, so_c, 0.0) - jnp.where(at_b, cs, 0.0)], axis=1)
            outer = lax.dot_general(left, right, (((1,), (1,)), ((), ())), precision=HIGHEST,
                                    preferred_element_type=F32)
            dmat.append(jnp.where(causal, outer, NEG_INF))
        inter = [bcol[c] + m_st[c] for c in n]
        m_t = [jnp.maximum(inter[c], dmat[c].max(axis=-1, keepdims=True)) for c in n]
        w_inter = [jnp.exp(inter[c] - m_t[c]) for c in n]
        qk = [qk_raw[c] * jnp.exp(dmat[c] - m_t[c]) for c in n]
        qk_v = [_dot(qk[c].astype(BF16), vb[c]) for c in n]
        b_last = [bcol[c][Lp - 1:Lp, :] for c in n]
        g_end = [b_last[c] - bcol[c] + ig[c] for c in n]
        m_new = [jnp.maximum(b_last[c] + m_st[c], g_end[c].max(axis=0, keepdims=True)) for c in n]
        a_prev = [jnp.exp(b_last[c] + m_st[c] - m_new[c]) for c in n]
        kw = [k[c] * jnp.exp(g_end[c] - m_new[c]) for c in n]
        k_v = [_dot_tn(kw[c].astype(BF16), vb[c]) for c in n]
        new_states = []
        for c, (bi, h) in enumerate(ch):
            num = w_inter[c] * q_c[c] + qk_v[c]
            den = (w_inter[c] * jnp.sum(q[c] * n_st[c], axis=-1, keepdims=True)
                   + jnp.sum(qk[c], axis=-1, keepdims=True))
            hout = num / jnp.maximum(jnp.abs(den), jnp.exp(-m_t[c]))
            new_states.append((a_prev[c] * c_st[c] + k_v[c],
                               a_prev[c] * n_st[c] + jnp.sum(kw[c], axis=0, keepdims=True),
                               jnp.broadcast_to(m_new[c], (1, HD))))
            hn = _rms_lanes(hout[0:L], hg)
            o_ref[bi, :, sls[c]] = hn * _sigmoid(og_ref[bi, :, sls[c]])
        for (bi, h), (c_new, n_new, m_nw) in zip(ch, new_states):
            c_ref[bi, h] = c_new
            n_ref[bi, h:h + 1, :] = n_new
            m_ref[bi, h:h + 1, :] = m_nw

    unroll = min(BATCH_UNROLL, bb)
    wide = min(wide_chains, unroll)

    def group(g, carry):
        for u0 in range(0, unroll, wide):
            run_chains([g * unroll + u0 + u for u in range(wide)])
        return carry

    if bb == unroll:
        group(0, 0)
    else:
        lax.fori_loop(0, bb // unroll, group, 0)


def _mlstm(h3d, so3d, hg, c0, n0, m0, state_layer, layer, depth, prev, bb, chunk):
    b, t, _ = h3d.shape
    nc = t // chunk
    wide_chains = 4 if chunk >= 64 else 2

    def col(off):
        blk = off // MIX_W
        return pl.BlockSpec((bb, chunk, MIX_W), lambda bi, c, blk=blk: (bi, c, blk))

    def st_c(li):
        return pl.BlockSpec((None, bb, N_HEADS, HD, HD), lambda bi, c: (li, bi, 0, 0, 0))

    def st_n(li):
        return pl.BlockSpec((None, bb, N_HEADS, HD), lambda bi, c: (li, bi, 0, 0))

    c_specs, c_args, aliases = _carry_args(prev, N_MLSTM_IN, (1, 2, 3))
    return pl.pallas_call(
        _with_carried_outputs(functools.partial(_mlstm_kernel, bb=bb, chunk=chunk, wide_chains=wide_chains),
                              N_MLSTM_IN, len(c_args)),
        grid=(b // bb, nc),
        in_specs=[col(COL_BQ), col(COL_BK), col(COL_BV), col(COL_BO),
                  pl.BlockSpec((bb, chunk, 128), lambda bi, c: (bi, c, 0)),
                  pl.BlockSpec((1, HD), lambda bi, c: (0, 0)),
                  st_c(state_layer), st_n(state_layer), st_n(state_layer)] + c_specs,
        out_specs=(pl.BlockSpec((bb, chunk, MIX_W), lambda bi, c: (bi, c, 0)),
                   st_c(layer), st_n(layer), st_n(layer)),
        out_shape=(jax.ShapeDtypeStruct((b, t, MIX_W), F32),
                   jax.ShapeDtypeStruct((depth, b, N_HEADS, HD, HD), F32),
                   jax.ShapeDtypeStruct((depth, b, N_HEADS, HD), F32),
                   jax.ShapeDtypeStruct((depth, b, N_HEADS, HD), F32)),
        input_output_aliases=aliases,
        compiler_params=_cparams(("parallel", "arbitrary"), 40),
        name="mlstm",
    )(h3d, h3d, h3d, h3d, so3d, hg.reshape(1, HD), c0, n0, m0, *c_args)


N_GLA_IN = 8
GLA_SUB = 16


def _gla_kernel(q_ref, k_ref, v_ref, so_ref, w2_ref, b2_ref, hg_ref, s0_ref, o_ref, s_ref, *, bb, chunk,
                wide_seqs):
    L = chunk
    Lp = max(L, BF16_ROWS)
    sub = GLA_SUB

    @pl.when(pl.program_id(1) == 0)
    def _():
        s_ref[...] = s0_ref[...]

    tril = _tril_ones(Lp)
    hg = hg_ref[...]
    w2 = w2_ref[...].astype(BF16)
    b2 = b2_ref[...]
    q_scale = GLA_DK ** -0.5
    real_row = lax.broadcasted_iota(jnp.int32, (Lp, 1), 0) < L

    heads = range(N_HEADS)
    dkw = N_HEADS * GLA_DK
    head_of_lane = lax.broadcasted_iota(jnp.int32, (1, dkw), 1) >> GLA_DK_SHIFT
    zero_blk = jnp.zeros((GLA_DK, GLA_DV), F32)

    def run_seqs(idx):
        n = range(len(idx))
        s_old = [[s_ref[bi, h] for h in heads] for bi in idx]
        pre = [_dot(_pad_rows(so_ref[bi], Lp).astype(BF16), w2) + b2 for bi in idx]
        la = [jnp.where(real_row, _log_sigmoid(pre[u]) / GLA_TAU, 0.0) for u in n]
        bcs = [_dot_exact(tril, la[u]) for u in n]
        q = [_pad_rows(q_ref[bi], Lp) * q_scale for bi in idx]
        k = [_pad_rows(k_ref[bi], Lp) for bi in idx]
        vb = [_pad_rows(v_ref[bi], Lp).astype(BF16) for bi in idx]
        s_bd = [jnp.concatenate(
            [jnp.concatenate([s_old[u][h] if g == h else zero_blk for g in heads], axis=1) for h in heads],
            axis=0) for u in n]
        inter = [_dot((q[u] * jnp.exp(bcs[u])).astype(BF16), s_bd[u].astype(BF16)) for u in n]
        for i in range(Lp // sub):
            r0 = i * sub
            hi = r0 + sub
            rows = min(hi, L) - r0
            rr = (lax.broadcasted_iota(jnp.int32, (N_HEADS * sub, hi), 0) & (sub - 1)) + r0
            cc = lax.broadcasted_iota(jnp.int32, (N_HEADS * sub, hi), 1)
            base = [bcs[u][r0 - 1:r0, :] if i > 0 else jnp.zeros((1, dkw), F32) for u in n]
            qi = [q[u][r0:hi, :] * jnp.exp(bcs[u][r0:hi, :] - base[u]) for u in n]
            ke = [(k[u][0:hi, :] * jnp.exp(base[u] - bcs[u][0:hi, :])).astype(BF16) for u in n]
            q_heads = [jnp.concatenate([jnp.where(head_of_lane == h, qi[u], 0.0) for h in heads], axis=0)
                       for u in n]
            a = [jnp.where(cc <= rr, _dot_nt(q_heads[u].astype(BF16), ke[u]), 0.0).astype(BF16) for u in n]
            for u, bi in enumerate(idx):
                for h in heads:
                    vsl = slice(h * GLA_DV, (h + 1) * GLA_DV)
                    oi = inter[u][r0:hi, vsl] + _dot(a[u][h * sub:(h + 1) * sub, :], vb[u][0:hi, vsl])
                    o_ref[bi, r0:r0 + rows, vsl] = _rms_lanes(oi[0:rows], hg)
        b_end = [bcs[u][Lp - 1:Lp, :] for u in n]
        kd = [(k[u] * jnp.exp(b_end[u] - bcs[u])).astype(BF16) for u in n]
        upd = [_dot_tn(kd[u], vb[u]) for u in n]
        dcol = [_row_to_col(jnp.exp(b_end[u])) for u in n]
        for u, bi in enumerate(idx):
            for h in heads:
                s_ref[bi, h] = (dcol[u][h * GLA_DK:(h + 1) * GLA_DK, :] * s_old[u][h]
                                + upd[u][h * GLA_DK:(h + 1) * GLA_DK, h * GLA_DV:(h + 1) * GLA_DV])

    unroll = min(BATCH_UNROLL, bb)
    wide = min(wide_seqs, unroll)

    def group(g, carry):
        for u0 in range(0, unroll, wide):
            run_seqs([g * unroll + u0 + u for u in range(wide)])
        return carry

    if bb == unroll:
        group(0, 0)
    else:
        lax.fori_loop(0, bb // unroll, group, 0)


def _gla(h3d, so3d, w2pad, b2, hg, s0, state_layer, layer, depth, prev, bb, chunk):
    b, t, _ = h3d.shape
    nc = t // chunk

    def st(li):
        return pl.BlockSpec((None, bb, N_HEADS, GLA_DK, GLA_DV), lambda bi, c: (li, bi, 0, 0, 0))

    c_specs, c_args, aliases = _carry_args(prev, N_GLA_IN, (1,))
    return pl.pallas_call(
        _with_carried_outputs(functools.partial(_gla_kernel, bb=bb, chunk=chunk, wide_seqs=4),
                              N_GLA_IN, len(c_args)),
        grid=(b // bb, nc),
        in_specs=[pl.BlockSpec((bb, chunk, 256), lambda bi, c: (bi, c, COL_CQ // 256)),
                  pl.BlockSpec((bb, chunk, 256), lambda bi, c: (bi, c, COL_CK // 256)),
                  pl.BlockSpec((bb, chunk, MIX_W), lambda bi, c: (bi, c, COL_CV // MIX_W)),
                  pl.BlockSpec((bb, chunk, 128), lambda bi, c: (bi, c, 0)),
                  pl.BlockSpec((128, 256), lambda bi, c: (0, 0)),
                  pl.BlockSpec((1, 256), lambda bi, c: (0, 0)),
                  pl.BlockSpec((1, GLA_DV), lambda bi, c: (0, 0)),
                  st(state_layer)] + c_specs,
        out_specs=(pl.BlockSpec((bb, chunk, MIX_W), lambda bi, c: (bi, c, 0)), st(layer)),
        out_shape=(jax.ShapeDtypeStruct((b, t, MIX_W), F32),
                   jax.ShapeDtypeStruct((depth, b, N_HEADS, GLA_DK, GLA_DV), F32)),
        input_output_aliases=aliases,
        compiler_params=_cparams(("parallel", "arbitrary"), 40),
        name="gla",
    )(h3d, h3d, h3d, so3d, w2pad, b2.reshape(1, 256), hg.reshape(1, GLA_DV), s0, *c_args)


def _select_blocks(imp, qpos_col, ns):
    r = imp.shape[0]
    j = lax.broadcasted_iota(jnp.int32, (r, 128), 1)
    cur = qpos_col >> SEL_SHIFT
    valid = j <= cur
    forced = (j == 0) | (valid & (j >= cur - 1))
    val = jnp.where(forced, 1e4, jnp.where(valid, imp, -1e4))
    val = jnp.where(j < ns, val, -3e38)
    rank = jnp.zeros((r, 128), F32)
    for i in range(ns):
        ci = val[:, i:i + 1]
        beats = (ci > val) | ((ci == val) & (j > i))
        rank = rank + jnp.where(beats, 1.0, 0.0)
    return (rank < float(min(N_SEL, ns))) & (j < ns)


def _pair_matrix(nc):
    c = jnp.arange(nc)[:, None]
    j = jnp.arange(128)[None, :]
    return (c // (SEL_BLOCK // CMP_BLOCK) == j).astype(F32)


def _expand_matrix(tk):
    j = jnp.arange(128)[:, None]
    s = jnp.arange(tk)[None, :]
    return (s // SEL_BLOCK == j).astype(BF16)


NSA_TQ = 128
STACK_HEADS_MAX_KEYS = 1024


def _nsa_prompt_kernel(q_ref, kcmp_ref, vcmp_ref, kv_ref, win_ref, so_ref, e_ref, pair_ref, o_ref, *, t_len):
    tq = NSA_TQ
    nc = t_len // CMP_BLOCK
    ns = -(-t_len // SEL_BLOCK)
    i = pl.program_id(1)
    start = pl.multiple_of(i * tq, tq)
    qpos_col = start + lax.broadcasted_iota(jnp.int32, (tq, 1), 0)
    so = so_ref[...]
    q_all = jnp.concatenate([q_ref[:, h * HD:(h + 1) * HD] for h in range(N_HEADS)], axis=0)
    sc = _dot_nt(q_all, kcmp_ref[...].astype(BF16))
    qpos4 = start + (lax.broadcasted_iota(jnp.int32, (N_HEADS * tq, 1), 0) & (tq - 1))
    cend = lax.broadcasted_iota(jnp.int32, (N_HEADS * tq, nc), 1) * CMP_BLOCK + (CMP_BLOCK - 1)
    (ec,), invc = _softmax_parts([sc], [cend <= qpos4])
    pc = ec * invc
    o_cmp = _dot(pc.astype(BF16), vcmp_ref[...].astype(BF16))
    imp_c = pc[0:tq]
    for h in range(1, N_HEADS):
        imp_c = imp_c + pc[h * tq:(h + 1) * tq]
    imp = _dot_exact(imp_c, pair_ref[...])
    sel_b = jnp.where(_select_blocks(imp, qpos_col, ns), 1.0, 0.0).astype(BF16)
    band = WINDOW + tq
    kwin = win_ref[pl.ds(start, band), 0:HD]
    vwin = win_ref[pl.ds(start, band), HD:2 * HD]
    wpos = start - WINDOW + lax.broadcasted_iota(jnp.int32, (tq, band), 1)
    wmask = (wpos <= qpos_col) & (wpos > qpos_col - WINDOW) & (wpos >= 0)
    def shared_kv_attention(keys, values, mask):
        s_all = _dot_nt(q_all, keys)
        parts = [_softmax_parts([s_all[h * tq:(h + 1) * tq]], [mask]) for h in range(N_HEADS)]
        e_all = jnp.concatenate([es.astype(BF16) for (es,), _ in parts], axis=0)
        o_all = _dot(e_all, values)
        return [o_all[h * tq:(h + 1) * tq] * parts[h][1] for h in range(N_HEADS)]

    o_win = shared_kv_attention(kwin, vwin, wmask)
    for h in range(N_HEADS):
        sl = slice(h * HD, (h + 1) * HD)
        g0 = so[:, SM_DG + 3 * h:SM_DG + 3 * h + 1]
        g2 = so[:, SM_DG + 3 * h + 2:SM_DG + 3 * h + 3]
        o_ref[:, sl] = g0 * o_cmp[h * tq:(h + 1) * tq] + g2 * o_win[h]

    n_cls = -(-t_len // KEY_CLASS)
    cls = (start + tq - 1) // KEY_CLASS

    def run_selected(kl):
        sel_keys = _dot(sel_b, e_ref[:, 0:kl]) > 0.5
        kpos = lax.broadcasted_iota(jnp.int32, (tq, kl), 1)
        smask = sel_keys & (kpos <= qpos_col)
        ks = kv_ref[0:kl, 2 * HD:3 * HD]
        vs = kv_ref[0:kl, 3 * HD:4 * HD]
        o_sel = shared_kv_attention(ks, vs, smask) if kl <= STACK_HEADS_MAX_KEYS else None
        for h in range(N_HEADS):
            sl = slice(h * HD, (h + 1) * HD)
            if o_sel is None:
                (es,), invs = _softmax_parts([_dot_nt(q_ref[:, sl], ks)], [smask])
                o_h = _dot(es.astype(BF16), vs) * invs
            else:
                o_h = o_sel[h]
            g1 = so[:, SM_DG + 3 * h + 1:SM_DG + 3 * h + 2]
            o_ref[:, sl] = o_ref[:, sl] + g1 * o_h

    for c in range(n_cls):
        pl.when(cls == c)(functools.partial(run_selected, min((c + 1) * KEY_CLASS, t_len)))


def _nsa_prompt(nq, kcmp, vcmp, nkv, win_pad, so):
    b, t, _ = nq.shape
    tq = NSA_TQ
    nc = t // CMP_BLOCK
    e_mat = _expand_matrix(t)
    pair = _pair_matrix(nc)
    return pl.pallas_call(
        functools.partial(_nsa_prompt_kernel, t_len=t),
        grid=(b, t // tq),
        in_specs=[pl.BlockSpec((None, tq, MIX_W), lambda bi, i: (bi, i, 0)),
                  pl.BlockSpec((None, nc, HD), lambda bi, i: (bi, 0, 0)),
                  pl.BlockSpec((None, nc, HD), lambda bi, i: (bi, 0, 0)),
                  pl.BlockSpec((None, t, 768), lambda bi, i: (bi, 0, 0)),
                  pl.BlockSpec((None, t + WINDOW, 256), lambda bi, i: (bi, 0, 0)),
                  pl.BlockSpec((None, tq, 128), lambda bi, i: (bi, i, 0)),
                  pl.BlockSpec((128, t), lambda bi, i: (0, 0)),
                  pl.BlockSpec((nc, 128), lambda bi, i: (0, 0))],
        out_specs=pl.BlockSpec((None, tq, MIX_W), lambda bi, i: (bi, i, 0)),
        out_shape=jax.ShapeDtypeStruct((b, t, MIX_W), F32),
        compiler_params=_cparams(("parallel", "arbitrary"), 48),
        name="nsa_prompt",
    )(nq, kcmp, vcmp, nkv, win_pad, so, e_mat, pair)


def _nsa_sample_stages(q_ref, kvn_ref, winn_ref, so_ref, win_ref, wk_ref, wv_ref, kcg_ref, e_ref, pair_ref,
                       pages, o_ref, wout_ref, cmp_k, cmp_v, n_pages, t_new):
    past = n_pages * PAGE_SIZE
    nc = past // CMP_BLOCK
    ns = -(-(past + t_new) // SEL_BLOCK)
    per_page = PAGE_SIZE // CMP_BLOCK
    n_grp = n_pages // PAGE_GROUP
    gw = PAGE_GROUP * PAGE_SIZE
    rows = N_HEADS * t_new
    st = {}

    def page_rows(p, j):
        return pages[p][pl.ds(j, PAGE_SIZE, stride=4), :]

    def group_rows(g, j):
        return jnp.concatenate([page_rows(g * PAGE_GROUP + u, j).astype(BF16) for u in range(PAGE_GROUP)], axis=0)

    def compress():
        for p in range(n_pages):
            kc = page_rows(p, 0).reshape(per_page, CMP_BLOCK, HD)
            vc = page_rows(p, 1).reshape(per_page, CMP_BLOCK, HD)
            cmp_k[p * per_page:(p + 1) * per_page, :] = jnp.sum(kc * wk_ref[...][None], axis=1)
            cmp_v[p * per_page:(p + 1) * per_page, :] = jnp.sum(vc * wv_ref[...][None], axis=1)
        kcmp = _rms_lanes(cmp_k[...], kcg_ref[...]).astype(BF16)
        st["vcmp"] = cmp_v[...].astype(BF16)
        q_all = jnp.concatenate([q_ref[:, h * HD:(h + 1) * HD].astype(F32) for h in range(N_HEADS)], axis=0)
        st["q_all"] = q_all
        st["q_b"] = q_all.astype(BF16)
        trow = lax.broadcasted_iota(jnp.int32, (rows, 1), 0) & (t_new - 1)
        st["qpos4"] = past + trow
        st["kvn"] = kvn_ref[...].astype(F32)
        st["npos"] = past + lax.broadcasted_iota(jnp.int32, (rows, t_new), 1)
        st["causal_new"] = st["npos"] <= st["qpos4"]
        st["sc"] = _dot_nt(st["q_b"], kcmp)

    def selected_scores():
        s_list = [_dot_nt(st["q_b"], group_rows(g, 2)) for g in range(n_grp)]
        s_list.append(_dot_nt(st["q_all"], st["kvn"][:, 2 * HD:3 * HD]))
        st["s_list"] = s_list

    def compressed():
        cend = lax.broadcasted_iota(jnp.int32, (rows, nc), 1) * CMP_BLOCK + (CMP_BLOCK - 1)
        (ec,), invc = _softmax_parts([st["sc"]], [cend <= st["qpos4"]])
        pc = ec * invc
        imp_c = pc[0:t_new]
        for h in range(1, N_HEADS):
            imp_c = imp_c + pc[h * t_new:(h + 1) * t_new]
        st["imp"] = _dot_exact(imp_c, pair_ref[...])
        st["o_cmp"] = _dot(pc.astype(BF16), st["vcmp"])

    def window():
        qpos4, npos, kvn = st["qpos4"], st["npos"], st["kvn"]
        w_buf = win_ref.shape[0] // 2
        kwb = win_ref[pl.ds(0, w_buf, stride=2), :].astype(BF16)
        vwb = win_ref[pl.ds(1, w_buf, stride=2), :].astype(BF16)
        wpos = past - w_buf + lax.broadcasted_iota(jnp.int32, (rows, w_buf), 1)
        wmask = (wpos <= qpos4) & (wpos > qpos4 - WINDOW) & (wpos >= 0)
        sw_list = [_dot_nt(st["q_b"], kwb), _dot_nt(st["q_all"], kvn[:, 4 * HD:5 * HD])]
        ew, invw = _softmax_parts(sw_list, [wmask, st["causal_new"] & (npos > qpos4 - WINDOW)])
        st["o_win"] = (_dot(ew[0].astype(BF16), vwb) + _dot(ew[1], kvn[:, 5 * HD:6 * HD])) * invw
        keep = 2 * (w_buf - t_new)
        wout_ref[0:keep, :] = win_ref[2 * t_new:2 * w_buf, :]
        wout_ref[keep:2 * w_buf, :] = winn_ref[...]

    def select():
        qpos_col = past + lax.broadcasted_iota(jnp.int32, (t_new, 1), 0)
        sel = _select_blocks(st["imp"], qpos_col, ns)
        self32 = jnp.where(sel, 1.0, 0.0)
        sel_past = _dot(self32.astype(BF16), e_ref[...])
        sel_past4 = jnp.concatenate([sel_past] * N_HEADS, axis=0) > 0.5
        new_blk = past // SEL_BLOCK
        sel_new = jnp.concatenate([self32[:, new_blk:new_blk + 1]] * N_HEADS, axis=0) > 0.5
        m_list = []
        for g in range(n_grp):
            kpos = g * gw + lax.broadcasted_iota(jnp.int32, (rows, gw), 1)
            m_list.append(sel_past4[:, g * gw:(g + 1) * gw] & (kpos <= st["qpos4"]))
        m_list.append(sel_new & st["causal_new"])
        st["m_list"] = m_list

    def selected():
        es, invs = _softmax_parts(st["s_list"], st["m_list"])
        o_sel = _dot(es[n_grp], st["kvn"][:, 3 * HD:4 * HD])
        for g in range(n_grp):
            o_sel = o_sel + _dot(es[g].astype(BF16), group_rows(g, 3))
        o_sel = o_sel * invs
        so = so_ref[...]
        for h in range(N_HEADS):
            r = slice(h * t_new, (h + 1) * t_new)
            g0 = so[:, SM_DG + 3 * h:SM_DG + 3 * h + 1]
            g1 = so[:, SM_DG + 3 * h + 1:SM_DG + 3 * h + 2]
            g2 = so[:, SM_DG + 3 * h + 2:SM_DG + 3 * h + 3]
            o_ref[:, h * HD:(h + 1) * HD] = g0 * st["o_cmp"][r] + g1 * o_sel[r] + g2 * st["o_win"][r]

    return [compress, selected_scores, compressed, window, select, selected]


N_SAMPLE_IN = 15


def _sample_attn_kernel(pt_ref, fq_ref, fk_ref, fv_ref, so_ref, spread_ref, nq_ref, nkv_ref, winn_ref,
                        win_ref, wk_ref, wv_ref, kcg_ref, e_ref, pair_ref, *rest, n_pages, t_new):
    del pt_ref
    kv_pages = rest[:n_pages]
    lf_pages = rest[n_pages:2 * n_pages]
    nsa_pages = rest[2 * n_pages:3 * n_pages]
    o_fox, o_nsa, wout_ref, lf_scr, cmp_k, cmp_v = rest[3 * n_pages:3 * n_pages + 6]
    fox = _fox_sample_stages(fq_ref, fk_ref, fv_ref, so_ref, spread_ref, kv_pages, lf_pages, o_fox, lf_scr,
                             n_pages, t_new)
    nsa = _nsa_sample_stages(nq_ref, nkv_ref, winn_ref, so_ref, win_ref, wk_ref, wv_ref, kcg_ref, e_ref,
                             pair_ref, nsa_pages, o_nsa, wout_ref, cmp_k, cmp_v, n_pages, t_new)
    for n_stage, f_stage in zip(nsa, fox):
        n_stage()
        f_stage()


def _sample_attn(layer, depth, prev, page_table, fq, fk, fv, nq, nkv, win_new, so, fox_kv4, fox_lf4, nsa_kv4,
                 win_state4, wk_b, wv_b, kcg):
    b, t, _ = nq.shape
    n_pages = page_table.shape[1]
    past = n_pages * PAGE_SIZE
    nc = past // CMP_BLOCK
    w2 = win_state4.shape[2]
    e_mat = _expand_matrix(past)
    pair = _pair_matrix(nc)

    def tok(rows, width):
        return pl.BlockSpec((None, rows, width), lambda bi, pt: (bi, 0, 0))

    def const(shape):
        return pl.BlockSpec(shape, lambda bi, pt: (0, 0))

    def page_spec(rows, p):
        return pl.BlockSpec((None, None, rows, 128), lambda bi, pt, p=p: (layer, pt[bi, p], 0, 0))

    in_specs = [tok(t, MIX_W), tok(t, MIX_W), tok(t, MIX_W), tok(t, 128), const((PAGE_SIZE, 2 * PAGE_SIZE)),
                tok(t, MIX_W), tok(t, 768), tok(2 * t, HD),
                pl.BlockSpec((None, None, w2, 128), lambda bi, pt: (layer, bi, 0, 0)),
                const((CMP_BLOCK, 128)), const((CMP_BLOCK, 128)), const((1, 128)),
                const((128, past)), const((nc, 128))]
    in_specs += [page_spec(PAGE_SIZE * 2 * N_HEADS, p) for p in range(n_pages)]
    in_specs += [page_spec(N_HEADS, p) for p in range(n_pages)]
    in_specs += [page_spec(PAGE_SIZE * 4, p) for p in range(n_pages)]
    n_in = N_SAMPLE_IN + 3 * n_pages
    c_specs, c_args, aliases = _carry_args(prev, n_in, (2,))
    out_tok = pl.BlockSpec((None, t, MIX_W), lambda bi, pt: (bi, 0, 0))
    grid_spec = pltpu.PrefetchScalarGridSpec(
        num_scalar_prefetch=1, grid=(b,), in_specs=in_specs + c_specs,
        out_specs=(out_tok, out_tok, pl.BlockSpec((None, None, w2, 128), lambda bi, pt: (layer, bi, 0, 0))),
        scratch_shapes=[pltpu.VMEM((n_pages * N_HEADS, PAGE_SIZE), F32),
                        pltpu.VMEM((nc, HD), F32), pltpu.VMEM((nc, HD), F32)])
    return pl.pallas_call(
        _with_carried_outputs(functools.partial(_sample_attn_kernel, n_pages=n_pages, t_new=t), n_in, len(c_args)),
        grid_spec=grid_spec,
        out_shape=(jax.ShapeDtypeStruct((b, t, MIX_W), F32), jax.ShapeDtypeStruct((b, t, MIX_W), F32),
                   jax.ShapeDtypeStruct((depth, b, w2, 128), F32)),
        input_output_aliases=aliases,
        compiler_params=_cparams(("arbitrary",), 56),
        name="sample_attn",
    )(page_table, fq, fk, fv, so, _spread_matrix(), nq, nkv, win_new, win_state4, wk_b, wv_b, kcg, e_mat, pair,
      *([fox_kv4] * n_pages), *([fox_lf4] * n_pages), *([nsa_kv4] * n_pages), *c_args)


OUT_TM = 256


def _out_kernel(x_ref, oa_ref, ob_ref, oc_ref, od_ref, z_ref, g0_ref, g1_ref, g2_ref, g3_ref,
                wb_ref, wo_ref, gn_ref, y_ref, xn_ref):
    acc = None
    for g, (o_r, g_r) in enumerate(((oa_ref, g0_ref), (ob_ref, g1_ref), (oc_ref, g2_ref), (od_ref, g3_ref))):
        z = z_ref[:, g * MIX_W:(g + 1) * MIX_W]
        br = (o_r[...] * (z * _sigmoid(z))).astype(BF16)
        term = (jnp.tanh(g_r[...]) + 1.0) * _dot(br, wb_ref[g])
        acc = term if acc is None else acc + term
    y = x_ref[...] + _dot(acc.astype(BF16), wo_ref[...])
    y_ref[...] = y
    xn_ref[...] = _rms_lanes(y, gn_ref[...]).astype(BF16)


def _out_proj(x2d, oa, ob, oc, od, h2d, wb_all, wo_all, layer, next_norm_g):
    m = x2d.shape[0]
    tm = OUT_TM

    def rows(width):
        return pl.BlockSpec((tm, width), lambda i: (i, 0))

    def gate(g):
        return pl.BlockSpec((tm, D_MODEL), lambda i, g=g: (i, COL_GATE // D_MODEL + g))

    single = pl.Buffered(1)
    return pl.pallas_call(
        _out_kernel,
        grid=(m // tm,),
        in_specs=[rows(D_MODEL), rows(MIX_W), rows(MIX_W), rows(MIX_W), rows(MIX_W),
                  rows(D_MODEL), gate(0), gate(1), gate(2), gate(3),
                  pl.BlockSpec((None, N_HEADS, MIX_W, D_MODEL), lambda i: (layer, 0, 0, 0), pipeline_mode=single),
                  pl.BlockSpec((None, D_MODEL, D_MODEL), lambda i: (layer, 0, 0), pipeline_mode=single),
                  pl.BlockSpec((1, D_MODEL), lambda i: (0, 0))],
        out_specs=(rows(D_MODEL), rows(D_MODEL)),
        out_shape=(jax.ShapeDtypeStruct((m, D_MODEL), F32), jax.ShapeDtypeStruct((m, D_MODEL), BF16)),
        compiler_params=_cparams(("parallel",), 56),
        name="out_proj",
    )(x2d, oa, ob, oc, od, h2d, h2d, h2d, h2d, h2d, wb_all, wo_all, next_norm_g.reshape(1, D_MODEL))


def _layer_params(l, p):
    zeros = jnp.zeros((2, HD), F32)
    gains = jnp.concatenate([p["fox_qg"][l][None], p["fox_kg"][l][None], p["nsa_qg"][l][None],
                             p["nsa_ksg"][l][None], p["nsa_kwg"][l][None], p["nsa_kcg"][l][None], zeros], axis=0)
    bias = jnp.concatenate([p["fox_bf"][l], p["mlstm_bi"][l], p["mlstm_bf"][l], jnp.zeros((GLA_RANK,), F32),
                            p["nsa_bg"][l].reshape(-1), jnp.zeros((128 - SM_DG - 12,), F32)]).reshape(1, 128)
    w2pad = jnp.zeros((128, 256), F32).at[SM_CA:SM_CA + GLA_RANK].set(p["gla_w2"][l])
    return dict(
        norm_g=p["norm_g"][l], gains=gains, bias=bias,
        wk_b=jnp.broadcast_to(p["nsa_wk"][l][:, None], (CMP_BLOCK, 128)),
        wv_b=jnp.broadcast_to(p["nsa_wv"][l][:, None], (CMP_BLOCK, 128)),
        kcg=p["nsa_kcg"][l].reshape(1, HD),
        mlstm_hg=p["mlstm_hg"][l], w2pad=w2pad, gla_b2=p["gla_b2"][l], gla_hg=p["gla_hg"][l])


def _layer(x3d, xn, lp, shared, st, prev, *, is_prompt, layer, depth):
    b, t, _ = x3d.shape
    m = b * t
    x2d = x3d.reshape(m, D_MODEL)
    if xn is None:
        xn = _rmsnorm(x2d, lp["norm_g"])
    h2d = _in_proj(xn, shared["w_in"], layer)
    h3d = h2d.reshape(b, t, N_PROJ)
    (foxkv, nsakv, win, so, fq, fk, fv, nq, nkv, kcmp, vcmp) = _prep(
        h2d, lp["gains"], lp["bias"], lp["wk_b"], lp["wv_b"], layer, depth,
        None if prev is None else (prev["fox_kv"], prev["nsa_kv"]))
    so3 = so.reshape(b, t, 128)
    fq3, fk3, fv3 = fq.reshape(b, t, MIX_W), fk.reshape(b, t, MIX_W), fv.reshape(b, t, MIX_W)
    nq3 = nq.reshape(b, t, MIX_W)
    nkv3 = nkv.reshape(b, t, 768)
    chunk = min(64, t)
    carried = dict(fox_kv=foxkv, nsa_kv=nsakv)
    if is_prompt:
        lf_t = so3[:, :, SM_AF:SM_AF + N_HEADS].transpose(0, 2, 1).reshape(b * N_HEADS, t)
        f_row = _cumsum_rows(lf_t).reshape(b, N_HEADS, t)
        f_col = f_row.transpose(0, 2, 1)
        o_a = _fox_prompt(fq3, fk3, fv3, f_col, f_row)
        win_pad = jnp.pad(nkv3[:, :, 4 * HD:], ((0, 0), (WINDOW, 0), (0, 0)))
        o_d = _nsa_prompt(nq3, kcmp.reshape(b, t // CMP_BLOCK, HD), vcmp.reshape(b, t // CMP_BLOCK, HD),
                          nkv3, win_pad, so3)
        win_out = win.reshape(b, t, 2, HD)[:, -min(WINDOW, t):]
        bb = min(BATCH_UNROLL, b)
    else:
        o_a, o_d, win_out = _sample_attn(
            layer, depth, None if prev is None else (prev["win"],), shared["page_table"],
            fq3, fk3, fv3, nq3, nkv3, win.reshape(b, 2 * t, HD), so3,
            shared["fox_kv4"], shared["fox_lf4"], shared["nsa_kv4"], shared["nsa_win4"],
            lp["wk_b"], lp["wv_b"], lp["kcg"])
        bb = 2 * BATCH_UNROLL
    o_b, c_new, n_new, m_new = _mlstm(
        h3d, so3, lp["mlstm_hg"], st["C"], st["n"], st["m"], st["layer"], layer, depth,
        None if prev is None else (prev["C"], prev["n"], prev["m"]), bb, chunk)
    o_c, s_new = _gla(h3d, so3, lp["w2pad"], lp["gla_b2"], lp["gla_hg"], st["S"], st["layer"], layer, depth,
                      None if prev is None else (prev["S"],), bb, chunk)
    y2d, xn_next = _out_proj(x2d, o_a.reshape(m, MIX_W), o_b.reshape(m, MIX_W), o_c.reshape(m, MIX_W),
                             o_d.reshape(m, MIX_W), h2d, shared["w_branch"], shared["w_out"], layer,
                             lp["next_norm_g"])
    carried.update(win=win_out, C=c_new, n=n_new, m=m_new, S=s_new)
    return y2d.reshape(b, t, D_MODEL), xn_next, carried, so3[:, :, SM_AF:SM_AF + N_HEADS]


def kernel(x_prompt, x_sample, cache_fox_kv, cache_fox_logf, cache_nsa_kv, page_table, state_nsa_win, state_mlstm_C, state_mlstm_n, state_mlstm_m, state_gla_S, norm_g, w_in, fox_qg, fox_kg, fox_bf, mlstm_bi, mlstm_bf, mlstm_hg, gla_w2, gla_b2, gla_hg, nsa_qg, nsa_kcg, nsa_ksg, nsa_kwg, nsa_wk, nsa_wv, nsa_bg, w_branch, w_out):
    params = dict(norm_g=norm_g, fox_qg=fox_qg, fox_kg=fox_kg, fox_bf=fox_bf, mlstm_bi=mlstm_bi,
                  mlstm_bf=mlstm_bf, mlstm_hg=mlstm_hg, gla_w2=gla_w2, gla_b2=gla_b2, gla_hg=gla_hg,
                  nsa_qg=nsa_qg, nsa_kcg=nsa_kcg, nsa_ksg=nsa_ksg, nsa_kwg=nsa_kwg, nsa_wk=nsa_wk,
                  nsa_wv=nsa_wv, nsa_bg=nsa_bg)
    depth = w_in.shape[0]
    bp, tp, _ = x_prompt.shape
    db, ts, _ = x_sample.shape
    n_pool = cache_fox_kv.shape[1]
    w_buf = state_nsa_win.shape[2]
    shared = dict(
        w_in=_permute_w_in(w_in), w_branch=(0.5 * w_branch).astype(BF16), w_out=w_out.astype(BF16),
        page_table=page_table,
        fox_kv4=cache_fox_kv.reshape(depth, n_pool, PAGE_SIZE * 2 * N_HEADS, HD),
        fox_lf4=cache_fox_logf.transpose(0, 1, 3, 2),
        nsa_kv4=cache_nsa_kv.reshape(depth, n_pool, PAGE_SIZE * 4, HD),
        nsa_win4=state_nsa_win.reshape(depth, db, w_buf * 2, HD))
    st_p = dict(C=jnp.zeros((1, bp, N_HEADS, HD, HD), F32), n=jnp.zeros((1, bp, N_HEADS, HD), F32),
                m=jnp.zeros((1, bp, N_HEADS, HD), F32), S=jnp.zeros((1, bp, N_HEADS, GLA_DK, GLA_DV), F32),
                layer=0)
    m_s = jnp.broadcast_to(state_mlstm_m[..., None], state_mlstm_m.shape + (HD,))
    y_p, y_s = x_prompt, x_sample
    xn_p = xn_s = None
    prev_p = prev_s = None
    lf_p, lf_s, win_p = [], [], []
    for l in range(depth):
        lp = _layer_params(l, params)
        lp["next_norm_g"] = norm_g[min(l + 1, depth - 1)]
        st_s = dict(C=state_mlstm_C, n=state_mlstm_n, m=m_s, S=state_gla_S, layer=l)
        y_p, xn_p, prev_p, lf = _layer(y_p, xn_p, lp, shared, st_p, prev_p, is_prompt=True, layer=l, depth=depth)
        lf_p.append(lf)
        win_p.append(prev_p["win"])
        y_s, xn_s, prev_s, lf = _layer(y_s, xn_s, lp, shared, st_s, prev_s, is_prompt=False, layer=l, depth=depth)
        lf_s.append(lf)

    def finish(c, b, t, win, lf):
        return (c["fox_kv"].reshape(depth, b, t, 2, N_HEADS, HD), jnp.stack(lf, axis=0),
                c["nsa_kv"].reshape(depth, b, t, 4, HD), win,
                c["C"], c["n"], c["m"][..., 0], c["S"])

    out_p = finish(prev_p, bp, tp, jnp.stack(win_p, axis=0), lf_p)
    out_s = finish(prev_s, db, ts, prev_s["win"].reshape(depth, db, w_buf, 2, HD), lf_s)
    return (y_p, y_s) + out_p + out_s
```
